```python
import math
import jax
import jax.numpy as jnp
from jax import lax
import numpy as np

D_MODEL = 1024
BATCH = 8
SEQ = 4096
DEPTH = 4

CTX_LEN = 256
GRID_W = 64
HEAD_DIM = 64
ROPE_BASE = 10000.0
N_MOD = 9
FFN_DIM = 2816
A_HEADS = 4
A_VDIM = 2 * HEAD_DIM
B_HEADS = 4
B_KDIM = 64
B_VDIM = 128
GATE_RANK = 16
GATE_TAU = 16.0
GLA_CHUNK = 64
C_HEADS = 8
C_KV_HEADS = 2
WINDOW = 128
Q_BLOCK = 128
KEY_SPAN = Q_BLOCK + 2 * WINDOW
NEG_INF = -1e30
MIX_SIZES = (A_HEADS * 2 * HEAD_DIM, A_HEADS * 2 * HEAD_DIM, A_HEADS * A_VDIM,
             B_HEADS * B_KDIM, B_HEADS * B_KDIM, B_HEADS * B_VDIM, 2 * GATE_RANK, B_HEADS * B_VDIM,
             C_HEADS * HEAD_DIM, C_KV_HEADS * HEAD_DIM, C_KV_HEADS * HEAD_DIM, 3 * D_MODEL)
IN_COLS = sum(MIX_SIZES)
SPLITS = tuple(int(s) for s in np.cumsum(MIX_SIZES)[:-1])

kernel_name = 'hybrid_prefix_dit_block'


def _rmsnorm(x, g, eps=1e-6):
    xf = x.astype(jnp.float32)
    y = xf * lax.rsqrt(jnp.mean(xf * xf, axis=-1, keepdims=True) + eps)
    return (y * g.astype(jnp.float32)).astype(x.dtype)


def _modulate(h, shift, scale):
    return h * (1.0 + scale) + shift


def _swiglu(h, w_up, w_down):
    u, v = jnp.split(h @ w_up, 2, axis=-1)
    return (jax.nn.silu(u) * v) @ w_down


def _rope_tables(n_tok):
    rows = n_tok // GRID_W
    row = jnp.repeat(jnp.arange(rows, dtype=jnp.float32), GRID_W)
    col = jnp.tile(jnp.arange(GRID_W, dtype=jnp.float32), rows)
    n_freq = HEAD_DIM // 4
    freqs = jnp.power(ROPE_BASE, -jnp.arange(n_freq, dtype=jnp.float32) / n_freq)
    ar = row[:, None] * freqs
    ac = col[:, None] * freqs
    ang = jnp.concatenate([ar, ar, ac, ac], axis=-1)
    return jnp.cos(ang), jnp.sin(ang)


def _apply_rope(x, cos, sin):
    bshape = (x.shape[1],) + (1,) * (x.ndim - 3) + (HEAD_DIM,)
    cos = cos.reshape(bshape)
    sin = sin.reshape(bshape)
    xs = x.reshape(x.shape[:-1] + (2, 2, HEAD_DIM // 4))
    rot = jnp.concatenate([-xs[..., 1:, :], xs[..., :1, :]], axis=-2).reshape(x.shape)
    return (x * cos + rot * sin).astype(x.dtype)


def _diff_attend(q, k, v, lam):
    s = jnp.einsum('bqhmd,bkhmd->bhmqk', q, k).astype(jnp.float32) * (HEAD_DIM ** -0.5)
    p = jax.nn.softmax(s, axis=-1)
    a = p[:, :, 0] - lam * p[:, :, 1]
    return jnp.einsum('bhqk,bkhe->bqhe', a.astype(v.dtype), v)


def _diff_attention_latent(q, k_all, v_all, lam):
    bsz, n = q.shape[:2]
    nb = n // Q_BLOCK
    qb = jnp.moveaxis(q.reshape(bsz, nb, Q_BLOCK, A_HEADS, 2, HEAD_DIM), 1, 0)
    o = lax.map(lambda blk: _diff_attend(blk, k_all, v_all, lam), qb)
    return jnp.moveaxis(o, 0, 1).reshape(bsz, n, A_HEADS, A_VDIM)


def _gla_inputs(p, gate_w, gate_b):
    bsz, n = p[3].shape[:2]

    def heads(t, dh):
        return t.reshape(bsz, n, B_HEADS, dh).transpose(0, 2, 1, 3).astype(jnp.float32)

    q = heads(p[3], B_KDIM)
    k = heads(p[4], B_KDIM)
    v = heads(p[5], B_VDIM)
    log_a = []
    for d in range(2):
        z = p[6][..., d * GATE_RANK:(d + 1) * GATE_RANK] @ gate_w[d] + gate_b[d]
        log_a.append(heads(jax.nn.log_sigmoid(z.astype(jnp.float32)) / GATE_TAU, B_KDIM))
    return q, k, v, log_a


def _gla_states(k, v, log_a, s0):
    bsz, h, n, dk = k.shape
    nc = n // GLA_CHUNK
    kc = k.reshape(bsz, h, nc, GLA_CHUNK, dk)
    vc = v.reshape(bsz, h, nc, GLA_CHUNK, B_VDIM)
    cum = jnp.cumsum(log_a.reshape(bsz, h, nc, GLA_CHUNK, dk), axis=3)
    last = cum[:, :, :, -1]
    kv = jnp.einsum('bhnck,bhncv->bhnkv', kc * jnp.exp(last[:, :, :, None] - cum), vc)

    def step(state, inp):
        dec, upd = inp
        return dec[..., None] * state + upd, state

    final, starts = lax.scan(step, s0, (jnp.moveaxis(jnp.exp(last), 2, 0), jnp.moveaxis(kv, 2, 0)))
    return cum, jnp.moveaxis(starts, 0, 2), final


def _gla_outputs(q, k, v, cum, starts):
    bsz, h, n, dk = q.shape
    nc = n // GLA_CHUNK
    qe = q.reshape(bsz, h, nc, GLA_CHUNK, dk) * (dk ** -0.5) * jnp.exp(cum)
    ke = k.reshape(bsz, h, nc, GLA_CHUNK, dk) * jnp.exp(-cum)
    vc = v.reshape(bsz, h, nc, GLA_CHUNK, B_VDIM)
    idx = jnp.arange(GLA_CHUNK)
    earlier = idx[:, None] >= idx[None, :]
    att = jnp.where(earlier, jnp.einsum('bhnik,bhnjk->bhnij', qe, ke), 0.0)
    o = jnp.einsum('bhnij,bhnjv->bhniv', att, vc) + jnp.einsum('bhnik,bhnkv->bhniv', qe, starts)
    return o.reshape(bsz, h, n, B_VDIM)


def _gla_direction(lat, ctx_in, la_lat, la_ctx, reverse, with_ctx):
    q, k, v = lat
    qc, kc, vc = ctx_in
    if reverse:
        q, k, v, la_lat, qc, kc, vc, la_ctx = [jnp.flip(t, axis=2) for t in (q, k, v, la_lat, qc, kc, vc, la_ctx)]
    s0 = jnp.zeros(k.shape[:2] + (B_KDIM, B_VDIM), jnp.float32)
    cum_c, starts_c, final_c = _gla_states(kc, vc, la_ctx, s0)
    cum, starts, _ = _gla_states(k, v, la_lat, final_c)
    o = _gla_outputs(q, k, v, cum, starts)
    o_c = _gla_outputs(qc, kc, vc, cum_c, starts_c) if with_ctx else None
    if reverse:
        o = jnp.flip(o, axis=2)
        o_c = jnp.flip(o_c, axis=2) if with_ctx else None
    return o, o_c


def _window_attention_latent(q, k, v, k_ctx, v_ctx, sink):
    bsz, n = q.shape[:2]
    nb = n // Q_BLOCK
    g = C_HEADS // C_KV_HEADS
    n_ctx = k_ctx.shape[1]
    qb = jnp.moveaxis(q.reshape(bsz, nb, Q_BLOCK, C_KV_HEADS, g, HEAD_DIM), 1, 0)
    pad = ((0, 0), (WINDOW, WINDOW), (0, 0), (0, 0))
    kp = jnp.pad(k, pad)
    vp = jnp.pad(v, pad)
    sink_row = jnp.broadcast_to(sink.astype(jnp.float32).reshape(1, C_KV_HEADS, g, 1, 1),
                                (bsz, C_KV_HEADS, g, Q_BLOCK, 1))
    qi = jnp.arange(Q_BLOCK)[:, None]
    kj = jnp.arange(KEY_SPAN)[None, :]
    rel = kj - qi
    scale = HEAD_DIM ** -0.5

    def one_block(args):
        blk, qblk = args
        kb = lax.dynamic_slice_in_dim(kp, blk * Q_BLOCK, KEY_SPAN, axis=1)
        vb = lax.dynamic_slice_in_dim(vp, blk * Q_BLOCK, KEY_SPAN, axis=1)
        pos = blk * Q_BLOCK - WINDOW + kj
        valid = (rel >= 0) & (rel <= 2 * WINDOW) & (pos >= 0) & (pos < n)
        s_loc = jnp.einsum('bqhgd,bkhd->bhgqk', qblk, kb).astype(jnp.float32) * scale
        s_loc = jnp.where(valid, s_loc, NEG_INF)
        s_ctx = jnp.einsum('bqhgd,bkhd->bhgqk', qblk, k_ctx).astype(jnp.float32) * scale
        p = jax.nn.softmax(jnp.concatenate([s_loc, s_ctx, sink_row], axis=-1), axis=-1).astype(v.dtype)
        return (jnp.einsum('bhgqk,bkhd->bqhgd', p[..., :KEY_SPAN], vb)
                + jnp.einsum('bhgqk,bkhd->bqhgd', p[..., KEY_SPAN:KEY_SPAN + n_ctx], v_ctx))

    o = lax.map(one_block, (jnp.arange(nb), qb))
    return jnp.moveaxis(o, 0, 1).reshape(bsz, n, C_HEADS * HEAD_DIM)


def _window_attention_ctx(q, k, v, sink):
    bsz, n = q.shape[:2]
    g = C_HEADS // C_KV_HEADS
    qg = q.reshape(bsz, n, C_KV_HEADS, g, HEAD_DIM)
    s = jnp.einsum('bqhgd,bkhd->bhgqk', qg, k).astype(jnp.float32) * (HEAD_DIM ** -0.5)
    sink_row = jnp.broadcast_to(sink.astype(jnp.float32).reshape(1, C_KV_HEADS, g, 1, 1), (bsz, C_KV_HEADS, g, n, 1))
    p = jax.nn.softmax(jnp.concatenate([s, sink_row], axis=-1), axis=-1)[..., :n].astype(v.dtype)
    return jnp.einsum('bhgqk,bkhd->bqhgd', p, v).reshape(bsz, n, C_HEADS * HEAD_DIM)


def _token_mix(hx, hc, w_in, diff_lambda, diff_subln, gla_gate_w, gla_gate_b, gla_norm, swa_sink,
               w_br_a, w_br_b, w_br_c, w_out, lam_init, cos, sin, with_ctx):
    bsz, n, _ = hx.shape
    n_ctx = hc.shape[1]
    px = jnp.split(hx @ w_in, SPLITS, axis=-1)
    pc = jnp.split(hc @ w_in, SPLITS, axis=-1)

    aq = _apply_rope(px[0].reshape(bsz, n, A_HEADS, 2, HEAD_DIM), cos, sin)
    ak = _apply_rope(px[1].reshape(bsz, n, A_HEADS, 2, HEAD_DIM), cos, sin)
    av = px[2].reshape(bsz, n, A_HEADS, A_VDIM)
    caq = pc[0].reshape(bsz, n_ctx, A_HEADS, 2, HEAD_DIM)
    cak = pc[1].reshape(bsz, n_ctx, A_HEADS, 2, HEAD_DIM)
    cav = pc[2].reshape(bsz, n_ctx, A_HEADS, A_VDIM)
    lp = diff_lambda.astype(jnp.float32)
    lam = jnp.exp(jnp.sum(lp[0] * lp[1])) - jnp.exp(jnp.sum(lp[2] * lp[3])) + lam_init

    def diff_out(o):
        return (_rmsnorm(o, diff_subln) * (1.0 - lam_init)).reshape(o.shape[0], o.shape[1], A_HEADS * A_VDIM)

    o_a = diff_out(_diff_attention_latent(aq, jnp.concatenate([ak, cak], axis=1),
                                          jnp.concatenate([av, cav], axis=1), lam))

    q, k, v, la = _gla_inputs(px, gla_gate_w, gla_gate_b)
    qc, kc, vc, lac = _gla_inputs(pc, gla_gate_w, gla_gate_b)
    of, ofc = _gla_direction((q, k, v), (qc, kc, vc), la[0], lac[0], False, with_ctx)
    ob, obc = _gla_direction((q, k, v), (qc, kc, vc), la[1], lac[1], True, with_ctx)

    def gla_out(o, r):
        o = _rmsnorm(o.transpose(0, 2, 1, 3), gla_norm)
        return o.reshape(o.shape[0], o.shape[1], B_HEADS * B_VDIM).astype(r.dtype) * jax.nn.silu(r)

    o_b = gla_out(of + ob, px[7])

    cq = _apply_rope(px[8].reshape(bsz, n, C_HEADS, HEAD_DIM), cos, sin)
    ck = _apply_rope(px[9].reshape(bsz, n, C_KV_HEADS, HEAD_DIM), cos, sin)
    cv = px[10].reshape(bsz, n, C_KV_HEADS, HEAD_DIM)
    ccq = pc[8].reshape(bsz, n_ctx, C_HEADS, HEAD_DIM)
    cck = pc[9].reshape(bsz, n_ctx, C_KV_HEADS, HEAD_DIM)
    ccv = pc[10].reshape(bsz, n_ctx, C_KV_HEADS, HEAD_DIM)
    o_c = _window_attention_latent(cq, ck, cv, cck, ccv, swa_sink)

    def merge(gate_cols, oa, obr, oc):
        ga, gb, gc = jnp.split(jax.nn.sigmoid(gate_cols), 3, axis=-1)
        y = ga * (oa @ w_br_a) + gb * (obr @ w_br_b) + gc * (oc @ w_br_c)
        return y @ w_out

    out_x = merge(px[11], o_a, o_b, o_c)
    if not with_ctx:
        return out_x, None
    oa_c = diff_out(_diff_attend(caq, cak, cav, lam))
    ob_c = gla_out(ofc + obc, pc[7])
    oc_c = _window_attention_ctx(ccq, cck, ccv, swa_sink)
    return out_x, merge(pc[11], oa_c, ob_c, oc_c)


def setup_inputs(seed: int = 0) -> dict:
    key = jax.random.key(seed)
    ks = jax.random.split(key, 24)
    f32 = jnp.float32
    D = D_MODEL
    L = DEPTH

    def nrm(k, shape, scale):
        return jax.random.normal(k, shape, f32) * scale

    return {
        'x': nrm(ks[0], (BATCH, SEQ, D), 1.0),
        'c': nrm(ks[1], (BATCH, D), 1.0),
        'ctx': nrm(ks[2], (BATCH, CTX_LEN, D), 1.0),
        'c_ctx': nrm(ks[3], (D,), 1.0),
        'w_ada': nrm(ks[4], (L, D, N_MOD * D), 0.3 * D ** -0.5),
        'b_ada': nrm(ks[5], (L, N_MOD * D), 0.02),
        'norm_g': 1.0 + nrm(ks[6], (L, 3, D), 0.02),
        'w_ffn1_in': nrm(ks[7], (L, D, 2 * FFN_DIM), D ** -0.5),
        'w_ffn1_out': nrm(ks[8], (L, FFN_DIM, D), FFN_DIM ** -0.5),
        'w_ffn2_in': nrm(ks[9], (L, D, 2 * FFN_DIM), D ** -0.5),
        'w_ffn2_out': nrm(ks[10], (L, FFN_DIM, D), FFN_DIM ** -0.5),
        'w_mix_in': nrm(ks[11], (L, D, IN_COLS), D ** -0.5),
        'diff_lambda': nrm(ks[12], (L, 4, HEAD_DIM), 0.1),
        'diff_subln': 1.0 + nrm(ks[13], (L, A_VDIM), 0.02),
        'gla_gate_w': nrm(ks[14], (L, 2, GATE_RANK, B_HEADS * B_KDIM), GATE_RANK ** -0.5),
        'gla_gate_b': nrm(ks[15], (L, 2, B_HEADS * B_KDIM), 0.1),
        'gla_norm': 1.0 + nrm(ks[16], (L, B_VDIM), 0.02),
        'swa_sink': nrm(ks[17], (L, C_HEADS), 0.5),
        'w_br_a': nrm(ks[18], (L, A_HEADS * A_VDIM, D), (A_HEADS * A_VDIM) ** -0.5),
        'w_br_b': nrm(ks[19], (L, B_HEADS * B_VDIM, D), (B_HEADS * B_VDIM) ** -0.5),
        'w_br_c': nrm(ks[20], (L, C_HEADS * HEAD_DIM, D), (C_HEADS * HEAD_DIM) ** -0.5),
        'w_mix_out': nrm(ks[21], (L, D, D), D ** -0.5),
        'final_g': 1.0 + nrm(ks[22], (D,), 0.02),
    }


def reference(x, c, ctx, c_ctx, w_ada, b_ada, norm_g, w_ffn1_in, w_ffn1_out, w_ffn2_in, w_ffn2_out,
              w_mix_in, diff_lambda, diff_subln, gla_gate_w, gla_gate_b, gla_norm, swa_sink,
              w_br_a, w_br_b, w_br_c, w_mix_out, final_g):
    bsz, n, d = x.shape
    cos, sin = _rope_tables(n)
    sc = jax.nn.silu(c)
    scc = jax.nn.silu(c_ctx)
    for l in range(DEPTH):
        with_ctx = l < DEPTH - 1
        lam_init = 0.8 - 0.6 * math.exp(-0.3 * l)
        mx = (sc @ w_ada[l] + b_ada[l]).reshape(bsz, N_MOD, 1, d)
        mc = (scc @ w_ada[l] + b_ada[l]).reshape(N_MOD, d)
        mx = [mx[:, i] for i in range(N_MOD)]
        mc = [mc[i] for i in range(N_MOD)]
        x = x + 0.5 * mx[2] * _swiglu(_modulate(_rmsnorm(x, norm_g[l, 0]), mx[0], mx[1]), w_ffn1_in[l], w_ffn1_out[l])
        ctx = ctx + 0.5 * mc[2] * _swiglu(_modulate(_rmsnorm(ctx, norm_g[l, 0]), mc[0], mc[1]), w_ffn1_in[l], w_ffn1_out[l])
        hx = _modulate(_rmsnorm(x, norm_g[l, 1]), mx[3], mx[4])
        hc = _modulate(_rmsnorm(ctx, norm_g[l, 1]), mc[3], mc[4])
        ox, oc = _token_mix(hx, hc, w_mix_in[l], diff_lambda[l], diff_subln[l], gla_gate_w[l], gla_gate_b[l],
                            gla_norm[l], swa_sink[l], w_br_a[l], w_br_b[l], w_br_c[l], w_mix_out[l],
                            lam_init, cos, sin, with_ctx)
        x = x + mx[5] * ox
        x = x + 0.5 * mx[8] * _swiglu(_modulate(_rmsnorm(x, norm_g[l, 2]), mx[6], mx[7]), w_ffn2_in[l], w_ffn2_out[l])
        if with_ctx:
            ctx = ctx + mc[5] * oc
            ctx = ctx + 0.5 * mc[8] * _swiglu(_modulate(_rmsnorm(ctx, norm_g[l, 2]), mc[6], mc[7]), w_ffn2_in[l], w_ffn2_out[l])
    return _rmsnorm(x, final_g)
```

```python
import functools
import math

import jax
import jax.numpy as jnp
from jax import lax
from jax.experimental import pallas as pl
from jax.experimental.pallas import tpu as pltpu

F32 = jnp.float32
BF16 = jnp.bfloat16

HEAD = 64
ROPE_BASE = 10000.0
GRID_W = 64
N_MOD = 9
A_HEADS = 4
B_HEADS = 4
B_KDIM = 64
B_VDIM = 128
GATE_RANK = 16
GATE_TAU = 16.0
GLA_CHUNK = 64
C_HEADS = 8
C_KV_HEADS = 2
WINDOW = 128
NEG_INF = -1e30
EPS = 1e-6

TM = 256
LANES = 128
VMEM_LIMIT = 56 * 1024 * 1024

_AQ, _AK, _CQ, _CK, _AV, _BQ, _BK, _BV, _BG, _CV, _MIX_COLS = (
    0, 512, 1024, 1536, 1792, 2304, 2560, 2816, 3328, 3456, 3712)


def _cparams(n_grid, vmem=VMEM_LIMIT):
    return pltpu.CompilerParams(dimension_semantics=("arbitrary",) * n_grid, vmem_limit_bytes=vmem)


def _const_spec(shape):
    nd = len(shape)
    return pl.BlockSpec(shape, lambda *_: (0,) * nd, pipeline_mode=pl.Buffered(1))


def _rms(x):
    return x * lax.rsqrt(jnp.mean(x * x, axis=-1, keepdims=True) + EPS)


def _norm_mod(x, g, shift, scale):
    return _rms(x) * g * (1.0 + scale) + shift


def _dot(a, b):
    return jnp.dot(a, b, preferred_element_type=F32)


def _ada_body(c_ref, w_ref, b_ref, o_ref):
    sc = jax.nn.silu(c_ref[...]).astype(BF16)
    o_ref[...] = _dot(sc, w_ref[...].astype(BF16)) + b_ref[...]


def _ada_call(cvec, w_ada, b_ada):
    depth, d, n_out = w_ada.shape
    rows = cvec.shape[0]
    tn = n_out // 4
    return pl.pallas_call(
        _ada_body,
        grid=(depth, n_out // tn),
        in_specs=[pl.BlockSpec((rows, d), lambda l, n: (0, 0)),
                  pl.BlockSpec((None, d, tn), lambda l, n: (l, 0, n)),
                  pl.BlockSpec((None, 1, tn), lambda l, n: (l, 0, n))],
        out_specs=pl.BlockSpec((None, rows, tn), lambda l, n: (l, 0, n)),
        out_shape=jax.ShapeDtypeStruct((depth, rows, n_out), F32),
        compiler_params=_cparams(2),
        name="adaln",
    )(cvec, w_ada, b_ada.reshape(depth, 1, n_out))


def _tok_spec(width):
    return pl.BlockSpec((None, TM, width), lambda b, j: (b, j, 0))


def _mod_spec(d, ctx_row):
    return pl.BlockSpec((None, N_MOD, d), lambda b, j: (jnp.where(j == 0, ctx_row, b), 0, 0))


def _ffn_body(x_ref, mod_ref, g_ref, wu_ref, wv_ref, wd_ref, o_ref, *, mod0, f_chunk):
    x = x_ref[...]
    mod = mod_ref[...]
    hb = _norm_mod(x, g_ref[...], mod[mod0:mod0 + 1], mod[mod0 + 1:mod0 + 2]).astype(BF16)
    f = wu_ref.shape[1]
    acc = jnp.zeros(x.shape, F32)
    for lo in range(0, f, f_chunk):
        u = _dot(hb, wu_ref[:, lo:lo + f_chunk])
        v = _dot(hb, wv_ref[:, lo:lo + f_chunk])
        a = (jax.nn.silu(u) * v).astype(BF16)
        acc = acc + _dot(a, wd_ref[lo:lo + f_chunk, :])
    o_ref[...] = x + 0.5 * mod[mod0 + 2:mod0 + 3] * acc


def _ffn_call(t, mod, g, wu, wv, wd, mod0, ctx_row):
    b, nt, d = t.shape
    f = wu.shape[1]
    return pl.pallas_call(
        functools.partial(_ffn_body, mod0=mod0, f_chunk=f // 2),
        grid=(b, nt // TM),
        in_specs=[_tok_spec(d), _mod_spec(d, ctx_row), _const_spec((1, d)),
                  _const_spec((d, f)), _const_spec((d, f)), _const_spec((f, d))],
        out_specs=_tok_spec(d),
        out_shape=jax.ShapeDtypeStruct(t.shape, F32),
        compiler_params=_cparams(2),
        name="ffn",
    )(t, mod, g, wu, wv, wd)


def _rope128(x, cos, sin_a, sin_b):
    return x * cos + pltpu.roll(x, LANES - 16, 1) * sin_a + pltpu.roll(x, 16, 1) * sin_b


def _mixin_body(x_ref, mod_ref, g_ref, w_ref, cos_ref, sa_ref, sb_ref,
                aq_ref, akt_ref, cq_ref, ckt_ref, av_ref, bq_ref, bk_ref, bv_ref, bg_ref, cv_ref):
    mod = mod_ref[...]
    hb = _norm_mod(x_ref[...], g_ref[...], mod[3:4], mod[4:5]).astype(BF16)
    cos, sin_a, sin_b = cos_ref[...], sa_ref[...], sb_ref[...]
    qk_scale = HEAD ** -0.5

    def roped(col):
        return _rope128(_dot(hb, w_ref[:, col:col + LANES]), cos, sin_a, sin_b)

    for i in range(4):
        lo = i * LANES
        aq_ref[:, lo:lo + LANES] = (roped(_AQ + lo) * qk_scale).astype(BF16)
        cq_ref[:, lo:lo + LANES] = (roped(_CQ + lo) * qk_scale).astype(BF16)
        akt_ref[lo:lo + LANES, :] = roped(_AK + lo).T.astype(BF16)
    for i in range(2):
        lo = i * LANES
        ckt_ref[lo:lo + LANES, :] = roped(_CK + lo).T.astype(BF16)
    av_ref[...] = _dot(hb, w_ref[:, _AV:_BQ]).astype(BF16)
    bq_ref[...] = _dot(hb, w_ref[:, _BQ:_BK]).astype(BF16)
    bk_ref[...] = _dot(hb, w_ref[:, _BK:_BV]).astype(BF16)
    bv_ref[...] = _dot(hb, w_ref[:, _BV:_BG]).astype(BF16)
    bg_ref[...] = _dot(hb, w_ref[:, _BG:_CV]).astype(BF16)
    cv_ref[...] = _dot(hb, w_ref[:, _CV:_MIX_COLS]).astype(BF16)


def _mixin_call(t, mod, g, w, cos, sin_a, sin_b, ctx_row):
    b, nt, d = t.shape
    tbl = pl.BlockSpec((TM, LANES), lambda bb, j: (j, 0))

    def tposed(rows):
        return pl.BlockSpec((None, rows, TM), lambda bb, j: (bb, 0, j))

    def sds(*shape):
        return jax.ShapeDtypeStruct(shape, BF16)

    return pl.pallas_call(
        _mixin_body,
        grid=(b, nt // TM),
        in_specs=[_tok_spec(d), _mod_spec(d, ctx_row), _const_spec((1, d)), _const_spec((d, _MIX_COLS)),
                  tbl, tbl, tbl],
        out_specs=[_tok_spec(512), tposed(512), _tok_spec(512), tposed(256), _tok_spec(512),
                   _tok_spec(256), _tok_spec(256), _tok_spec(512), _tok_spec(128), _tok_spec(256)],
        out_shape=[sds(b, nt, 512), sds(b, 512, nt), sds(b, nt, 512), sds(b, 256, nt), sds(b, nt, 512),
                   sds(b, nt, 256), sds(b, nt, 256), sds(b, nt, 512), sds(b, nt, 128), sds(b, nt, 256)],
        compiler_params=_cparams(2),
        name="mixin",
    )(t, mod, g, w, cos, sin_a, sin_b)


def _attn_a_body(lam_ref, q_ref, kt_ref, v_ref, sub_ref, o_ref, *, nctx, k_chunk, lam_init):
    j = pl.program_id(2)
    nt = kt_ref.shape[1]
    q = q_ref[...]
    lane = lax.broadcasted_iota(jnp.int32, (1, LANES), 1)
    zero = jnp.zeros_like(q)
    qm = (jnp.where(lane < HEAD, q, zero), jnp.where(lane >= HEAD, q, zero))

    def first(kt, v):
        out = []
        for m in range(2):
            s = _dot(qm[m], kt)
            mx = jnp.max(s, axis=-1, keepdims=True)
            e = jnp.exp(s - mx)
            out += [mx, jnp.sum(e, axis=-1, keepdims=True), _dot(e.astype(BF16), v)]
        return tuple(out)

    def step(c, carry):
        start = pl.multiple_of(nctx + c * k_chunk, TM)
        kt = kt_ref[:, pl.ds(start, k_chunk)]
        v = v_ref[pl.ds(start, k_chunk), :]
        out = []
        for m in range(2):
            mx, l, acc = carry[3 * m:3 * m + 3]
            s = _dot(qm[m], kt)
            mx_new = jnp.maximum(mx, jnp.max(s, axis=-1, keepdims=True))
            alpha = jnp.exp(mx - mx_new)
            e = jnp.exp(s - mx_new)
            out += [mx_new, alpha * l + jnp.sum(e, axis=-1, keepdims=True),
                    alpha * acc + _dot(e.astype(BF16), v)]
        return tuple(out)

    carry = first(kt_ref[:, 0:nctx], v_ref[0:nctx, :])
    n_steps = jnp.where(j > 0, (nt - nctx) // k_chunk, 0)
    _, l0, acc0, _, l1, acc1 = lax.fori_loop(0, n_steps, step, carry)

    lp = lam_ref[...]
    lam = (jnp.exp(jnp.sum(lp[0:1] * lp[1:2], axis=-1, keepdims=True))
           - jnp.exp(jnp.sum(lp[2:3] * lp[3:4], axis=-1, keepdims=True)) + lam_init)
    o = acc0 / l0 - lam * (acc1 / l1)
    o_ref[...] = (_rms(o) * sub_ref[...] * (1.0 - lam_init)).astype(BF16)


def _attn_a_call(aq, akt, av, diff_lambda, subln, nctx, lam_init):
    b, nt, _ = aq.shape
    k_chunk = 512
    assert (nt - nctx) % k_chunk == 0
    return pl.pallas_call(
        functools.partial(_attn_a_body, nctx=nctx, k_chunk=k_chunk, lam_init=lam_init),
        grid=(b, A_HEADS, nt // TM),
        in_specs=[pl.BlockSpec((4, HEAD), lambda bb, h, j: (0, 0)),
                  pl.BlockSpec((None, TM, LANES), lambda bb, h, j: (bb, j, h)),
                  pl.BlockSpec((None, LANES, nt), lambda bb, h, j: (bb, h, 0)),
                  pl.BlockSpec((None, nt, LANES), lambda bb, h, j: (bb, 0, h)),
                  pl.BlockSpec((1, LANES), lambda bb, h, j: (0, 0))],
        out_specs=pl.BlockSpec((None, TM, LANES), lambda bb, h, j: (bb, j, h)),
        out_shape=jax.ShapeDtypeStruct(aq.shape, BF16),
        compiler_params=_cparams(3),
        name="attn_a",
    )(diff_lambda, aq, akt, av, subln)


def _gla_body(q_ref, k_ref, v_ref, g_ref, gw_ref, gb_ref, gn_ref, o_ref, la_ref, st_ref, *, nctx):
    nt = q_ref.shape[0]
    ch = GLA_CHUNK
    n_chunks = nt // ch
    n_ctx_chunks = nctx // ch

    def la_tile(t, carry):
        r0 = pl.multiple_of(t * TM, TM)
        gt = g_ref[pl.ds(r0, TM), :]
        for d in range(2):
            z = _dot(gt, gw_ref[d]) + gb_ref[d]
            log_sig = jnp.minimum(z, 0.0) - jnp.log(1.0 + jnp.exp(-jnp.abs(z)))
            la_ref[d, pl.ds(r0, TM), :] = log_sig * (1.0 / GATE_TAU)
        return carry

    lax.fori_loop(0, nt // TM, la_tile, 0)

    row = lax.broadcasted_iota(jnp.int32, (ch, ch), 0)
    col = lax.broadcasted_iota(jnp.int32, (ch, ch), 1)
    srow = lax.broadcasted_iota(jnp.int32, (2 * B_KDIM, 2 * B_VDIM), 0)
    scol = lax.broadcasted_iota(jnp.int32, (2 * B_KDIM, 2 * B_VDIM), 1)
    same_head = (srow < B_KDIM) == (scol < B_VDIM)
    lane = lax.broadcasted_iota(jnp.int32, (1, LANES), 1)

    def chunk(d, c):
        r0 = pl.multiple_of(c * ch, ch)
        rows = pl.ds(r0, ch)
        la = la_ref[d, rows, :]
        earlier = (col <= row) if d == 0 else (col >= row)
        tri = jnp.where(earlier, 1.0, 0.0).astype(BF16)
        hi = la.astype(BF16)
        r1 = la - hi.astype(F32)
        mid = r1.astype(BF16)
        low = (r1 - mid.astype(F32)).astype(BF16)
        cum = _dot(tri, hi) + _dot(tri, mid) + _dot(tri, low)
        cum_t = cum.T
        last_t = cum_t[:, ch - 1:ch] if d == 0 else cum_t[:, 0:1]
        qf = q_ref[rows, :].astype(F32)
        kf_t = k_ref[rows, :].astype(F32).T
        v = v_ref[rows, :]
        qe = qf * (B_KDIM ** -0.5) * jnp.exp(cum)
        ke_t = (kf_t * jnp.exp(-cum_t)).astype(BF16)
        kd_t = (kf_t * jnp.exp(last_t - cum_t)).astype(BF16)
        state = st_ref[...]
        o_inter = _dot(qe.astype(BF16), state.astype(BF16))
        outs = []
        for h in range(2):
            q_h = jnp.where((lane < B_KDIM) if h == 0 else (lane >= B_KDIM), qe, 0.0).astype(BF16)
            att = jnp.where(earlier, _dot(q_h, ke_t), 0.0).astype(BF16)
            outs.append(_dot(att, v[:, h * B_VDIM:(h + 1) * B_VDIM]))
        st_ref[...] = jnp.exp(last_t) * state + jnp.where(same_head, _dot(kd_t, v), 0.0)
        return jnp.concatenate(outs, axis=1) + o_inter, rows

    st_ref[...] = jnp.zeros(st_ref.shape, F32)

    def fwd(c, carry):
        o, rows = chunk(0, c)
        o_ref[rows, :] = o
        return carry

    lax.fori_loop(0, n_chunks, fwd, 0)

    st_ref[...] = jnp.zeros(st_ref.shape, F32)
    gn = gn_ref[...]

    def rev(t, carry):
        c = jnp.where(t < n_ctx_chunks, n_ctx_chunks - 1 - t, n_chunks + n_ctx_chunks - 1 - t)
        o, rows = chunk(1, c)
        tot = o_ref[rows, :] + o
        for h in range(2):
            lo = h * B_VDIM
            o_ref[rows, lo:lo + B_VDIM] = _rms(tot[:, lo:lo + B_VDIM]) * gn
        return carry

    lax.fori_loop(0, n_chunks, rev, 0)


def _gla_call(bq, bk, bv, bg, gw, gb, gn, nctx):
    b, nt, _ = bq.shape
    return pl.pallas_call(
        functools.partial(_gla_body, nctx=nctx),
        grid=(b, B_HEADS // 2),
        in_specs=[pl.BlockSpec((None, nt, LANES), lambda bb, p: (bb, 0, p)),
                  pl.BlockSpec((None, nt, LANES), lambda bb, p: (bb, 0, p)),
                  pl.BlockSpec((None, nt, 2 * B_VDIM), lambda bb, p: (bb, 0, p)),
                  pl.BlockSpec((None, nt, LANES), lambda bb, p: (bb, 0, 0)),
                  pl.BlockSpec((2, LANES, LANES), lambda bb, p: (0, 0, p)),
                  pl.BlockSpec((2, 1, LANES), lambda bb, p: (0, 0, p)),
                  pl.BlockSpec((1, B_VDIM), lambda bb, p: (0, 0))],
        out_specs=pl.BlockSpec((None, nt, 2 * B_VDIM), lambda bb, p: (bb, 0, p)),
        out_shape=jax.ShapeDtypeStruct((b, nt, B_HEADS * B_VDIM), F32),
        scratch_shapes=[pltpu.VMEM((2, nt, LANES), F32), pltpu.VMEM((2 * B_KDIM, 2 * B_VDIM), F32)],
        compiler_params=_cparams(2),
        name="gla",
    )(bq, bk, bv, bg, gw, gb, gn)


def _attn_c_body(sink_ref, q_ref, kt_ref, v_ref, o_ref, *, nctx, span):
    j = pl.program_id(1)
    nlat = kt_ref.shape[1] - nctx
    lane = lax.broadcasted_iota(jnp.int32, (1, LANES), 1)
    lower = lane < HEAD
    heads_per_kv = C_HEADS // C_KV_HEADS

    def head_q(h):
        q2 = q_ref[:, (h // 2) * LANES:(h // 2 + 1) * LANES]
        return jnp.where(lower if h % 2 == 0 else jnp.logical_not(lower), q2, jnp.zeros_like(q2))

    def store_pair(h, o_even, o_odd):
        lo = (h // 2) * LANES
        o_ref[:, lo:lo + LANES] = jnp.where(lower, o_even, o_odd).astype(BF16)

    @pl.when(j == 0)
    def _():
        prev = None
        for h in range(C_HEADS):
            kv = h // heads_per_kv
            s = _dot(head_q(h), kt_ref[kv * LANES:(kv + 1) * LANES, 0:nctx])
            sink = sink_ref[h]
            mx = jnp.maximum(jnp.max(s, axis=-1, keepdims=True), sink)
            e = jnp.exp(s - mx)
            l = jnp.sum(e, axis=-1, keepdims=True) + jnp.exp(sink - mx)
            o = _dot(e.astype(BF16), v_ref[0:nctx, kv * LANES:(kv + 1) * LANES]) / l
            if h % 2 == 1:
                store_pair(h, prev, o)
            prev = o

    @pl.when(j > 0)
    def _():
        q0 = (j - 1) * TM
        k0 = jnp.clip(q0 - WINDOW, 0, nlat - span)
        start = pl.multiple_of(nctx + k0, LANES)
        qpos = q0 + lax.broadcasted_iota(jnp.int32, (TM, span), 0)
        kpos = k0 + lax.broadcasted_iota(jnp.int32, (TM, span), 1)
        valid = jnp.abs(qpos - kpos) <= WINDOW
        prev = None
        for h in range(C_HEADS):
            kv = h // heads_per_kv
            kvl = slice(kv * LANES, (kv + 1) * LANES)
            qh = head_q(h)
            s_loc = jnp.where(valid, _dot(qh, kt_ref[kvl, pl.ds(start, span)]), NEG_INF)
            s_ctx = _dot(qh, kt_ref[kvl, 0:nctx])
            sink = sink_ref[h]
            mx = jnp.maximum(jnp.maximum(jnp.max(s_loc, axis=-1, keepdims=True),
                                         jnp.max(s_ctx, axis=-1, keepdims=True)), sink)
            e_loc = jnp.exp(s_loc - mx)
            e_ctx = jnp.exp(s_ctx - mx)
            l = (jnp.sum(e_loc, axis=-1, keepdims=True) + jnp.sum(e_ctx, axis=-1, keepdims=True)
                 + jnp.exp(sink - mx))
            o = (_dot(e_loc.astype(BF16), v_ref[pl.ds(start, span), kvl])
                 + _dot(e_ctx.astype(BF16), v_ref[0:nctx, kvl])) / l
            if h % 2 == 1:
                store_pair(h, prev, o)
            prev = o


def _attn_c_call(cq, ckt, cv, sink, nctx):
    b, nt, _ = cq.shape
    span = TM + 2 * WINDOW
    assert nt - nctx >= span
    return pl.pallas_call(
        functools.partial(_attn_c_body, nctx=nctx, span=span),
        grid=(b, nt // TM),
        in_specs=[pl.BlockSpec(memory_space=pltpu.SMEM),
                  _tok_spec(C_HEADS * HEAD),
                  pl.BlockSpec((None, C_KV_HEADS * LANES, nt), lambda bb, j: (bb, 0, 0)),
                  pl.BlockSpec((None, nt, C_KV_HEADS * LANES), lambda bb, j: (bb, 0, 0))],
        out_specs=_tok_spec(C_HEADS * HEAD),
        out_shape=jax.ShapeDtypeStruct(cq.shape, BF16),
        compiler_params=_cparams(2),
        name="attn_c",
    )(sink, cq, ckt, cv)


def _merge_body(x_ref, mod_ref, g_ref, wg_ref, wa_ref, wb_ref, wc_ref, wo_ref,
                oa_ref, ob_ref, oc_ref, o_ref):
    x = x_ref[...]
    d = x.shape[1]
    mod = mod_ref[...]
    hb = _norm_mod(x, g_ref[...], mod[3:4], mod[4:5]).astype(BF16)
    r = _dot(hb, wg_ref[:, 3 * d:])
    ob = (ob_ref[...] * jax.nn.silu(r)).astype(BF16)
    y = jax.nn.sigmoid(_dot(hb, wg_ref[:, 0:d])) * _dot(oa_ref[...], wa_ref[...])
    y = y + jax.nn.sigmoid(_dot(hb, wg_ref[:, d:2 * d])) * _dot(ob, wb_ref[...])
    y = y + jax.nn.sigmoid(_dot(hb, wg_ref[:, 2 * d:3 * d])) * _dot(oc_ref[...], wc_ref[...])
    o_ref[...] = x + mod[5:6] * _dot(y.astype(BF16), wo_ref[...])


def _merge_call(t, mod, g, wg, wa, wb, wc, wo, oa, ob, oc, ctx_row):
    b, nt, d = t.shape
    return pl.pallas_call(
        _merge_body,
        grid=(b, nt // TM),
        in_specs=[_tok_spec(d), _mod_spec(d, ctx_row), _const_spec((1, d)), _const_spec(wg.shape),
                  _const_spec(wa.shape), _const_spec(wb.shape), _const_spec(wc.shape), _const_spec(wo.shape),
                  _tok_spec(oa.shape[2]), _tok_spec(ob.shape[2]), _tok_spec(oc.shape[2])],
        out_specs=_tok_spec(d),
        out_shape=jax.ShapeDtypeStruct(t.shape, F32),
        compiler_params=_cparams(2),
        name="merge",
    )(t, mod, g, wg, wa, wb, wc, wo, oa, ob, oc)


def _final_body(x_ref, g_ref, o_ref):
    o_ref[...] = _rms(x_ref[...]) * g_ref[...]


def _final_call(t, g, nctx):
    b, nt, d = t.shape
    skip = nctx // TM
    return pl.pallas_call(
        _final_body,
        grid=(b, nt // TM - skip),
        in_specs=[pl.BlockSpec((None, TM, d), lambda bb, j: (bb, j + skip, 0)), _const_spec((1, d))],
        out_specs=_tok_spec(d),
        out_shape=jax.ShapeDtypeStruct((b, nt - nctx, d), F32),
        compiler_params=_cparams(2),
        name="final_norm",
    )(t, g)


def _rope_tables(n_lat, nctx):
    rows = n_lat // GRID_W
    row = jnp.repeat(jnp.arange(rows, dtype=F32), GRID_W)
    col = jnp.tile(jnp.arange(GRID_W, dtype=F32), rows)
    n_freq = HEAD // 4
    freqs = jnp.power(ROPE_BASE, -jnp.arange(n_freq, dtype=F32) / n_freq)
    ar = row[:, None] * freqs
    ac = col[:, None] * freqs
    ang = jnp.concatenate([ar, ar, ac, ac], axis=-1)
    ang = jnp.concatenate([ang, ang], axis=-1)
    first = (jnp.arange(LANES) % 32) < 16
    cos, sin = jnp.cos(ang), jnp.sin(ang)
    sin_a = jnp.where(first, -sin, 0.0)
    sin_b = jnp.where(first, 0.0, sin)
    pad = lambda a, v: jnp.concatenate([jnp.full((nctx, LANES), v, F32), a], axis=0)
    return pad(cos, 1.0), pad(sin_a, 0.0), pad(sin_b, 0.0)


def _mixin_weight(w):
    aq, ak, av = w[:, 0:512], w[:, 512:1024], w[:, 1024:1536]
    bq, bk, bv = w[:, 1536:1792], w[:, 1792:2048], w[:, 2048:2560]
    bg = jnp.pad(w[:, 2560:2560 + 2 * GATE_RANK], ((0, 0), (0, LANES - 2 * GATE_RANK)))
    cq = w[:, 3104:3616]
    ck = [w[:, 3616 + i * HEAD:3616 + (i + 1) * HEAD] for i in range(C_KV_HEADS)]
    cv = [w[:, 3744 + i * HEAD:3744 + (i + 1) * HEAD] for i in range(C_KV_HEADS)]
    cols = [aq, ak, cq, ck[0], ck[0], ck[1], ck[1], av, bq, bk, bv, bg, cv[0], cv[0], cv[1], cv[1]]
    return jnp.concatenate(cols, axis=1).astype(BF16)


def kernel(x, c, ctx, c_ctx, w_ada, b_ada, norm_g, w_ffn1_in, w_ffn1_out, w_ffn2_in, w_ffn2_out,
           w_mix_in, diff_lambda, diff_subln, gla_gate_w, gla_gate_b, gla_norm, swa_sink,
           w_br_a, w_br_b, w_br_c, w_mix_out, final_g):
    bsz, n, d = x.shape
    nctx = ctx.shape[1]
    depth = w_ada.shape[0]
    ffn = w_ffn1_out.shape[1]
    assert nctx == TM and n % 512 == 0 and d % LANES == 0

    t = jnp.concatenate([ctx, x], axis=1)
    ctx_row = bsz
    n_rows = -(-(bsz + 1) // 8) * 8
    cvec = jnp.concatenate([c, c_ctx[None, :], jnp.zeros((n_rows - bsz - 1, d), F32)], axis=0)
    mod_all = _ada_call(cvec, w_ada, b_ada).reshape(depth, n_rows, N_MOD, d)
    cos, sin_a, sin_b = _rope_tables(n, nctx)
    gate_cols = 2560 + 2 * GATE_RANK
    merge_cols = 3872

    for l in range(depth):
        lam_init = 0.8 - 0.6 * math.exp(-0.3 * l)
        mod = mod_all[l]
        ng = norm_g[l]
        t = _ffn_call(t, mod, ng[0:1], w_ffn1_in[l, :, :ffn].astype(BF16), w_ffn1_in[l, :, ffn:].astype(BF16),
                      w_ffn1_out[l].astype(BF16), 0, ctx_row)
        aq, akt, cq, ckt, av, bq, bk, bv, bg, cv = _mixin_call(
            t, mod, ng[1:2], _mixin_weight(w_mix_in[l]), cos, sin_a, sin_b, ctx_row)
        oa = _attn_a_call(aq, akt, av, diff_lambda[l], diff_subln[l][None, :], nctx, lam_init)
        gw = jnp.zeros((2, LANES, B_HEADS * B_KDIM), F32)
        for dd in range(2):
            gw = gw.at[dd, dd * GATE_RANK:(dd + 1) * GATE_RANK].set(gla_gate_w[l, dd])
        ob = _gla_call(bq, bk, bv, bg, gw.astype(BF16), gla_gate_b[l][:, None, :], gla_norm[l][None, :], nctx)
        oc = _attn_c_call(cq, ckt, cv, swa_sink[l], nctx)
        wg = jnp.concatenate([w_mix_in[l, :, merge_cols:], w_mix_in[l, :, gate_cols:gate_cols + 512]],
                             axis=1).astype(BF16)
        t = _merge_call(t, mod, ng[1:2], wg, w_br_a[l].astype(BF16), w_br_b[l].astype(BF16),
                        w_br_c[l].astype(BF16), w_mix_out[l].astype(BF16), oa, ob, oc, ctx_row)
        t = _ffn_call(t, mod, ng[2:3], w_ffn2_in[l, :, :ffn].astype(BF16), w_ffn2_in[l, :, ffn:].astype(BF16),
                      w_ffn2_out[l].astype(BF16), 6, ctx_row)
    return _final_call(t, final_g[None, :], nctx)
```

```python
import functools
import math

import jax
import jax.numpy as jnp
from jax import lax
from jax.experimental import pallas as pl
from jax.experimental.pallas import tpu as pltpu

F32 = jnp.float32
BF16 = jnp.bfloat16

HEAD = 64
ROPE_BASE = 10000.0
GRID_W = 64
N_MOD = 9
A_HEADS = 4
B_HEADS = 4
B_KDIM = 64
B_VDIM = 128
GATE_RANK = 16
GATE_TAU = 16.0
GLA_CHUNK = 64
C_HEADS = 8
C_KV_HEADS = 2
WINDOW = 128
NEG_INF = -1e30
EPS = 1e-6

TM = 256
LANES = 128
VMEM_LIMIT = 56 * 1024 * 1024

_AQ, _AK, _CQ, _CK, _AV, _BQ, _BK, _BV, _BG, _CV, _MIX_COLS = (
    0, 512, 1024, 1536, 1792, 2304, 2560, 2816, 3328, 3456, 3712)


def _cparams(n_grid, vmem=VMEM_LIMIT):
    return pltpu.CompilerParams(dimension_semantics=("arbitrary",) * n_grid, vmem_limit_bytes=vmem)


def _const_spec(shape):
    nd = len(shape)
    return pl.BlockSpec(shape, lambda *_: (0,) * nd, pipeline_mode=pl.Buffered(1))


def _rms(x):
    return x * lax.rsqrt(jnp.mean(x * x, axis=-1, keepdims=True) + EPS)


def _norm_mod(x, g, shift, scale):
    return _rms(x) * g * (1.0 + scale) + shift


def _dot(a, b):
    return jnp.dot(a, b, preferred_element_type=F32)


def _ada_body(c_ref, w_ref, b_ref, o_ref):
    sc = jax.nn.silu(c_ref[...]).astype(BF16)
    o_ref[...] = _dot(sc, w_ref[...].astype(BF16)) + b_ref[...]


def _ada_call(cvec, w_ada, b_ada):
    depth, d, n_out = w_ada.shape
    rows = cvec.shape[0]
    tn = n_out // 4
    return pl.pallas_call(
        _ada_body,
        grid=(depth, n_out // tn),
        in_specs=[pl.BlockSpec((rows, d), lambda l, n: (0, 0)),
                  pl.BlockSpec((None, d, tn), lambda l, n: (l, 0, n)),
                  pl.BlockSpec((None, 1, tn), lambda l, n: (l, 0, n))],
        out_specs=pl.BlockSpec((None, rows, tn), lambda l, n: (l, 0, n)),
        out_shape=jax.ShapeDtypeStruct((depth, rows, n_out), F32),
        compiler_params=_cparams(2),
        name="adaln",
    )(cvec, w_ada, b_ada.reshape(depth, 1, n_out))


def _tok_spec(width):
    return pl.BlockSpec((None, TM, width), lambda b, j: (b, j, 0))


def _mod_spec(d, ctx_row):
    return pl.BlockSpec((None, N_MOD, d), lambda b, j: (jnp.where(j == 0, ctx_row, b), 0, 0))


def _ffn_body(x_ref, mod_ref, g_ref, wu_ref, wv_ref, wd_ref, o_ref, *, mod0, f_chunk):
    x = x_ref[...]
    mod = mod_ref[...]
    hb = _norm_mod(x, g_ref[...], mod[mod0:mod0 + 1], mod[mod0 + 1:mod0 + 2]).astype(BF16)
    f = wu_ref.shape[1]
    acc = jnp.zeros(x.shape, F32)
    for lo in range(0, f, f_chunk):
        u = _dot(hb, wu_ref[:, lo:lo + f_chunk])
        v = _dot(hb, wv_ref[:, lo:lo + f_chunk])
        a = (jax.nn.silu(u) * v).astype(BF16)
        acc = acc + _dot(a, wd_ref[lo:lo + f_chunk, :])
    o_ref[...] = x + 0.5 * mod[mod0 + 2:mod0 + 3] * acc


def _ffn_call(t, mod, g, wu, wv, wd, mod0, ctx_row):
    b, nt, d = t.shape
    f = wu.shape[1]
    return pl.pallas_call(
        functools.partial(_ffn_body, mod0=mod0, f_chunk=f // 2),
        grid=(b, nt // TM),
        in_specs=[_tok_spec(d), _mod_spec(d, ctx_row), _const_spec((1, d)),
                  _const_spec((d, f)), _const_spec((d, f)), _const_spec((f, d))],
        out_specs=_tok_spec(d),
        out_shape=jax.ShapeDtypeStruct(t.shape, F32),
        compiler_params=_cparams(2),
        name="ffn",
    )(t, mod, g, wu, wv, wd)


def _rope128(x, cos, sin_a, sin_b):
    return x * cos + pltpu.roll(x, LANES - 16, 1) * sin_a + pltpu.roll(x, 16, 1) * sin_b


def _mixin_body(x_ref, mod_ref, g_ref, w_ref, cos_ref, sa_ref, sb_ref,
                aqt_ref, ak_ref, cq_ref, ckt_ref, avt_ref, bq_ref, bk_ref, bv_ref, bg_ref, cv_ref, bvt_ref):
    mod = mod_ref[...]
    hb = _norm_mod(x_ref[...], g_ref[...], mod[3:4], mod[4:5]).astype(BF16)
    cos, sin_a, sin_b = cos_ref[...], sa_ref[...], sb_ref[...]
    qk_scale = HEAD ** -0.5

    def roped(col):
        return _rope128(_dot(hb, w_ref[:, col:col + LANES]), cos, sin_a, sin_b)

    for i in range(4):
        lo = i * LANES
        aqt_ref[lo:lo + LANES, :] = (roped(_AQ + lo) * (qk_scale * math.log2(math.e))).T.astype(BF16)
        cq_ref[:, lo:lo + LANES] = (roped(_CQ + lo) * qk_scale).astype(BF16)
        ak_ref[:, lo:lo + LANES] = roped(_AK + lo).astype(BF16)
        avt_ref[lo:lo + LANES, :] = _dot(hb, w_ref[:, _AV + lo:_AV + lo + LANES]).T.astype(BF16)
    for i in range(2):
        lo = i * LANES
        ckt_ref[lo:lo + LANES, :] = roped(_CK + lo).T.astype(BF16)
    bq_ref[...] = _dot(hb, w_ref[:, _BQ:_BK]).astype(BF16)
    bk_ref[...] = _dot(hb, w_ref[:, _BK:_BV]).astype(BF16)
    bv = _dot(hb, w_ref[:, _BV:_BG])
    bv_ref[...] = bv.astype(BF16)
    bvt_ref[...] = bv.T.astype(BF16)
    bg_ref[...] = _dot(hb, w_ref[:, _BG:_CV]).astype(BF16)
    cv_ref[...] = _dot(hb, w_ref[:, _CV:_MIX_COLS]).astype(BF16)


def _mixin_call(t, mod, g, w, cos, sin_a, sin_b, ctx_row):
    b, nt, d = t.shape
    tbl = pl.BlockSpec((TM, LANES), lambda bb, j: (j, 0))

    def tposed(rows):
        return pl.BlockSpec((None, rows, TM), lambda bb, j: (bb, 0, j))

    def sds(*shape):
        return jax.ShapeDtypeStruct(shape, BF16)

    return pl.pallas_call(
        _mixin_body,
        grid=(b, nt // TM),
        in_specs=[_tok_spec(d), _mod_spec(d, ctx_row), _const_spec((1, d)), _const_spec((d, _MIX_COLS)),
                  tbl, tbl, tbl],
        out_specs=[tposed(512), _tok_spec(512), _tok_spec(512), tposed(256), tposed(512),
                   _tok_spec(256), _tok_spec(256), _tok_spec(512), _tok_spec(128), _tok_spec(256), tposed(512)],
        out_shape=[sds(b, 512, nt), sds(b, nt, 512), sds(b, nt, 512), sds(b, 256, nt), sds(b, 512, nt),
                   sds(b, nt, 256), sds(b, nt, 256), sds(b, nt, 512), sds(b, nt, 128), sds(b, nt, 256),
                   sds(b, 512, nt)],
        compiler_params=_cparams(2),
        name="mixin",
    )(t, mod, g, w, cos, sin_a, sin_b)


def _attn_a_body(lam_ref, qt_ref, k_ref, vt_ref, sub_ref, o_ref, s_ref, *, nctx, k_chunk):
    j = pl.program_id(1)
    nt = k_ref.shape[0]
    row = lax.broadcasted_iota(jnp.int32, (LANES, 1), 0)
    lp = lam_ref[...]
    lam_init = lp[4:5, 0:1]
    lam = (jnp.exp(jnp.sum(lp[0:1] * lp[1:2], axis=-1, keepdims=True))
           - jnp.exp(jnp.sum(lp[2:3] * lp[3:4], axis=-1, keepdims=True)) + lam_init)
    out_gain = sub_ref[...] * (1.0 - lam_init)
    units = [(h, m) for h in range(A_HEADS) for m in range(2)]

    def attend(n_keys):
        chunks = [(0, nctx)] + [(lo, k_chunk) for lo in range(nctx, n_keys, k_chunk)]
        state = {}

        def pass1(u, lo, w):
            h, m = units[u]
            if lo == 0:
                qt = qt_ref[h * LANES:(h + 1) * LANES, :]
                state["q", u] = jnp.where((row < HEAD) if m == 0 else (row >= HEAD), qt, jnp.zeros_like(qt))
            s = _dot(k_ref[lo:lo + w, h * LANES:(h + 1) * LANES], state["q", u])
            s_ref[u % 2, lo:lo + w, :] = s
            pm = jnp.max(s, axis=0, keepdims=True)
            state["mx", u] = pm if lo == 0 else jnp.maximum(state["mx", u], pm)

        def pass2(u, lo, w):
            h, m = units[u]
            e = jnp.exp2(s_ref[u % 2, lo:lo + w, :] - state["mx", u])
            ps = jnp.sum(e, axis=0, keepdims=True)
            pv = _dot(vt_ref[h * LANES:(h + 1) * LANES, lo:lo + w], e.astype(BF16))
            state["l", u] = ps if lo == 0 else state["l", u] + ps
            state["acc", u] = pv if lo == 0 else state["acc", u] + pv

        def finish(u):
            h, m = units[u]
            res = state.pop(("acc", u)) / state.pop(("l", u))
            if m == 0:
                state["head", h] = res
            else:
                o = (state.pop(("head", h)) - lam * res).T
                o_ref[:, h * LANES:(h + 1) * LANES] = (_rms(o) * out_gain).astype(BF16)

        for lo, w in chunks:
            pass1(0, lo, w)
        for u in range(1, len(units)):
            for lo, w in chunks:
                pass1(u, lo, w)
                pass2(u - 1, lo, w)
            finish(u - 1)
        for lo, w in chunks:
            pass2(len(units) - 1, lo, w)
        finish(len(units) - 1)

    @pl.when(j == 0)
    def _():
        attend(nctx)

    @pl.when(j > 0)
    def _():
        attend(nt)


def _attn_a_call(aqt, ak, avt, lam_rows, subln, nctx):
    b, nt, width = ak.shape
    k_chunk = 512
    assert (nt - nctx) % k_chunk == 0
    return pl.pallas_call(
        functools.partial(_attn_a_body, nctx=nctx, k_chunk=k_chunk),
        grid=(b, nt // TM),
        in_specs=[pl.BlockSpec(lam_rows.shape, lambda bb, j: (0, 0)),
                  pl.BlockSpec((None, width, TM), lambda bb, j: (bb, 0, j)),
                  pl.BlockSpec((None, nt, width), lambda bb, j: (bb, 0, 0)),
                  pl.BlockSpec((None, width, nt), lambda bb, j: (bb, 0, 0)),
                  pl.BlockSpec((1, LANES), lambda bb, j: (0, 0))],
        out_specs=_tok_spec(width),
        out_shape=jax.ShapeDtypeStruct((b, nt, width), BF16),
        scratch_shapes=[pltpu.VMEM((2, nt, TM), F32)],
        compiler_params=_cparams(2),
        name="attn_a",
    )(lam_rows, aqt, ak, avt, subln)


def _dot_nt(a, b):
    return lax.dot_general(a, b, (((1,), (1,)), ((), ())), preferred_element_type=F32)


def _gla_body(q_ref, k_ref, v_ref, vt_ref, g_ref, gw_ref, gb_ref, gn_ref, o_ref, st_ref, *, nctx):
    nt = q_ref.shape[0]
    ch = GLA_CHUNK
    per_tile = TM // ch
    n_tiles = nt // TM
    n_ctx_tiles = nctx // TM

    row = lax.broadcasted_iota(jnp.int32, (TM, TM), 0)
    col = lax.broadcasted_iota(jnp.int32, (TM, TM), 1)
    same_chunk = (row // ch) == (col // ch)
    chunk_ones = jnp.where(same_chunk, 1.0, 0.0).astype(BF16)
    tok_chunk = lax.broadcasted_iota(jnp.int32, (1, TM), 1) // ch
    srow = lax.broadcasted_iota(jnp.int32, (2 * B_VDIM, 2 * B_KDIM), 0)
    scol = lax.broadcasted_iota(jnp.int32, (2 * B_VDIM, 2 * B_KDIM), 1)
    same_head = (srow < B_VDIM) == (scol < B_KDIM)
    lane = lax.broadcasted_iota(jnp.int32, (1, LANES), 1)

    def tile(d, t):
        r0 = pl.multiple_of(t * TM, TM)
        rows = pl.ds(r0, TM)
        z = _dot(g_ref[rows, :], gw_ref[d]) + gb_ref[d]
        la = (jnp.minimum(z, 0.0) - jnp.log(1.0 + jnp.exp(-jnp.abs(z)))) * (1.0 / GATE_TAU)
        earlier = jnp.logical_and(same_chunk, (col <= row) if d == 0 else (col >= row))
        tri = jnp.where(earlier, 1.0, 0.0).astype(BF16)
        hi = la.astype(BF16)
        mid = (la - hi.astype(F32)).astype(BF16)
        cum = _dot(tri, hi) + _dot(tri, mid)
        tot = _dot(chunk_ones, hi) + _dot(chunk_ones, mid)
        qf = q_ref[rows, :].astype(F32)
        kf = k_ref[rows, :].astype(F32)
        v = v_ref[rows, :]
        vt = vt_ref[:, rows]
        qe = qf * (B_KDIM ** -0.5) * jnp.exp(cum)
        ke = (kf * jnp.exp(-cum)).astype(BF16)
        kd = (kf * jnp.exp(tot - cum)).astype(BF16)
        decay = jnp.exp(tot)
        outs = []
        for h in range(2):
            q_h = jnp.where((lane < B_KDIM) if h == 0 else (lane >= B_KDIM), qe, 0.0).astype(BF16)
            att = jnp.where(earlier, _dot_nt(q_h, ke), 0.0).astype(BF16)
            outs.append(_dot(att, v[:, h * B_VDIM:(h + 1) * B_VDIM]))
        incs = [jnp.where(same_head, _dot(jnp.where(tok_chunk == c, vt, jnp.zeros_like(vt)), kd), 0.0)
                for c in range(per_tile)]
        state = st_ref[...]
        inter = [None] * per_tile
        for c in (range(per_tile) if d == 0 else reversed(range(per_tile))):
            inter[c] = _dot_nt(qe[c * ch:(c + 1) * ch].astype(BF16), state.astype(BF16))
            state = state * decay[c * ch:c * ch + 1, :] + incs[c]
        st_ref[...] = state
        return jnp.concatenate(outs, axis=1) + jnp.concatenate(inter, axis=0), rows

    st_ref[...] = jnp.zeros(st_ref.shape, F32)

    def fwd(t, carry):
        o, rows = tile(0, t)
        o_ref[rows, :] = o
        return carry

    lax.fori_loop(0, n_tiles, fwd, 0)

    st_ref[...] = jnp.zeros(st_ref.shape, F32)
    gn = gn_ref[...]

    def rev(i, carry):
        t = jnp.where(i < n_ctx_tiles, n_ctx_tiles - 1 - i, n_tiles + n_ctx_tiles - 1 - i)
        o, rows = tile(1, t)
        tot = o_ref[rows, :] + o
        for h in range(2):
            lo = h * B_VDIM
            o_ref[rows, lo:lo + B_VDIM] = _rms(tot[:, lo:lo + B_VDIM]) * gn
        return carry

    lax.fori_loop(0, n_tiles, rev, 0)


def _gla_call(bq, bk, bv, bvt, bg, gw, gb, gn, nctx):
    b, nt, _ = bq.shape
    return pl.pallas_call(
        functools.partial(_gla_body, nctx=nctx),
        grid=(b, B_HEADS // 2),
        in_specs=[pl.BlockSpec((None, nt, LANES), lambda bb, p: (bb, 0, p)),
                  pl.BlockSpec((None, nt, LANES), lambda bb, p: (bb, 0, p)),
                  pl.BlockSpec((None, nt, 2 * B_VDIM), lambda bb, p: (bb, 0, p)),
                  pl.BlockSpec((None, 2 * B_VDIM, nt), lambda bb, p: (bb, p, 0)),
                  pl.BlockSpec((None, nt, LANES), lambda bb, p: (bb, 0, 0)),
                  pl.BlockSpec((2, LANES, LANES), lambda bb, p: (0, 0, p)),
                  pl.BlockSpec((2, 1, LANES), lambda bb, p: (0, 0, p)),
                  pl.BlockSpec((1, B_VDIM), lambda bb, p: (0, 0))],
        out_specs=pl.BlockSpec((None, nt, 2 * B_VDIM), lambda bb, p: (bb, 0, p)),
        out_shape=jax.ShapeDtypeStruct((b, nt, B_HEADS * B_VDIM), F32),
        scratch_shapes=[pltpu.VMEM((2 * B_VDIM, 2 * B_KDIM), F32)],
        compiler_params=_cparams(2),
        name="gla",
    )(bq, bk, bv, bvt, bg, gw, gb, gn)


def _attn_c_body(sink_ref, q_ref, kt_ref, v_ref, o_ref, *, nctx, span):
    j = pl.program_id(1)
    nlat = kt_ref.shape[1] - nctx
    lane = lax.broadcasted_iota(jnp.int32, (1, LANES), 1)
    lower = lane < HEAD
    heads_per_kv = C_HEADS // C_KV_HEADS

    def head_q(h):
        q2 = q_ref[:, (h // 2) * LANES:(h // 2 + 1) * LANES]
        return jnp.where(lower if h % 2 == 0 else jnp.logical_not(lower), q2, jnp.zeros_like(q2))

    def store_pair(h, o_even, o_odd):
        lo = (h // 2) * LANES
        o_ref[:, lo:lo + LANES] = jnp.where(lower, o_even, o_odd).astype(BF16)

    @pl.when(j == 0)
    def _():
        prev = None
        for h in range(C_HEADS):
            kv = h // heads_per_kv
            s = _dot(head_q(h), kt_ref[kv * LANES:(kv + 1) * LANES, 0:nctx])
            sink = sink_ref[h]
            mx = jnp.maximum(jnp.max(s, axis=-1, keepdims=True), sink)
            e = jnp.exp(s - mx)
            l = jnp.sum(e, axis=-1, keepdims=True) + jnp.exp(sink - mx)
            o = _dot(e.astype(BF16), v_ref[0:nctx, kv * LANES:(kv + 1) * LANES]) / l
            if h % 2 == 1:
                store_pair(h, prev, o)
            prev = o

    @pl.when(j > 0)
    def _():
        q0 = (j - 1) * TM
        k0 = jnp.clip(q0 - WINDOW, 0, nlat - span)
        start = pl.multiple_of(nctx + k0, LANES)
        qpos = q0 + lax.broadcasted_iota(jnp.int32, (TM, span), 0)
        kpos = k0 + lax.broadcasted_iota(jnp.int32, (TM, span), 1)
        valid = jnp.abs(qpos - kpos) <= WINDOW
        prev = None
        for h in range(C_HEADS):
            kv = h // heads_per_kv
            kvl = slice(kv * LANES, (kv + 1) * LANES)
            qh = head_q(h)
            s_loc = jnp.where(valid, _dot(qh, kt_ref[kvl, pl.ds(start, span)]), NEG_INF)
            s_ctx = _dot(qh, kt_ref[kvl, 0:nctx])
            sink = sink_ref[h]
            mx = jnp.maximum(jnp.maximum(jnp.max(s_loc, axis=-1, keepdims=True),
                                         jnp.max(s_ctx, axis=-1, keepdims=True)), sink)
            e_loc = jnp.exp(s_loc - mx)
            e_ctx = jnp.exp(s_ctx - mx)
            l = (jnp.sum(e_loc, axis=-1, keepdims=True) + jnp.sum(e_ctx, axis=-1, keepdims=True)
                 + jnp.exp(sink - mx))
            o = (_dot(e_loc.astype(BF16), v_ref[pl.ds(start, span), kvl])
                 + _dot(e_ctx.astype(BF16), v_ref[0:nctx, kvl])) / l
            if h % 2 == 1:
                store_pair(h, prev, o)
            prev = o


def _attn_c_call(cq, ckt, cv, sink, nctx):
    b, nt, _ = cq.shape
    span = TM + 2 * WINDOW
    assert nt - nctx >= span
    return pl.pallas_call(
        functools.partial(_attn_c_body, nctx=nctx, span=span),
        grid=(b, nt // TM),
        in_specs=[pl.BlockSpec(memory_space=pltpu.SMEM),
                  _tok_spec(C_HEADS * HEAD),
                  pl.BlockSpec((None, C_KV_HEADS * LANES, nt), lambda bb, j: (bb, 0, 0)),
                  pl.BlockSpec((None, nt, C_KV_HEADS * LANES), lambda bb, j: (bb, 0, 0))],
        out_specs=_tok_spec(C_HEADS * HEAD),
        out_shape=jax.ShapeDtypeStruct(cq.shape, BF16),
        compiler_params=_cparams(2),
        name="attn_c",
    )(sink, cq, ckt, cv)


def _merge_body(x_ref, mod_ref, g_ref, wg_ref, wa_ref, wb_ref, wc_ref, wo_ref,
                oa_ref, ob_ref, oc_ref, o_ref):
    x = x_ref[...]
    d = x.shape[1]
    mod = mod_ref[...]
    hb = _norm_mod(x, g_ref[...], mod[3:4], mod[4:5]).astype(BF16)
    r = _dot(hb, wg_ref[:, 3 * d:])
    ob = (ob_ref[...] * jax.nn.silu(r)).astype(BF16)
    y = jax.nn.sigmoid(_dot(hb, wg_ref[:, 0:d])) * _dot(oa_ref[...], wa_ref[...])
    y = y + jax.nn.sigmoid(_dot(hb, wg_ref[:, d:2 * d])) * _dot(ob, wb_ref[...])
    y = y + jax.nn.sigmoid(_dot(hb, wg_ref[:, 2 * d:3 * d])) * _dot(oc_ref[...], wc_ref[...])
    o_ref[...] = x + mod[5:6] * _dot(y.astype(BF16), wo_ref[...])


def _merge_call(t, mod, g, wg, wa, wb, wc, wo, oa, ob, oc, ctx_row):
    b, nt, d = t.shape
    return pl.pallas_call(
        _merge_body,
        grid=(b, nt // TM),
        in_specs=[_tok_spec(d), _mod_spec(d, ctx_row), _const_spec((1, d)), _const_spec(wg.shape),
                  _const_spec(wa.shape), _const_spec(wb.shape), _const_spec(wc.shape), _const_spec(wo.shape),
                  _tok_spec(oa.shape[2]), _tok_spec(ob.shape[2]), _tok_spec(oc.shape[2])],
        out_specs=_tok_spec(d),
        out_shape=jax.ShapeDtypeStruct(t.shape, F32),
        compiler_params=_cparams(2),
        name="merge",
    )(t, mod, g, wg, wa, wb, wc, wo, oa, ob, oc)


def _final_body(x_ref, g_ref, o_ref):
    o_ref[...] = _rms(x_ref[...]) * g_ref[...]


def _final_call(t, g, nctx):
    b, nt, d = t.shape
    skip = nctx // TM
    return pl.pallas_call(
        _final_body,
        grid=(b, nt // TM - skip),
        in_specs=[pl.BlockSpec((None, TM, d), lambda bb, j: (bb, j + skip, 0)), _const_spec((1, d))],
        out_specs=_tok_spec(d),
        out_shape=jax.ShapeDtypeStruct((b, nt - nctx, d), F32),
        compiler_params=_cparams(2),
        name="final_norm",
    )(t, g)


def _rope_tables(n_lat, nctx):
    rows = n_lat // GRID_W
    row = jnp.repeat(jnp.arange(rows, dtype=F32), GRID_W)
    col = jnp.tile(jnp.arange(GRID_W, dtype=F32), rows)
    n_freq = HEAD // 4
    freqs = jnp.power(ROPE_BASE, -jnp.arange(n_freq, dtype=F32) / n_freq)
    ar = row[:, None] * freqs
    ac = col[:, None] * freqs
    ang = jnp.concatenate([ar, ar, ac, ac], axis=-1)
    ang = jnp.concatenate([ang, ang], axis=-1)
    first = (jnp.arange(LANES) % 32) < 16
    cos, sin = jnp.cos(ang), jnp.sin(ang)
    sin_a = jnp.where(first, -sin, 0.0)
    sin_b = jnp.where(first, 0.0, sin)
    pad = lambda a, v: jnp.concatenate([jnp.full((nctx, LANES), v, F32), a], axis=0)
    return pad(cos, 1.0), pad(sin_a, 0.0), pad(sin_b, 0.0)


def _mixin_weight(w):
    aq, ak, av = w[:, 0:512], w[:, 512:1024], w[:, 1024:1536]
    bq, bk, bv = w[:, 1536:1792], w[:, 1792:2048], w[:, 2048:2560]
    bg = jnp.pad(w[:, 2560:2560 + 2 * GATE_RANK], ((0, 0), (0, LANES - 2 * GATE_RANK)))
    cq = w[:, 3104:3616]
    ck = [w[:, 3616 + i * HEAD:3616 + (i + 1) * HEAD] for i in range(C_KV_HEADS)]
    cv = [w[:, 3744 + i * HEAD:3744 + (i + 1) * HEAD] for i in range(C_KV_HEADS)]
    cols = [aq, ak, cq, ck[0], ck[0], ck[1], ck[1], av, bq, bk, bv, bg, cv[0], cv[0], cv[1], cv[1]]
    return jnp.concatenate(cols, axis=1).astype(BF16)


def kernel(x, c, ctx, c_ctx, w_ada, b_ada, norm_g, w_ffn1_in, w_ffn1_out, w_ffn2_in, w_ffn2_out,
           w_mix_in, diff_lambda, diff_subln, gla_gate_w, gla_gate_b, gla_norm, swa_sink,
           w_br_a, w_br_b, w_br_c, w_mix_out, final_g):
    bsz, n, d = x.shape
    nctx = ctx.shape[1]
    depth = w_ada.shape[0]
    ffn = w_ffn1_out.shape[1]
    assert nctx == TM and n % 512 == 0 and d % LANES == 0

    t = jnp.concatenate([ctx, x], axis=1)
    ctx_row = bsz
    n_rows = -(-(bsz + 1) // 8) * 8
    cvec = jnp.concatenate([c, c_ctx[None, :], jnp.zeros((n_rows - bsz - 1, d), F32)], axis=0)
    mod_all = _ada_call(cvec, w_ada, b_ada).reshape(depth, n_rows, N_MOD, d)
    cos, sin_a, sin_b = _rope_tables(n, nctx)
    gate_cols = 2560 + 2 * GATE_RANK
    merge_cols = 3872

    for l in range(depth):
        lam_init = 0.8 - 0.6 * math.exp(-0.3 * l)
        mod = mod_all[l]
        ng = norm_g[l]
        t = _ffn_call(t, mod, ng[0:1], w_ffn1_in[l, :, :ffn].astype(BF16), w_ffn1_in[l, :, ffn:].astype(BF16),
                      w_ffn1_out[l].astype(BF16), 0, ctx_row)
        aqt, ak, cq, ckt, avt, bq, bk, bv, bg, cv, bvt = _mixin_call(
            t, mod, ng[1:2], _mixin_weight(w_mix_in[l]), cos, sin_a, sin_b, ctx_row)
        lam_rows = jnp.concatenate([diff_lambda[l], jnp.full((4, HEAD), lam_init, F32)], axis=0)
        oa = _attn_a_call(aqt, ak, avt, lam_rows, diff_subln[l][None, :], nctx)
        gw = jnp.zeros((2, LANES, B_HEADS * B_KDIM), F32)
        for dd in range(2):
            gw = gw.at[dd, dd * GATE_RANK:(dd + 1) * GATE_RANK].set(gla_gate_w[l, dd])
        ob = _gla_call(bq, bk, bv, bvt, bg, gw.astype(BF16), gla_gate_b[l][:, None, :], gla_norm[l][None, :], nctx)
        oc = _attn_c_call(cq, ckt, cv, swa_sink[l], nctx)
        wg = jnp.concatenate([w_mix_in[l, :, merge_cols:], w_mix_in[l, :, gate_cols:gate_cols + 512]],
                             axis=1).astype(BF16)
        t = _merge_call(t, mod, ng[1:2], wg, w_br_a[l].astype(BF16), w_br_b[l].astype(BF16),
                        w_br_c[l].astype(BF16), w_mix_out[l].astype(BF16), oa, ob, oc, ctx_row)
        t = _ffn_call(t, mod, ng[2:3], w_ffn2_in[l, :, :ffn].astype(BF16), w_ffn2_in[l, :, ffn:].astype(BF16),
                      w_ffn2_out[l].astype(BF16), 6, ctx_row)
    return _final_call(t, final_g[None, :], nctx)
```

```python
import functools
import math

import jax
import jax.numpy as jnp
from jax import lax
from jax.experimental import pallas as pl
from jax.experimental.pallas import tpu as pltpu

F32 = jnp.float32
BF16 = jnp.bfloat16

HEAD = 64
ROPE_BASE = 10000.0
GRID_W = 64
N_MOD = 9
A_HEADS = 4
B_HEADS = 4
B_KDIM = 64
B_VDIM = 128
GATE_RANK = 16
GATE_TAU = 16.0
GLA_CHUNK = 64
C_HEADS = 8
C_KV_HEADS = 2
WINDOW = 128
NEG_INF = -1e30
EPS = 1e-6

TM = 256
LANES = 128
VMEM_LIMIT = 56 * 1024 * 1024

_AQ, _AK, _CQ, _CK, _AV, _BQ, _BK, _BV, _BG, _CV, _MIX_COLS = (
    0, 512, 1024, 1536, 1664, 2176, 2432, 2688, 3200, 3328, 3456)


def _cparams(n_grid, vmem=VMEM_LIMIT):
    return pltpu.CompilerParams(dimension_semantics=("arbitrary",) * n_grid, vmem_limit_bytes=vmem)


def _const_spec(shape):
    nd = len(shape)
    return pl.BlockSpec(shape, lambda *_: (0,) * nd, pipeline_mode=pl.Buffered(1))


def _rms(x):
    return x * lax.rsqrt(jnp.mean(x * x, axis=-1, keepdims=True) + EPS)


def _norm_mod(x, g, shift, scale):
    return _rms(x) * g * (1.0 + scale) + shift


def _dot(a, b):
    return jnp.dot(a, b, preferred_element_type=F32)


def _ada_body(c_ref, w_ref, b_ref, o_ref):
    sc = jax.nn.silu(c_ref[...]).astype(BF16)
    o_ref[...] = _dot(sc, w_ref[...].astype(BF16)) + b_ref[...]


def _ada_call(cvec, w_ada, b_ada):
    depth, d, n_out = w_ada.shape
    rows = cvec.shape[0]
    tn = n_out // 4
    return pl.pallas_call(
        _ada_body,
        grid=(depth, n_out // tn),
        in_specs=[pl.BlockSpec((rows, d), lambda l, n: (0, 0)),
                  pl.BlockSpec((None, d, tn), lambda l, n: (l, 0, n)),
                  pl.BlockSpec((None, 1, tn), lambda l, n: (l, 0, n))],
        out_specs=pl.BlockSpec((None, rows, tn), lambda l, n: (l, 0, n)),
        out_shape=jax.ShapeDtypeStruct((depth, rows, n_out), F32),
        compiler_params=_cparams(2),
        name="adaln",
    )(cvec, w_ada, b_ada.reshape(depth, 1, n_out))


def _tok_spec(width, skip=0):
    return pl.BlockSpec((None, TM, width), lambda b, j: (b, j + skip, 0))


def _mod_spec(d, ctx_row):
    if ctx_row is None:
        return pl.BlockSpec((None, N_MOD, d), lambda b, j: (b, 0, 0))
    return pl.BlockSpec((None, N_MOD, d), lambda b, j: (jnp.where(j == 0, ctx_row, b), 0, 0))


def _ffn_body(*refs, mod0, f_chunk, split_input, final_norm):
    refs = list(refs)
    if split_input:
        ctx_ref, lat_ref = refs[0:2]
        x = jnp.where(pl.program_id(1) == 0, ctx_ref[...], lat_ref[...])
        refs = refs[2:]
    else:
        x = refs[0][...]
        refs = refs[1:]
    mod_ref, g_ref, wu_ref, wv_ref, wd_ref = refs[0:5]
    o_ref = refs[-1]
    mod = mod_ref[...]
    hb = _norm_mod(x, g_ref[...], mod[mod0:mod0 + 1], mod[mod0 + 1:mod0 + 2]).astype(BF16)
    f = wu_ref.shape[1]
    acc = jnp.zeros(x.shape, F32)
    for lo in range(0, f, f_chunk):
        u = _dot(hb, wu_ref[:, lo:lo + f_chunk])
        v = _dot(hb, wv_ref[:, lo:lo + f_chunk])
        a = (jax.nn.silu(u) * v).astype(BF16)
        acc = acc + _dot(a, wd_ref[lo:lo + f_chunk, :])
    y = x + 0.5 * mod[mod0 + 2:mod0 + 3] * acc
    if final_norm:
        y = _rms(y) * refs[5][...]
    o_ref[...] = y


def _ffn_call(t, mod, g, wu, wv, wd, mod0, ctx_row, final_g=None):
    split_input = isinstance(t, tuple)
    d, f = wu.shape
    if split_input:
        ctx, lat = t
        b, n_rows = lat.shape[0], ctx.shape[1] + lat.shape[1]
        acts = [ctx, lat]
        act_specs = [pl.BlockSpec((None, TM, d), lambda bb, j: (bb, 0, 0)),
                     pl.BlockSpec((None, TM, d), lambda bb, j: (bb, jnp.maximum(j - 1, 0), 0))]
    else:
        b, n_rows = t.shape[0], t.shape[1]
        acts, act_specs = [t], [_tok_spec(d)]
    extra, extra_specs = ([], []) if final_g is None else ([final_g], [_const_spec((1, d))])
    return pl.pallas_call(
        functools.partial(_ffn_body, mod0=mod0, f_chunk=f // 2, split_input=split_input,
                          final_norm=final_g is not None),
        grid=(b, n_rows // TM),
        in_specs=act_specs + [_mod_spec(d, ctx_row), _const_spec((1, d)),
                              _const_spec((d, f)), _const_spec((d, f)), _const_spec((f, d))] + extra_specs,
        out_specs=_tok_spec(d),
        out_shape=jax.ShapeDtypeStruct((b, n_rows, d), F32),
        compiler_params=_cparams(2),
        name="ffn",
    )(*acts, mod, g, wu, wv, wd, *extra)


def _rope128(x, cos, sin_a, sin_b):
    return x * cos + pltpu.roll(x, LANES - 16, 1) * sin_a + pltpu.roll(x, 16, 1) * sin_b


def _mixin_body(x_ref, mod_ref, g_ref, w_ref, cos_ref, sa_ref, sb_ref,
                aqt_ref, ak_ref, cqt_ref, ck_ref, avt_ref, bq_ref, bk_ref, bv_ref, bg_ref, cvt_ref, bvt_ref):
    mod = mod_ref[...]
    hb = _norm_mod(x_ref[...], g_ref[...], mod[3:4], mod[4:5]).astype(BF16)
    cos, sin_a, sin_b = cos_ref[...], sa_ref[...], sb_ref[...]
    q_scale = HEAD ** -0.5 * math.log2(math.e)

    def proj(lo, hi):
        return _dot(hb, w_ref[:, lo:hi])

    def roped(p, i):
        return _rope128(p[:, i * LANES:(i + 1) * LANES], cos, sin_a, sin_b)

    p_aq, p_ak, p_cqk, p_av = proj(_AQ, _AK), proj(_AK, _CQ), proj(_CQ, _AV), proj(_AV, _BQ)
    for i in range(4):
        lo = i * LANES
        aqt_ref[lo:lo + LANES, :] = (roped(p_aq, i) * q_scale).T.astype(BF16)
        cqt_ref[lo:lo + LANES, :] = (roped(p_cqk, i) * q_scale).T.astype(BF16)
        ak_ref[:, lo:lo + LANES] = roped(p_ak, i).astype(BF16)
        avt_ref[lo:lo + LANES, :] = p_av[:, lo:lo + LANES].T.astype(BF16)
    ck_ref[...] = roped(p_cqk, 4).astype(BF16)
    bq_ref[...] = proj(_BQ, _BK).astype(BF16)
    bk_ref[...] = proj(_BK, _BV).astype(BF16)
    bv = proj(_BV, _BG)
    bv_ref[...] = bv.astype(BF16)
    bvt_ref[...] = bv.T.astype(BF16)
    p_gc = proj(_BG, _MIX_COLS)
    bg_ref[...] = p_gc[:, 0:_CV - _BG].astype(BF16)
    cvt_ref[...] = p_gc[:, _CV - _BG:].T.astype(BF16)


def _mixin_call(t, mod, g, w, cos, sin_a, sin_b, ctx_row):
    b, nt, d = t.shape
    tbl = pl.BlockSpec((TM, LANES), lambda bb, j: (j, 0))

    def tposed(rows):
        return pl.BlockSpec((None, rows, TM), lambda bb, j: (bb, 0, j))

    def sds(*shape):
        return jax.ShapeDtypeStruct(shape, BF16)

    return pl.pallas_call(
        _mixin_body,
        grid=(b, nt // TM),
        in_specs=[_tok_spec(d), _mod_spec(d, ctx_row), _const_spec((1, d)), _const_spec((d, _MIX_COLS)),
                  tbl, tbl, tbl],
        out_specs=[tposed(512), _tok_spec(512), tposed(512), _tok_spec(128), tposed(512),
                   _tok_spec(256), _tok_spec(256), _tok_spec(512), _tok_spec(128), tposed(128), tposed(512)],
        out_shape=[sds(b, 512, nt), sds(b, nt, 512), sds(b, 512, nt), sds(b, nt, 128), sds(b, 512, nt),
                   sds(b, nt, 256), sds(b, nt, 256), sds(b, nt, 512), sds(b, nt, 128), sds(b, 128, nt),
                   sds(b, 512, nt)],
        compiler_params=_cparams(2),
        name="mixin",
    )(t, mod, g, w, cos, sin_a, sin_b)


def _attn_a_body(lam_ref, qt_ref, k_ref, vt_ref, sub_ref, o_ref, s_ref, *, nctx, k_chunk):
    j = pl.program_id(1)
    nt = k_ref.shape[0]
    row = lax.broadcasted_iota(jnp.int32, (LANES, 1), 0)
    lp = lam_ref[...]
    lam_init = lp[4:5, 0:1]
    lam = (jnp.exp(jnp.sum(lp[0:1] * lp[1:2], axis=-1, keepdims=True))
           - jnp.exp(jnp.sum(lp[2:3] * lp[3:4], axis=-1, keepdims=True)) + lam_init)
    out_gain = sub_ref[...] * (1.0 - lam_init)
    units = [(h, m) for h in range(A_HEADS) for m in range(2)]

    def attend(n_keys):
        chunks = [(0, nctx)] + [(lo, k_chunk) for lo in range(nctx, n_keys, k_chunk)]
        state = {}

        def pass1(u, lo, w):
            h, m = units[u]
            if lo == 0:
                qt = qt_ref[h * LANES:(h + 1) * LANES, :]
                state["q", u] = jnp.where((row < HEAD) if m == 0 else (row >= HEAD), qt, jnp.zeros_like(qt))
            s = _dot(k_ref[lo:lo + w, h * LANES:(h + 1) * LANES], state["q", u])
            s_ref[u % 2, lo:lo + w, :] = s
            pm = jnp.max(s, axis=0, keepdims=True)
            state["mx", u] = pm if lo == 0 else jnp.maximum(state["mx", u], pm)

        def pass2(u, lo, w):
            h, m = units[u]
            e = jnp.exp2(s_ref[u % 2, lo:lo + w, :] - state["mx", u])
            ps = jnp.sum(e, axis=0, keepdims=True)
            pv = _dot(vt_ref[h * LANES:(h + 1) * LANES, lo:lo + w], e.astype(BF16))
            state["l", u] = ps if lo == 0 else state["l", u] + ps
            state["acc", u] = pv if lo == 0 else state["acc", u] + pv

        def finish(u):
            h, m = units[u]
            res = state.pop(("acc", u)) / state.pop(("l", u))
            if m == 0:
                state["head", h] = res
            else:
                o = (state.pop(("head", h)) - lam * res).T
                o_ref[:, h * LANES:(h + 1) * LANES] = (_rms(o) * out_gain).astype(BF16)

        for lo, w in chunks:
            pass1(0, lo, w)
        for u in range(1, len(units)):
            for lo, w in chunks:
                pass1(u, lo, w)
                pass2(u - 1, lo, w)
            finish(u - 1)
        for lo, w in chunks:
            pass2(len(units) - 1, lo, w)
        finish(len(units) - 1)

    @pl.when(j == 0)
    def _():
        attend(nctx)

    @pl.when(j > 0)
    def _():
        attend(nt)


def _attn_a_call(aqt, ak, avt, lam_rows, subln, nctx):
    b, nt, width = ak.shape
    k_chunk = 512
    assert (nt - nctx) % k_chunk == 0
    return pl.pallas_call(
        functools.partial(_attn_a_body, nctx=nctx, k_chunk=k_chunk),
        grid=(b, nt // TM),
        in_specs=[pl.BlockSpec(lam_rows.shape, lambda bb, j: (0, 0)),
                  pl.BlockSpec((None, width, TM), lambda bb, j: (bb, 0, j)),
                  pl.BlockSpec((None, nt, width), lambda bb, j: (bb, 0, 0)),
                  pl.BlockSpec((None, width, nt), lambda bb, j: (bb, 0, 0)),
                  pl.BlockSpec((1, LANES), lambda bb, j: (0, 0))],
        out_specs=_tok_spec(width),
        out_shape=jax.ShapeDtypeStruct((b, nt, width), BF16),
        scratch_shapes=[pltpu.VMEM((2, nt, TM), F32)],
        compiler_params=_cparams(2),
        name="attn_a",
    )(lam_rows, aqt, ak, avt, subln)


def _dot_nt(a, b):
    return lax.dot_general(a, b, (((1,), (1,)), ((), ())), preferred_element_type=F32)


def _gla_body(q_ref, k_ref, v_ref, vt_ref, g_ref, gw_ref, gb_ref, gn_ref, o_ref, rev_ref, st_ref, *, nctx):
    nt = q_ref.shape[0]
    ch = GLA_CHUNK
    per_tile = TM // ch
    n_tiles = nt // TM
    n_ctx_tiles = nctx // TM

    row = lax.broadcasted_iota(jnp.int32, (TM, TM), 0)
    col = lax.broadcasted_iota(jnp.int32, (TM, TM), 1)
    same_chunk = (row // ch) == (col // ch)
    chunk_ones = jnp.where(same_chunk, 1.0, 0.0).astype(BF16)
    tok_chunk = lax.broadcasted_iota(jnp.int32, (TM, 1), 0) // ch
    srow = lax.broadcasted_iota(jnp.int32, (2 * B_VDIM, 2 * B_KDIM), 0)
    scol = lax.broadcasted_iota(jnp.int32, (2 * B_VDIM, 2 * B_KDIM), 1)
    same_head = (srow < B_VDIM) == (scol < B_KDIM)
    lane = lax.broadcasted_iota(jnp.int32, (1, LANES), 1)

    dirs = (0, 1)
    earlier = [jnp.logical_and(same_chunk, (col <= row) if d == 0 else (col >= row)) for d in dirs]
    tri = [jnp.where(e, 1.0, 0.0).astype(BF16) for e in earlier]
    half = [lane < B_KDIM, lane >= B_KDIM]

    def tile_pair(tiles):
        rows = [pl.ds(pl.multiple_of(t * TM, TM), TM) for t in tiles]
        z = [_dot(g_ref[rows[d], :], gw_ref[d]) + gb_ref[d] for d in dirs]
        la = [(jnp.minimum(x, 0.0) - jnp.log(1.0 + jnp.exp(-jnp.abs(x)))) * (1.0 / GATE_TAU) for x in z]
        hi = [x.astype(BF16) for x in la]
        pieces = [jnp.concatenate([hi[d], (la[d] - hi[d].astype(F32)).astype(BF16)], axis=1) for d in dirs]
        cum = [_dot(tri[d], pieces[d]) for d in dirs]
        cum = [x[:, :LANES] + x[:, LANES:] for x in cum]
        tot = [_dot(chunk_ones, pieces[d]) for d in dirs]
        tot = [x[:, :LANES] + x[:, LANES:] for x in tot]
        qf = [q_ref[rows[d], :].astype(F32) for d in dirs]
        kf = [k_ref[rows[d], :].astype(F32) for d in dirs]
        v = [v_ref[rows[d], :] for d in dirs]
        vt = [vt_ref[:, rows[d]] for d in dirs]
        qe = [qf[d] * (B_KDIM ** -0.5) * jnp.exp(cum[d]) for d in dirs]
        ke = [(kf[d] * jnp.exp(-cum[d])).astype(BF16) for d in dirs]
        kd = [(kf[d] * jnp.exp(tot[d] - cum[d])).astype(BF16) for d in dirs]
        decay = [jnp.exp(x) for x in tot]
        q_h = [[jnp.where(half[h], qe[d], 0.0).astype(BF16) for h in range(2)] for d in dirs]
        att = [[jnp.where(earlier[d], _dot_nt(q_h[d][h], ke[d]), 0.0).astype(BF16) for h in range(2)]
               for d in dirs]
        intra = [[_dot(att[d][h], v[d][:, h * B_VDIM:(h + 1) * B_VDIM]) for h in range(2)] for d in dirs]
        kd_c = [[jnp.where(tok_chunk == c, kd[d], jnp.zeros_like(kd[d])) for c in range(per_tile)] for d in dirs]
        inc2 = [[_dot(vt[d], jnp.concatenate(kd_c[d][c:c + 2], axis=1)) for c in range(0, per_tile, 2)]
                for d in dirs]
        incs = [[jnp.where(same_head, inc2[d][c // 2][:, (c % 2) * LANES:(c % 2 + 1) * LANES], 0.0)
                 for c in range(per_tile)] for d in dirs]
        state = [st_ref[d] for d in dirs]
        inter = [[None] * per_tile for _ in dirs]
        for step in range(per_tile):
            for d in dirs:
                c = step if d == 0 else per_tile - 1 - step
                inter[d][c] = _dot_nt(qe[d][c * ch:(c + 1) * ch].astype(BF16), state[d].astype(BF16))
                state[d] = state[d] * decay[d][c * ch:c * ch + 1, :] + incs[d][c]
        for d in dirs:
            st_ref[d] = state[d]
        return [jnp.concatenate(intra[d], axis=1) + jnp.concatenate(inter[d], axis=0) for d in dirs], rows

    st_ref[...] = jnp.zeros(st_ref.shape, F32)

    def scan(i, carry):
        t_rev = jnp.where(i < n_ctx_tiles, n_ctx_tiles - 1 - i, n_tiles + n_ctx_tiles - 1 - i)
        o, rows = tile_pair((i, t_rev))
        o_ref[rows[0], :] = o[0]
        rev_ref[rows[1], :] = o[1]
        return carry

    lax.fori_loop(0, n_tiles, scan, 0)
    gn = gn_ref[...]

    def combine(t, carry):
        rows = pl.ds(pl.multiple_of(t * TM, TM), TM)
        tot = o_ref[rows, :] + rev_ref[rows, :]
        for h in range(2):
            lo = h * B_VDIM
            o_ref[rows, lo:lo + B_VDIM] = _rms(tot[:, lo:lo + B_VDIM]) * gn
        return carry

    lax.fori_loop(0, n_tiles, combine, 0)


def _gla_call(bq, bk, bv, bvt, bg, gw, gb, gn, nctx):
    b, nt, _ = bq.shape
    return pl.pallas_call(
        functools.partial(_gla_body, nctx=nctx),
        grid=(b, B_HEADS // 2),
        in_specs=[pl.BlockSpec((None, nt, LANES), lambda bb, p: (bb, 0, p)),
                  pl.BlockSpec((None, nt, LANES), lambda bb, p: (bb, 0, p)),
                  pl.BlockSpec((None, nt, 2 * B_VDIM), lambda bb, p: (bb, 0, p)),
                  pl.BlockSpec((None, 2 * B_VDIM, nt), lambda bb, p: (bb, p, 0)),
                  pl.BlockSpec((None, nt, LANES), lambda bb, p: (bb, 0, 0)),
                  pl.BlockSpec((2, LANES, LANES), lambda bb, p: (0, 0, p)),
                  pl.BlockSpec((2, 1, LANES), lambda bb, p: (0, 0, p)),
                  pl.BlockSpec((1, B_VDIM), lambda bb, p: (0, 0))],
        out_specs=pl.BlockSpec((None, nt, 2 * B_VDIM), lambda bb, p: (bb, 0, p)),
        out_shape=jax.ShapeDtypeStruct((b, nt, B_HEADS * B_VDIM), F32),
        scratch_shapes=[pltpu.VMEM((nt, 2 * B_VDIM), F32), pltpu.VMEM((2, 2 * B_VDIM, 2 * B_KDIM), F32)],
        compiler_params=_cparams(2),
        name="gla",
    )(bq, bk, bv, bvt, bg, gw, gb, gn)


def _attn_c_body(sink_ref, qt_ref, k_ref, vt_ref, o_ref, *, nctx, span):
    j = pl.program_id(1)
    nlat = k_ref.shape[0] - nctx
    per_kv = C_HEADS // C_KV_HEADS
    log2e = math.log2(math.e)

    def attend(k_rows, vt_cols, valid):
        k = jnp.concatenate([k_ref[pl.ds(s0, w), :] for s0, w in k_rows], axis=0)
        n_keys = k.shape[0]
        ones = jnp.ones((8, n_keys), BF16)
        outs = []
        for g in range(C_KV_HEADS):
            blocks = []
            for i in range(per_kv):
                h = g * per_kv + i
                qh = qt_ref[h * HEAD:(h + 1) * HEAD, :]
                zero = jnp.zeros_like(qh)
                blocks.append(jnp.concatenate([qh, zero] if g == 0 else [zero, qh], axis=0))
            s = _dot(k, jnp.concatenate(blocks, axis=1))
            if valid is not None:
                w_last = valid.shape[0]
                s_last = jnp.where(jnp.concatenate([valid] * per_kv, axis=1), s[n_keys - w_last:], NEG_INF)
                s = jnp.concatenate([s[:n_keys - w_last], s_last], axis=0)
            sink = jnp.concatenate([jnp.full((1, TM), sink_ref[g * per_kv + i] * log2e, F32)
                                    for i in range(per_kv)], axis=1)
            mx = jnp.maximum(jnp.max(s, axis=0, keepdims=True), sink)
            e = jnp.exp2(s - mx).astype(BF16)
            vt = jnp.concatenate([vt_ref[g * HEAD:(g + 1) * HEAD, pl.ds(s0, w)] for s0, w in vt_cols],
                                 axis=1)
            pv = _dot(jnp.concatenate([vt, ones], axis=0), e)
            l = pv[HEAD:HEAD + 1] + jnp.exp2(sink - mx)
            o = pv[0:HEAD] / l
            outs += [o[:, i * TM:(i + 1) * TM] for i in range(per_kv)]
        o_ref[...] = jnp.concatenate(outs, axis=0).T.astype(BF16)

    @pl.when(j == 0)
    def _():
        attend([(0, nctx)], [(0, nctx)], None)

    @pl.when(j > 0)
    def _():
        q0 = (j - 1) * TM
        k0 = jnp.clip(q0 - WINDOW, 0, nlat - span)
        start = pl.multiple_of(nctx + k0, LANES)
        kpos = k0 + lax.broadcasted_iota(jnp.int32, (span, TM), 0)
        qpos = q0 + lax.broadcasted_iota(jnp.int32, (span, TM), 1)
        valid = jnp.abs(qpos - kpos) <= WINDOW
        attend([(0, nctx), (start, span)], [(0, nctx), (start, span)], valid)


def _attn_c_call(cqt, ck, cvt, sink, nctx):
    b, nt, kv_width = ck.shape
    width = cqt.shape[1]
    span = TM + 2 * WINDOW
    assert nt - nctx >= span
    return pl.pallas_call(
        functools.partial(_attn_c_body, nctx=nctx, span=span),
        grid=(b, nt // TM),
        in_specs=[pl.BlockSpec(memory_space=pltpu.SMEM),
                  pl.BlockSpec((None, width, TM), lambda bb, j: (bb, 0, j)),
                  pl.BlockSpec((None, nt, kv_width), lambda bb, j: (bb, 0, 0)),
                  pl.BlockSpec((None, kv_width, nt), lambda bb, j: (bb, 0, 0))],
        out_specs=_tok_spec(width),
        out_shape=jax.ShapeDtypeStruct((b, nt, width), BF16),
        compiler_params=_cparams(2),
        name="attn_c",
    )(sink, cqt, ck, cvt)


def _merge_body(x_ref, mod_ref, g_ref, wg_ref, wa_ref, wb_ref, wc_ref, wo_ref,
                oa_ref, ob_ref, oc_ref, o_ref):
    x = x_ref[...]
    d = x.shape[1]
    mod = mod_ref[...]
    hb = _norm_mod(x, g_ref[...], mod[3:4], mod[4:5]).astype(BF16)
    r = _dot(hb, wg_ref[:, 3 * d:])
    ob = (ob_ref[...] * jax.nn.silu(r)).astype(BF16)
    y = jax.nn.sigmoid(_dot(hb, wg_ref[:, 0:d])) * _dot(oa_ref[...], wa_ref[...])
    y = y + jax.nn.sigmoid(_dot(hb, wg_ref[:, d:2 * d])) * _dot(ob, wb_ref[...])
    y = y + jax.nn.sigmoid(_dot(hb, wg_ref[:, 2 * d:3 * d])) * _dot(oc_ref[...], wc_ref[...])
    o_ref[...] = x + mod[5:6] * _dot(y.astype(BF16), wo_ref[...])


def _merge_call(t, mod, g, wg, wa, wb, wc, wo, oa, ob, oc, ctx_row, skip_tiles=0):
    b, nt, d = t.shape
    n_tiles = nt // TM - skip_tiles
    return pl.pallas_call(
        _merge_body,
        grid=(b, n_tiles),
        in_specs=[_tok_spec(d, skip_tiles), _mod_spec(d, ctx_row), _const_spec((1, d)), _const_spec(wg.shape),
                  _const_spec(wa.shape), _const_spec(wb.shape), _const_spec(wc.shape), _const_spec(wo.shape),
                  _tok_spec(oa.shape[2], skip_tiles), _tok_spec(ob.shape[2], skip_tiles),
                  _tok_spec(oc.shape[2], skip_tiles)],
        out_specs=_tok_spec(d),
        out_shape=jax.ShapeDtypeStruct((b, n_tiles * TM, d), F32),
        compiler_params=_cparams(2),
        name="merge",
    )(t, mod, g, wg, wa, wb, wc, wo, oa, ob, oc)


def _rope_tables(n_lat, nctx):
    rows = n_lat // GRID_W
    row = jnp.repeat(jnp.arange(rows, dtype=F32), GRID_W)
    col = jnp.tile(jnp.arange(GRID_W, dtype=F32), rows)
    n_freq = HEAD // 4
    freqs = jnp.power(ROPE_BASE, -jnp.arange(n_freq, dtype=F32) / n_freq)
    ar = row[:, None] * freqs
    ac = col[:, None] * freqs
    ang = jnp.concatenate([ar, ar, ac, ac], axis=-1)
    ang = jnp.concatenate([ang, ang], axis=-1)
    first = (jnp.arange(LANES) % 32) < 16
    cos, sin = jnp.cos(ang), jnp.sin(ang)
    sin_a = jnp.where(first, -sin, 0.0)
    sin_b = jnp.where(first, 0.0, sin)
    pad = lambda a, v: jnp.concatenate([jnp.full((nctx, LANES), v, F32), a], axis=0)
    return pad(cos, 1.0), pad(sin_a, 0.0), pad(sin_b, 0.0)


def _mixin_weight(w):
    aq, ak, av = w[:, 0:512], w[:, 512:1024], w[:, 1024:1536]
    bq, bk, bv = w[:, 1536:1792], w[:, 1792:2048], w[:, 2048:2560]
    bg = jnp.pad(w[:, 2560:2560 + 2 * GATE_RANK], ((0, 0), (0, LANES - 2 * GATE_RANK)))
    cq, ck, cv = w[:, 3104:3616], w[:, 3616:3744], w[:, 3744:3872]
    return jnp.concatenate([aq, ak, cq, ck, av, bq, bk, bv, bg, cv], axis=1).astype(BF16)


def kernel(x, c, ctx, c_ctx, w_ada, b_ada, norm_g, w_ffn1_in, w_ffn1_out, w_ffn2_in, w_ffn2_out,
           w_mix_in, diff_lambda, diff_subln, gla_gate_w, gla_gate_b, gla_norm, swa_sink,
           w_br_a, w_br_b, w_br_c, w_mix_out, final_g):
    bsz, n, d = x.shape
    nctx = ctx.shape[1]
    depth = w_ada.shape[0]
    ffn = w_ffn1_out.shape[1]
    assert nctx == TM and n % 512 == 0 and d % LANES == 0

    t = (ctx, x)
    ctx_row = bsz
    n_rows = -(-(bsz + 1) // 8) * 8
    cvec = jnp.concatenate([c, c_ctx[None, :], jnp.zeros((n_rows - bsz - 1, d), F32)], axis=0)
    mod_all = _ada_call(cvec, w_ada, b_ada).reshape(depth, n_rows, N_MOD, d)
    cos, sin_a, sin_b = _rope_tables(n, nctx)
    gate_cols = 2560 + 2 * GATE_RANK
    merge_cols = 3872

    for l in range(depth):
        lam_init = 0.8 - 0.6 * math.exp(-0.3 * l)
        mod = mod_all[l]
        ng = norm_g[l]
        t = _ffn_call(t, mod, ng[0:1], w_ffn1_in[l, :, :ffn].astype(BF16), w_ffn1_in[l, :, ffn:].astype(BF16),
                      w_ffn1_out[l].astype(BF16), 0, ctx_row)
        aqt, ak, cqt, ck, avt, bq, bk, bv, bg, cvt, bvt = _mixin_call(
            t, mod, ng[1:2], _mixin_weight(w_mix_in[l]), cos, sin_a, sin_b, ctx_row)
        lam_rows = jnp.concatenate([diff_lambda[l], jnp.full((4, HEAD), lam_init, F32)], axis=0)
        oa = _attn_a_call(aqt, ak, avt, lam_rows, diff_subln[l][None, :], nctx)
        gw = jnp.zeros((2, LANES, B_HEADS * B_KDIM), F32)
        for dd in range(2):
            gw = gw.at[dd, dd * GATE_RANK:(dd + 1) * GATE_RANK].set(gla_gate_w[l, dd])
        ob = _gla_call(bq, bk, bv, bvt, bg, gw.astype(BF16), gla_gate_b[l][:, None, :], gla_norm[l][None, :], nctx)
        oc = _attn_c_call(cqt, ck, cvt, swa_sink[l], nctx)
        wg = jnp.concatenate([w_mix_in[l, :, merge_cols:], w_mix_in[l, :, gate_cols:gate_cols + 512]],
                             axis=1).astype(BF16)
        last = l == depth - 1
        tail_row, skip = (None, nctx // TM) if last else (ctx_row, 0)
        t = _merge_call(t, mod, ng[1:2], wg, w_br_a[l].astype(BF16), w_br_b[l].astype(BF16),
                        w_br_c[l].astype(BF16), w_mix_out[l].astype(BF16), oa, ob, oc, tail_row, skip)
        t = _ffn_call(t, mod, ng[2:3], w_ffn2_in[l, :, :ffn].astype(BF16), w_ffn2_in[l, :, ffn:].astype(BF16),
                      w_ffn2_out[l].astype(BF16), 6, tail_row, final_g[None, :] if last else None)
    return t
```

```python
import functools
import math

import jax
import jax.numpy as jnp
from jax import lax
from jax.experimental import pallas as pl
from jax.experimental.pallas import tpu as pltpu

F32 = jnp.float32
BF16 = jnp.bfloat16

HEAD = 64
ROPE_BASE = 10000.0
GRID_W = 64
N_MOD = 9
A_HEADS = 4
A_VROWS = 128 + 16
B_HEADS = 4
B_KDIM = 64
B_VDIM = 128
GATE_RANK = 16
GATE_TAU = 16.0
GLA_CHUNK = 64
C_HEADS = 8
C_KV_HEADS = 2
WINDOW = 128
NEG_INF = -1e30
EPS = 1e-6

TM = 256
LANES = 128
VMEM_LIMIT = 56 * 1024 * 1024

_AQ, _AK, _CQ, _CK, _AV, _BQ, _BK, _BV, _BG, _CV, _MIX_COLS = (
    0, 512, 1024, 1536, 1664, 2176, 2432, 2688, 3200, 3328, 3456)


def _cparams(n_grid, vmem=VMEM_LIMIT):
    return pltpu.CompilerParams(dimension_semantics=("arbitrary",) * n_grid, vmem_limit_bytes=vmem)


def _const_spec(shape):
    nd = len(shape)
    return pl.BlockSpec(shape, lambda *_: (0,) * nd, pipeline_mode=pl.Buffered(1))


def _rms(x):
    return x * lax.rsqrt(jnp.mean(x * x, axis=-1, keepdims=True) + EPS)


def _norm_mod(x, g, shift, scale):
    return _rms(x) * g * (1.0 + scale) + shift


def _dot(a, b):
    return jnp.dot(a, b, preferred_element_type=F32)


def _ada_body(c_ref, w_ref, b_ref, o_ref):
    sc = jax.nn.silu(c_ref[...]).astype(BF16)
    o_ref[...] = _dot(sc, w_ref[...].astype(BF16)) + b_ref[...]


def _ada_call(cvec, w_ada, b_ada):
    depth, d, n_out = w_ada.shape
    rows = cvec.shape[0]
    tn = n_out // 4
    return pl.pallas_call(
        _ada_body,
        grid=(depth, n_out // tn),
        in_specs=[pl.BlockSpec((rows, d), lambda l, n: (0, 0)),
                  pl.BlockSpec((None, d, tn), lambda l, n: (l, 0, n)),
                  pl.BlockSpec((None, 1, tn), lambda l, n: (l, 0, n))],
        out_specs=pl.BlockSpec((None, rows, tn), lambda l, n: (l, 0, n)),
        out_shape=jax.ShapeDtypeStruct((depth, rows, n_out), F32),
        compiler_params=_cparams(2),
        name="adaln",
    )(cvec, w_ada, b_ada.reshape(depth, 1, n_out))


def _tok_spec(width, skip=0):
    return pl.BlockSpec((None, TM, width), lambda b, j: (b, j + skip, 0))


def _mod_spec(d, ctx_row):
    if ctx_row is None:
        return pl.BlockSpec((None, N_MOD, d), lambda b, j: (b, 0, 0))
    return pl.BlockSpec((None, N_MOD, d), lambda b, j: (jnp.where(j == 0, ctx_row, b), 0, 0))


def _ffn_body(*refs, mod0, f_chunk, split_input, final_norm):
    refs = list(refs)
    if split_input:
        ctx_ref, lat_ref = refs[0:2]
        x = jnp.where(pl.program_id(1) == 0, ctx_ref[...], lat_ref[...])
        refs = refs[2:]
    else:
        x = refs[0][...]
        refs = refs[1:]
    mod_ref, g_ref, wu_ref, wv_ref, wd_ref = refs[0:5]
    o_ref = refs[-1]
    mod = mod_ref[...]
    hb = _norm_mod(x, g_ref[...], mod[mod0:mod0 + 1], mod[mod0 + 1:mod0 + 2]).astype(BF16)
    f = wu_ref.shape[1]
    acc = jnp.zeros(x.shape, F32)
    for lo in range(0, f, f_chunk):
        u = _dot(hb, wu_ref[:, lo:lo + f_chunk])
        v = _dot(hb, wv_ref[:, lo:lo + f_chunk])
        a = (jax.nn.silu(u) * v).astype(BF16)
        acc = acc + _dot(a, wd_ref[lo:lo + f_chunk, :])
    y = x + 0.5 * mod[mod0 + 2:mod0 + 3] * acc
    if final_norm:
        y = _rms(y) * refs[5][...]
    o_ref[...] = y


def _ffn_call(t, mod, g, wu, wv, wd, mod0, ctx_row, final_g=None):
    split_input = isinstance(t, tuple)
    d, f = wu.shape
    if split_input:
        ctx, lat = t
        b, n_rows = lat.shape[0], ctx.shape[1] + lat.shape[1]
        acts = [ctx, lat]
        act_specs = [pl.BlockSpec((None, TM, d), lambda bb, j: (bb, 0, 0)),
                     pl.BlockSpec((None, TM, d), lambda bb, j: (bb, jnp.maximum(j - 1, 0), 0))]
    else:
        b, n_rows = t.shape[0], t.shape[1]
        acts, act_specs = [t], [_tok_spec(d)]
    extra, extra_specs = ([], []) if final_g is None else ([final_g], [_const_spec((1, d))])
    return pl.pallas_call(
        functools.partial(_ffn_body, mod0=mod0, f_chunk=f // 2, split_input=split_input,
                          final_norm=final_g is not None),
        grid=(b, n_rows // TM),
        in_specs=act_specs + [_mod_spec(d, ctx_row), _const_spec((1, d)),
                              _const_spec((d, f)), _const_spec((d, f)), _const_spec((f, d))] + extra_specs,
        out_specs=_tok_spec(d),
        out_shape=jax.ShapeDtypeStruct((b, n_rows, d), F32),
        compiler_params=_cparams(2),
        name="ffn",
    )(*acts, mod, g, wu, wv, wd, *extra)


def _rope128(x, cos, sin_a, sin_b):
    return x * cos + pltpu.roll(x, LANES - 16, 1) * sin_a + pltpu.roll(x, 16, 1) * sin_b


def _mixin_body(x_ref, mod_ref, g_ref, w_ref, cos_ref, sa_ref, sb_ref,
                aqt_ref, ak_ref, cqt_ref, ck_ref, avt_ref, bq_ref, bk_ref, bv_ref, bg_ref, cvt_ref, bvt_ref):
    mod = mod_ref[...]
    hb = _norm_mod(x_ref[...], g_ref[...], mod[3:4], mod[4:5]).astype(BF16)
    cos, sin_a, sin_b = cos_ref[...], sa_ref[...], sb_ref[...]
    q_scale = HEAD ** -0.5 * math.log2(math.e)

    def proj(lo, hi):
        return _dot(hb, w_ref[:, lo:hi])

    def roped(p, i):
        return _rope128(p[:, i * LANES:(i + 1) * LANES], cos, sin_a, sin_b)

    p_aq, p_ak, p_cqk, p_av = proj(_AQ, _AK), proj(_AK, _CQ), proj(_CQ, _AV), proj(_AV, _BQ)
    for i in range(4):
        lo = i * LANES
        aqt_ref[lo:lo + LANES, :] = (roped(p_aq, i) * q_scale).T.astype(BF16)
        cqt_ref[lo:lo + LANES, :] = (roped(p_cqk, i) * q_scale).T.astype(BF16)
        ak_ref[:, lo:lo + LANES] = roped(p_ak, i).astype(BF16)
        avt_ref[i * A_VROWS:i * A_VROWS + LANES, :] = p_av[:, lo:lo + LANES].T.astype(BF16)
        avt_ref[i * A_VROWS + LANES:(i + 1) * A_VROWS, :] = jnp.ones((A_VROWS - LANES, TM), BF16)
    ck_ref[...] = roped(p_cqk, 4).astype(BF16)
    bq_ref[...] = proj(_BQ, _BK).astype(BF16)
    bk_ref[...] = proj(_BK, _BV).astype(BF16)
    bv = proj(_BV, _BG)
    bv_ref[...] = bv.astype(BF16)
    bvt_ref[...] = bv.T.astype(BF16)
    p_gc = proj(_BG, _MIX_COLS)
    bg_ref[...] = p_gc[:, 0:_CV - _BG].astype(BF16)
    cvt_ref[...] = p_gc[:, _CV - _BG:].T.astype(BF16)


def _mixin_call(t, mod, g, w, cos, sin_a, sin_b, ctx_row):
    b, nt, d = t.shape
    tbl = pl.BlockSpec((TM, LANES), lambda bb, j: (j, 0))

    def tposed(rows):
        return pl.BlockSpec((None, rows, TM), lambda bb, j: (bb, 0, j))

    def sds(*shape):
        return jax.ShapeDtypeStruct(shape, BF16)

    return pl.pallas_call(
        _mixin_body,
        grid=(b, nt // TM),
        in_specs=[_tok_spec(d), _mod_spec(d, ctx_row), _const_spec((1, d)), _const_spec((d, _MIX_COLS)),
                  tbl, tbl, tbl],
        out_specs=[tposed(512), _tok_spec(512), tposed(512), _tok_spec(128), tposed(A_HEADS * A_VROWS),
                   _tok_spec(256), _tok_spec(256), _tok_spec(512), _tok_spec(128), tposed(128), tposed(512)],
        out_shape=[sds(b, 512, nt), sds(b, nt, 512), sds(b, 512, nt), sds(b, nt, 128), sds(b, A_HEADS * A_VROWS, nt),
                   sds(b, nt, 256), sds(b, nt, 256), sds(b, nt, 512), sds(b, nt, 128), sds(b, 128, nt),
                   sds(b, 512, nt)],
        compiler_params=_cparams(2),
        name="mixin",
    )(t, mod, g, w, cos, sin_a, sin_b)


def _attn_a_body(lam_ref, qt_ref, k_ref, vt_ref, sub_ref, o_ref, s_ref, *, nctx, k_chunk):
    j = pl.program_id(1)
    nt = k_ref.shape[0]
    row = lax.broadcasted_iota(jnp.int32, (LANES, 1), 0)
    lp = lam_ref[...]
    lam_init = lp[4:5, 0:1]
    lam = (jnp.exp(jnp.sum(lp[0:1] * lp[1:2], axis=-1, keepdims=True))
           - jnp.exp(jnp.sum(lp[2:3] * lp[3:4], axis=-1, keepdims=True)) + lam_init)
    out_gain = sub_ref[...] * (1.0 - lam_init)
    units = [(h, m) for h in range(A_HEADS) for m in range(2)]

    def attend(n_keys):
        chunks = [(0, nctx)] + [(lo, k_chunk) for lo in range(nctx, n_keys, k_chunk)]
        state = {}

        def pass1(u, lo, w):
            h, m = units[u]
            if lo == 0:
                qt = qt_ref[h * LANES:(h + 1) * LANES, :]
                state["q", u] = jnp.where((row < HEAD) if m == 0 else (row >= HEAD), qt, jnp.zeros_like(qt))
            s = _dot(k_ref[lo:lo + w, h * LANES:(h + 1) * LANES], state["q", u])
            s_ref[u % 2, lo:lo + w, :] = s
            pm = jnp.max(s, axis=0, keepdims=True)
            state["mx", u] = pm if lo == 0 else jnp.maximum(state["mx", u], pm)

        def pass2(u, lo, w):
            h, m = units[u]
            e = jnp.exp2(s_ref[u % 2, lo:lo + w, :] - state["mx", u])
            pv = _dot(vt_ref[h * A_VROWS:(h + 1) * A_VROWS, lo:lo + w], e.astype(BF16))
            state["acc", u] = pv if lo == 0 else state["acc", u] + pv

        def finish(u):
            h, m = units[u]
            acc = state.pop(("acc", u))
            res = acc[0:LANES] / acc[LANES:LANES + 1]
            if m == 0:
                state["head", h] = res
            else:
                o = (state.pop(("head", h)) - lam * res).T
                o_ref[:, h * LANES:(h + 1) * LANES] = (_rms(o) * out_gain).astype(BF16)

        for lo, w in chunks:
            pass1(0, lo, w)
        for u in range(1, len(units)):
            for lo, w in chunks:
                pass1(u, lo, w)
                pass2(u - 1, lo, w)
            finish(u - 1)
        for lo, w in chunks:
            pass2(len(units) - 1, lo, w)
        finish(len(units) - 1)

    @pl.when(j == 0)
    def _():
        attend(nctx)

    @pl.when(j > 0)
    def _():
        attend(nt)


def _attn_a_call(aqt, ak, avt, lam_rows, subln, nctx):
    b, nt, width = ak.shape
    k_chunk = 512
    assert (nt - nctx) % k_chunk == 0
    return pl.pallas_call(
        functools.partial(_attn_a_body, nctx=nctx, k_chunk=k_chunk),
        grid=(b, nt // TM),
        in_specs=[pl.BlockSpec(lam_rows.shape, lambda bb, j: (0, 0)),
                  pl.BlockSpec((None, width, TM), lambda bb, j: (bb, 0, j)),
                  pl.BlockSpec((None, nt, width), lambda bb, j: (bb, 0, 0)),
                  pl.BlockSpec((None, avt.shape[1], nt), lambda bb, j: (bb, 0, 0)),
                  pl.BlockSpec((1, LANES), lambda bb, j: (0, 0))],
        out_specs=_tok_spec(width),
        out_shape=jax.ShapeDtypeStruct((b, nt, width), BF16),
        scratch_shapes=[pltpu.VMEM((2, nt, TM), F32)],
        compiler_params=_cparams(2),
        name="attn_a",
    )(lam_rows, aqt, ak, avt, subln)


def _dot_nt(a, b):
    return lax.dot_general(a, b, (((1,), (1,)), ((), ())), preferred_element_type=F32)


def _gla_body(q_ref, k_ref, v_ref, vt_ref, g_ref, gw_ref, gb_ref, gn_ref, o_ref, rev_ref, st_ref, *, nctx):
    nt = q_ref.shape[0]
    ch = GLA_CHUNK
    per_tile = TM // ch
    n_tiles = nt // TM
    n_ctx_tiles = nctx // TM

    row = lax.broadcasted_iota(jnp.int32, (TM, TM), 0)
    col = lax.broadcasted_iota(jnp.int32, (TM, TM), 1)
    same_chunk = (row // ch) == (col // ch)
    chunk_ones = jnp.where(same_chunk, 1.0, 0.0).astype(BF16)
    tok_chunk = lax.broadcasted_iota(jnp.int32, (TM, 1), 0) // ch
    srow = lax.broadcasted_iota(jnp.int32, (2 * B_VDIM, 2 * B_KDIM), 0)
    scol = lax.broadcasted_iota(jnp.int32, (2 * B_VDIM, 2 * B_KDIM), 1)
    same_head = (srow < B_VDIM) == (scol < B_KDIM)
    lane = lax.broadcasted_iota(jnp.int32, (1, LANES), 1)

    dirs = (0, 1)
    earlier = [jnp.logical_and(same_chunk, (col <= row) if d == 0 else (col >= row)) for d in dirs]
    tri = [jnp.where(e, 1.0, 0.0).astype(BF16) for e in earlier]
    half = [lane < B_KDIM, lane >= B_KDIM]

    out_refs = (o_ref, rev_ref)

    def advance(work):
        n = range(len(work))
        dr = [d for d, _ in work]
        rows = [pl.ds(t * TM if isinstance(t, int) else pl.multiple_of(t * TM, TM), TM) for _, t in work]
        z = [_dot(g_ref[rows[i], :], gw_ref[dr[i]]) + gb_ref[dr[i]] for i in n]
        la = [(jnp.minimum(x, 0.0) - jnp.log(1.0 + jnp.exp(-jnp.abs(x)))) * (1.0 / GATE_TAU) for x in z]
        hi = [x.astype(BF16) for x in la]
        pieces = [jnp.concatenate([hi[i], (la[i] - hi[i].astype(F32)).astype(BF16)], axis=1) for i in n]
        cum = [_dot(tri[dr[i]], pieces[i]) for i in n]
        cum = [x[:, :LANES] + x[:, LANES:] for x in cum]
        tot = [_dot(chunk_ones, pieces[i]) for i in n]
        tot = [x[:, :LANES] + x[:, LANES:] for x in tot]
        qf = [q_ref[rows[i], :].astype(F32) for i in n]
        kf = [k_ref[rows[i], :].astype(F32) for i in n]
        v = [v_ref[rows[i], :] for i in n]
        vt = [vt_ref[:, rows[i]] for i in n]
        qe = [qf[i] * (B_KDIM ** -0.5) * jnp.exp(cum[i]) for i in n]
        ke = [(kf[i] * jnp.exp(-cum[i])).astype(BF16) for i in n]
        kd = [(kf[i] * jnp.exp(tot[i] - cum[i])).astype(BF16) for i in n]
        decay = [jnp.exp(x) for x in tot]
        q_h = [[jnp.where(half[h], qe[i], 0.0).astype(BF16) for h in range(2)] for i in n]
        att = [[jnp.where(earlier[dr[i]], _dot_nt(q_h[i][h], ke[i]), 0.0).astype(BF16) for h in range(2)]
               for i in n]
        intra = [[_dot(att[i][h], v[i][:, h * B_VDIM:(h + 1) * B_VDIM]) for h in range(2)] for i in n]
        kd_c = [[jnp.where(tok_chunk == c, kd[i], jnp.zeros_like(kd[i])) for c in range(per_tile)] for i in n]
        inc2 = [[_dot(vt[i], jnp.concatenate(kd_c[i][c:c + 2], axis=1)) for c in range(0, per_tile, 2)]
                for i in n]
        incs = [[jnp.where(same_head, inc2[i][c // 2][:, (c % 2) * LANES:(c % 2 + 1) * LANES], 0.0)
                 for c in range(per_tile)] for i in n]
        state = [st_ref[d] for d in dirs]
        inter = [[None] * per_tile for _ in n]
        for i in n:
            d = dr[i]
            for c in (range(per_tile) if d == 0 else reversed(range(per_tile))):
                inter[i][c] = _dot_nt(qe[i][c * ch:(c + 1) * ch].astype(BF16), state[d].astype(BF16))
                state[d] = state[d] * decay[i][c * ch:c * ch + 1, :] + incs[i][c]
        for d in dirs:
            st_ref[d] = state[d]
        for i in n:
            out_refs[dr[i]][rows[i], :] = jnp.concatenate(intra[i], axis=1) + jnp.concatenate(inter[i], axis=0)

    st_ref[...] = jnp.zeros(st_ref.shape, F32)
    for i in range(n_ctx_tiles):
        advance([(0, i), (1, n_ctx_tiles - 1 - i)])

    def scan(i, carry):
        t_fwd = n_ctx_tiles + 2 * i
        t_rev = n_tiles - 1 - 2 * i
        advance([(0, t_fwd), (1, t_rev), (0, t_fwd + 1), (1, t_rev - 1)])
        return carry

    lax.fori_loop(0, (n_tiles - n_ctx_tiles) // 2, scan, 0)
    gn = gn_ref[...]

    def combine(t, carry):
        rows = pl.ds(pl.multiple_of(t * TM, TM), TM)
        tot = o_ref[rows, :] + rev_ref[rows, :]
        for h in range(2):
            lo = h * B_VDIM
            o_ref[rows, lo:lo + B_VDIM] = _rms(tot[:, lo:lo + B_VDIM]) * gn
        return carry

    lax.fori_loop(0, n_tiles, combine, 0)


def _gla_call(bq, bk, bv, bvt, bg, gw, gb, gn, nctx):
    b, nt, _ = bq.shape
    assert nctx % TM == 0 and (nt - nctx) % (2 * TM) == 0
    return pl.pallas_call(
        functools.partial(_gla_body, nctx=nctx),
        grid=(b, B_HEADS // 2),
        in_specs=[pl.BlockSpec((None, nt, LANES), lambda bb, p: (bb, 0, p)),
                  pl.BlockSpec((None, nt, LANES), lambda bb, p: (bb, 0, p)),
                  pl.BlockSpec((None, nt, 2 * B_VDIM), lambda bb, p: (bb, 0, p)),
                  pl.BlockSpec((None, 2 * B_VDIM, nt), lambda bb, p: (bb, p, 0)),
                  pl.BlockSpec((None, nt, LANES), lambda bb, p: (bb, 0, 0)),
                  pl.BlockSpec((2, LANES, LANES), lambda bb, p: (0, 0, p)),
                  pl.BlockSpec((2, 1, LANES), lambda bb, p: (0, 0, p)),
                  pl.BlockSpec((1, B_VDIM), lambda bb, p: (0, 0))],
        out_specs=pl.BlockSpec((None, nt, 2 * B_VDIM), lambda bb, p: (bb, 0, p)),
        out_shape=jax.ShapeDtypeStruct((b, nt, B_HEADS * B_VDIM), F32),
        scratch_shapes=[pltpu.VMEM((nt, 2 * B_VDIM), F32), pltpu.VMEM((2, 2 * B_VDIM, 2 * B_KDIM), F32)],
        compiler_params=_cparams(2),
        name="gla",
    )(bq, bk, bv, bvt, bg, gw, gb, gn)


def _attn_c_body(sink_ref, qt_ref, k_ref, vt_ref, o_ref, *, nctx, span):
    j = pl.program_id(1)
    nlat = k_ref.shape[0] - nctx
    per_kv = C_HEADS // C_KV_HEADS
    log2e = math.log2(math.e)

    def attend(k_rows, vt_cols, valid):
        k = jnp.concatenate([k_ref[pl.ds(s0, w), :] for s0, w in k_rows], axis=0)
        n_keys = k.shape[0]
        ones = jnp.ones((8, n_keys), BF16)
        groups = range(C_KV_HEADS)

        def q_stack(g):
            blocks = []
            for i in range(per_kv):
                h = g * per_kv + i
                qh = qt_ref[h * HEAD:(h + 1) * HEAD, :]
                zero = jnp.zeros_like(qh)
                blocks.append(jnp.concatenate([qh, zero] if g == 0 else [zero, qh], axis=0))
            return jnp.concatenate(blocks, axis=1)

        def masked(s):
            w_last = valid.shape[0]
            s_last = jnp.where(jnp.concatenate([valid] * per_kv, axis=1), s[n_keys - w_last:], NEG_INF)
            return jnp.concatenate([s[:n_keys - w_last], s_last], axis=0)

        s = [_dot(k, q_stack(g)) for g in groups]
        if valid is not None:
            s = [masked(x) for x in s]
        sink = [jnp.concatenate([jnp.full((1, TM), sink_ref[g * per_kv + i] * log2e, F32)
                                 for i in range(per_kv)], axis=1) for g in groups]
        mx = [jnp.maximum(jnp.max(s[g], axis=0, keepdims=True), sink[g]) for g in groups]
        e = [jnp.exp2(s[g] - mx[g]).astype(BF16) for g in groups]
        vt = [jnp.concatenate([vt_ref[g * HEAD:(g + 1) * HEAD, pl.ds(s0, w)] for s0, w in vt_cols], axis=1)
              for g in groups]
        pv = [_dot(jnp.concatenate([vt[g], ones], axis=0), e[g]) for g in groups]
        outs = []
        for g in groups:
            o = pv[g][0:HEAD] / (pv[g][HEAD:HEAD + 1] + jnp.exp2(sink[g] - mx[g]))
            outs += [o[:, i * TM:(i + 1) * TM] for i in range(per_kv)]
        o_ref[...] = jnp.concatenate(outs, axis=0).T.astype(BF16)

    @pl.when(j == 0)
    def _():
        attend([(0, nctx)], [(0, nctx)], None)

    @pl.when(j > 0)
    def _():
        q0 = (j - 1) * TM
        k0 = jnp.clip(q0 - WINDOW, 0, nlat - span)
        start = pl.multiple_of(nctx + k0, LANES)
        kpos = k0 + lax.broadcasted_iota(jnp.int32, (span, TM), 0)
        qpos = q0 + lax.broadcasted_iota(jnp.int32, (span, TM), 1)
        valid = jnp.abs(qpos - kpos) <= WINDOW
        attend([(0, nctx), (start, span)], [(0, nctx), (start, span)], valid)


def _attn_c_call(cqt, ck, cvt, sink, nctx):
    b, nt, kv_width = ck.shape
    width = cqt.shape[1]
    span = TM + 2 * WINDOW
    assert nt - nctx >= span
    return pl.pallas_call(
        functools.partial(_attn_c_body, nctx=nctx, span=span),
        grid=(b, nt // TM),
        in_specs=[pl.BlockSpec(memory_space=pltpu.SMEM),
                  pl.BlockSpec((None, width, TM), lambda bb, j: (bb, 0, j)),
                  pl.BlockSpec((None, nt, kv_width), lambda bb, j: (bb, 0, 0)),
                  pl.BlockSpec((None, kv_width, nt), lambda bb, j: (bb, 0, 0))],
        out_specs=_tok_spec(width),
        out_shape=jax.ShapeDtypeStruct((b, nt, width), BF16),
        compiler_params=_cparams(2),
        name="attn_c",
    )(sink, cqt, ck, cvt)


def _merge_body(x_ref, mod_ref, g_ref, wg_ref, wa_ref, wb_ref, wc_ref, wo_ref,
                oa_ref, ob_ref, oc_ref, o_ref):
    x = x_ref[...]
    d = x.shape[1]
    mod = mod_ref[...]
    hb = _norm_mod(x, g_ref[...], mod[3:4], mod[4:5]).astype(BF16)
    r = _dot(hb, wg_ref[:, 3 * d:])
    ob = (ob_ref[...] * jax.nn.silu(r)).astype(BF16)
    y = jax.nn.sigmoid(_dot(hb, wg_ref[:, 0:d])) * _dot(oa_ref[...], wa_ref[...])
    y = y + jax.nn.sigmoid(_dot(hb, wg_ref[:, d:2 * d])) * _dot(ob, wb_ref[...])
    y = y + jax.nn.sigmoid(_dot(hb, wg_ref[:, 2 * d:3 * d])) * _dot(oc_ref[...], wc_ref[...])
    o_ref[...] = x + mod[5:6] * _dot(y.astype(BF16), wo_ref[...])


def _merge_call(t, mod, g, wg, wa, wb, wc, wo, oa, ob, oc, ctx_row, skip_tiles=0):
    b, nt, d = t.shape
    n_tiles = nt // TM - skip_tiles
    return pl.pallas_call(
        _merge_body,
        grid=(b, n_tiles),
        in_specs=[_tok_spec(d, skip_tiles), _mod_spec(d, ctx_row), _const_spec((1, d)), _const_spec(wg.shape),
                  _const_spec(wa.shape), _const_spec(wb.shape), _const_spec(wc.shape), _const_spec(wo.shape),
                  _tok_spec(oa.shape[2], skip_tiles), _tok_spec(ob.shape[2], skip_tiles),
                  _tok_spec(oc.shape[2], skip_tiles)],
        out_specs=_tok_spec(d),
        out_shape=jax.ShapeDtypeStruct((b, n_tiles * TM, d), F32),
        compiler_params=_cparams(2),
        name="merge",
    )(t, mod, g, wg, wa, wb, wc, wo, oa, ob, oc)


def _rope_tables(n_lat, nctx):
    rows = n_lat // GRID_W
    row = jnp.repeat(jnp.arange(rows, dtype=F32), GRID_W)
    col = jnp.tile(jnp.arange(GRID_W, dtype=F32), rows)
    n_freq = HEAD // 4
    freqs = jnp.power(ROPE_BASE, -jnp.arange(n_freq, dtype=F32) / n_freq)
    ar = row[:, None] * freqs
    ac = col[:, None] * freqs
    ang = jnp.concatenate([ar, ar, ac, ac], axis=-1)
    ang = jnp.concatenate([ang, ang], axis=-1)
    first = (jnp.arange(LANES) % 32) < 16
    cos, sin = jnp.cos(ang), jnp.sin(ang)
    sin_a = jnp.where(first, -sin, 0.0)
    sin_b = jnp.where(first, 0.0, sin)
    pad = lambda a, v: jnp.concatenate([jnp.full((nctx, LANES), v, F32), a], axis=0)
    return pad(cos, 1.0), pad(sin_a, 0.0), pad(sin_b, 0.0)


def _mixin_weight(w):
    aq, ak, av = w[:, 0:512], w[:, 512:1024], w[:, 1024:1536]
    bq, bk, bv = w[:, 1536:1792], w[:, 1792:2048], w[:, 2048:2560]
    bg = jnp.pad(w[:, 2560:2560 + 2 * GATE_RANK], ((0, 0), (0, LANES - 2 * GATE_RANK)))
    cq, ck, cv = w[:, 3104:3616], w[:, 3616:3744], w[:, 3744:3872]
    return jnp.concatenate([aq, ak, cq, ck, av, bq, bk, bv, bg, cv], axis=1).astype(BF16)


def kernel(x, c, ctx, c_ctx, w_ada, b_ada, norm_g, w_ffn1_in, w_ffn1_out, w_ffn2_in, w_ffn2_out,
           w_mix_in, diff_lambda, diff_subln, gla_gate_w, gla_gate_b, gla_norm, swa_sink,
           w_br_a, w_br_b, w_br_c, w_mix_out, final_g):
    bsz, n, d = x.shape
    nctx = ctx.shape[1]
    depth = w_ada.shape[0]
    ffn = w_ffn1_out.shape[1]
    assert nctx == TM and n % 512 == 0 and d % LANES == 0

    t = (ctx, x)
    ctx_row = bsz
    n_rows = -(-(bsz + 1) // 8) * 8
    cvec = jnp.concatenate([c, c_ctx[None, :], jnp.zeros((n_rows - bsz - 1, d), F32)], axis=0)
    mod_all = _ada_call(cvec, w_ada, b_ada).reshape(depth, n_rows, N_MOD, d)
    cos, sin_a, sin_b = _rope_tables(n, nctx)
    gate_cols = 2560 + 2 * GATE_RANK
    merge_cols = 3872

    for l in range(depth):
        lam_init = 0.8 - 0.6 * math.exp(-0.3 * l)
        mod = mod_all[l]
        ng = norm_g[l]
        t = _ffn_call(t, mod, ng[0:1], w_ffn1_in[l, :, :ffn].astype(BF16), w_ffn1_in[l, :, ffn:].astype(BF16),
                      w_ffn1_out[l].astype(BF16), 0, ctx_row)
        aqt, ak, cqt, ck, avt, bq, bk, bv, bg, cvt, bvt = _mixin_call(
            t, mod, ng[1:2], _mixin_weight(w_mix_in[l]), cos, sin_a, sin_b, ctx_row)
        lam_rows = jnp.concatenate([diff_lambda[l], jnp.full((4, HEAD), lam_init, F32)], axis=0)
        oa = _attn_a_call(aqt, ak, avt, lam_rows, diff_subln[l][None, :], nctx)
        gw = jnp.zeros((2, LANES, B_HEADS * B_KDIM), F32)
        for dd in range(2):
            gw = gw.at[dd, dd * GATE_RANK:(dd + 1) * GATE_RANK].set(gla_gate_w[l, dd])
        ob = _gla_call(bq, bk, bv, bvt, bg, gw.astype(BF16), gla_gate_b[l][:, None, :], gla_norm[l][None, :], nctx)
        oc = _attn_c_call(cqt, ck, cvt, swa_sink[l], nctx)
        wg = jnp.concatenate([w_mix_in[l, :, merge_cols:], w_mix_in[l, :, gate_cols:gate_cols + 512]],
                             axis=1).astype(BF16)
        last = l == depth - 1
        tail_row, skip = (None, nctx // TM) if last else (ctx_row, 0)
        t = _merge_call(t, mod, ng[1:2], wg, w_br_a[l].astype(BF16), w_br_b[l].astype(BF16),
                        w_br_c[l].astype(BF16), w_mix_out[l].astype(BF16), oa, ob, oc, tail_row, skip)
        t = _ffn_call(t, mod, ng[2:3], w_ffn2_in[l, :, :ffn].astype(BF16), w_ffn2_in[l, :, ffn:].astype(BF16),
                      w_ffn2_out[l].astype(BF16), 6, tail_row, final_g[None, :] if last else None)
    return t
```

```python
import functools
import math

import jax
import jax.numpy as jnp
from jax import lax
from jax.experimental import pallas as pl
from jax.experimental.pallas import tpu as pltpu

F32 = jnp.float32
BF16 = jnp.bfloat16

HEAD = 64
ROPE_BASE = 10000.0
GRID_W = 64
N_MOD = 9
A_HEADS = 4
A_VROWS = 128 + 16
B_HEADS = 4
B_KDIM = 64
B_VDIM = 128
GATE_RANK = 16
GATE_TAU = 16.0
GLA_CHUNK = 64
C_HEADS = 8
C_KV_HEADS = 2
WINDOW = 128
NEG_INF = -1e30
EPS = 1e-6

TM = 256
LANES = 128
VMEM_LIMIT = 56 * 1024 * 1024

_AQ, _AK, _CQ, _CK, _AV, _BQ, _BK, _BV, _BG, _CV, _MIX_COLS = (
    0, 512, 1024, 1536, 1664, 2176, 2432, 2688, 3200, 3328, 3456)


def _cparams(n_grid, vmem=VMEM_LIMIT):
    return pltpu.CompilerParams(dimension_semantics=("arbitrary",) * n_grid, vmem_limit_bytes=vmem)


def _const_spec(shape):
    nd = len(shape)
    return pl.BlockSpec(shape, lambda *_: (0,) * nd, pipeline_mode=pl.Buffered(1))


def _rms(x):
    return x * lax.rsqrt(jnp.mean(x * x, axis=-1, keepdims=True) + EPS)


def _norm_mod(x, g, shift, scale):
    return _rms(x) * g * (1.0 + scale) + shift


def _dot(a, b):
    return jnp.dot(a, b, preferred_element_type=F32)


def _ada_body(c_ref, w_ref, b_ref, o_ref):
    sc = jax.nn.silu(c_ref[...]).astype(BF16)
    o_ref[...] = _dot(sc, w_ref[...]) + b_ref[...]


def _ada_call(cvec, w_ada, b_ada):
    depth, d, n_out = w_ada.shape
    rows = cvec.shape[0]
    tn = n_out // 4
    return pl.pallas_call(
        _ada_body,
        grid=(depth, n_out // tn),
        in_specs=[pl.BlockSpec((rows, d), lambda l, n: (0, 0)),
                  pl.BlockSpec((None, d, tn), lambda l, n: (l, 0, n)),
                  pl.BlockSpec((None, 1, tn), lambda l, n: (l, 0, n))],
        out_specs=pl.BlockSpec((None, rows, tn), lambda l, n: (l, 0, n)),
        out_shape=jax.ShapeDtypeStruct((depth, rows, n_out), F32),
        compiler_params=_cparams(2),
        name="adaln",
    )(cvec, w_ada, b_ada.reshape(depth, 1, n_out))


def _tok_spec(width, skip=0):
    return pl.BlockSpec((None, TM, width), lambda b, j: (b, j + skip, 0))


def _mod_spec(d, ctx_row):
    if ctx_row is None:
        return pl.BlockSpec((None, N_MOD, d), lambda b, j: (b, 0, 0))
    return pl.BlockSpec((None, N_MOD, d), lambda b, j: (jnp.where(j == 0, ctx_row, b), 0, 0))


def _swiglu_step(x, mod, g, wu_ref, wv_ref, wd_ref, mod0):
    hb = _norm_mod(x, g, mod[mod0:mod0 + 1], mod[mod0 + 1:mod0 + 2]).astype(BF16)
    f = wu_ref.shape[1]
    f_chunk = f // 2
    acc = jnp.zeros(x.shape, F32)
    for lo in range(0, f, f_chunk):
        u = _dot(hb, wu_ref[:, lo:lo + f_chunk])
        v = _dot(hb, wv_ref[:, lo:lo + f_chunk])
        a = (jax.nn.silu(u) * v).astype(BF16)
        acc = acc + _dot(a, wd_ref[lo:lo + f_chunk, :])
    return x + 0.5 * mod[mod0 + 2:mod0 + 3] * acc


def _rope128(x, cos, sin_a, sin_b):
    return x * cos + pltpu.roll(x, LANES - 16, 1) * sin_a + pltpu.roll(x, 16, 1) * sin_b


def _head_body(*refs, split_input):
    refs = list(refs)
    if split_input:
        ctx_ref, lat_ref = refs[0:2]
        x = jnp.where(pl.program_id(1) == 0, ctx_ref[...], lat_ref[...])
        refs = refs[2:]
    else:
        x = refs[0][...]
        refs = refs[1:]
    (mod_ref, g0_ref, wu_ref, wv_ref, wd_ref, g_ref, w_ref, cos_ref, sa_ref, sb_ref, t_ref,
     aqt_ref, ak_ref, cqt_ref, ck_ref, avt_ref, bq_ref, bk_ref, bv_ref, bg_ref, cvt_ref, bvt_ref) = refs
    mod = mod_ref[...]
    x = _swiglu_step(x, mod, g0_ref[...], wu_ref, wv_ref, wd_ref, 0)
    t_ref[...] = x
    hb = _norm_mod(x, g_ref[...], mod[3:4], mod[4:5]).astype(BF16)
    cos, sin_a, sin_b = cos_ref[...], sa_ref[...], sb_ref[...]
    q_scale = HEAD ** -0.5 * math.log2(math.e)

    def proj(lo, hi):
        return _dot(hb, w_ref[:, lo:hi])

    def roped(p, i):
        return _rope128(p[:, i * LANES:(i + 1) * LANES], cos, sin_a, sin_b)

    p_aq, p_ak, p_cqk, p_av = proj(_AQ, _AK), proj(_AK, _CQ), proj(_CQ, _AV), proj(_AV, _BQ)
    for i in range(4):
        lo = i * LANES
        aqt_ref[lo:lo + LANES, :] = (roped(p_aq, i) * q_scale).T.astype(BF16)
        cqt_ref[lo:lo + LANES, :] = (roped(p_cqk, i) * q_scale).T.astype(BF16)
        ak_ref[:, lo:lo + LANES] = roped(p_ak, i).astype(BF16)
        avt_ref[i * A_VROWS:i * A_VROWS + LANES, :] = p_av[:, lo:lo + LANES].T.astype(BF16)
        avt_ref[i * A_VROWS + LANES:(i + 1) * A_VROWS, :] = jnp.ones((A_VROWS - LANES, TM), BF16)
    ck_ref[...] = roped(p_cqk, 4).astype(BF16)
    bq_ref[...] = proj(_BQ, _BK).astype(BF16)
    bk_ref[...] = proj(_BK, _BV).astype(BF16)
    bv = proj(_BV, _BG)
    bv_ref[...] = bv.astype(BF16)
    bvt_ref[...] = bv.T.astype(BF16)
    p_gc = proj(_BG, _MIX_COLS)
    bg_ref[...] = p_gc[:, 0:_CV - _BG].astype(BF16)
    cvt_ref[...] = p_gc[:, _CV - _BG:].T.astype(BF16)


def _head_call(t, mod, g0, wu, wv, wd, g1, w_mix, cos, sin_a, sin_b, ctx_row):
    split_input = isinstance(t, tuple)
    d, f = wu.shape
    if split_input:
        ctx, lat = t
        b, nt = lat.shape[0], ctx.shape[1] + lat.shape[1]
        acts = [ctx, lat]
        act_specs = [pl.BlockSpec((None, TM, d), lambda bb, j: (bb, 0, 0)),
                     pl.BlockSpec((None, TM, d), lambda bb, j: (bb, jnp.maximum(j - 1, 0), 0))]
    else:
        b, nt = t.shape[0], t.shape[1]
        acts, act_specs = [t], [_tok_spec(d)]
    tbl = pl.BlockSpec((TM, LANES), lambda bb, j: (j, 0))

    def tposed(rows):
        return pl.BlockSpec((None, rows, TM), lambda bb, j: (bb, 0, j))

    def sds(*shape):
        return jax.ShapeDtypeStruct(shape, BF16)

    return pl.pallas_call(
        functools.partial(_head_body, split_input=split_input),
        grid=(b, nt // TM),
        in_specs=act_specs + [_mod_spec(d, ctx_row), _const_spec((1, d)),
                              _const_spec((d, f)), _const_spec((d, f)), _const_spec((f, d)),
                              _const_spec((1, d)), _const_spec((d, _MIX_COLS)), tbl, tbl, tbl],
        out_specs=[_tok_spec(d),
                   tposed(512), _tok_spec(512), tposed(512), _tok_spec(128), tposed(A_HEADS * A_VROWS),
                   _tok_spec(256), _tok_spec(256), _tok_spec(512), _tok_spec(128), tposed(128), tposed(512)],
        out_shape=[jax.ShapeDtypeStruct((b, nt, d), F32),
                   sds(b, 512, nt), sds(b, nt, 512), sds(b, 512, nt), sds(b, nt, 128), sds(b, A_HEADS * A_VROWS, nt),
                   sds(b, nt, 256), sds(b, nt, 256), sds(b, nt, 512), sds(b, nt, 128), sds(b, 128, nt),
                   sds(b, 512, nt)],
        compiler_params=_cparams(2),
        name="head",
    )(*acts, mod, g0, wu, wv, wd, g1, w_mix, cos, sin_a, sin_b)


def _attn_a_body(lam_ref, qt_ref, k_ref, vt_ref, sub_ref, o_ref, s_ref, *, nctx, k_chunk):
    j = pl.program_id(1)
    nt = k_ref.shape[0]
    row = lax.broadcasted_iota(jnp.int32, (LANES, 1), 0)
    lp = lam_ref[...]
    lam_init = lp[4:5, 0:1]
    lam = (jnp.exp(jnp.sum(lp[0:1] * lp[1:2], axis=-1, keepdims=True))
           - jnp.exp(jnp.sum(lp[2:3] * lp[3:4], axis=-1, keepdims=True)) + lam_init)
    out_gain = sub_ref[...] * (1.0 - lam_init)
    units = [(h, m) for h in range(A_HEADS) for m in range(2)]

    def attend(n_keys):
        chunks = [(0, nctx)] + [(lo, k_chunk) for lo in range(nctx, n_keys, k_chunk)]
        state = {}

        def pass1(u, lo, w):
            h, m = units[u]
            if lo == 0:
                qt = qt_ref[h * LANES:(h + 1) * LANES, :]
                state["q", u] = jnp.where((row < HEAD) if m == 0 else (row >= HEAD), qt, jnp.zeros_like(qt))
            s = _dot(k_ref[lo:lo + w, h * LANES:(h + 1) * LANES], state["q", u])
            s_ref[u % 2, lo:lo + w, :] = s
            pm = jnp.max(s, axis=0, keepdims=True)
            state["mx", u] = pm if lo == 0 else jnp.maximum(state["mx", u], pm)

        def pass2(u, lo, w):
            h, m = units[u]
            e = jnp.exp2(s_ref[u % 2, lo:lo + w, :] - state["mx", u])
            pv = _dot(vt_ref[h * A_VROWS:(h + 1) * A_VROWS, lo:lo + w], e.astype(BF16))
            state["acc", u] = pv if lo == 0 else state["acc", u] + pv

        def finish(u):
            h, m = units[u]
            acc = state.pop(("acc", u))
            res = acc[0:LANES] / acc[LANES:LANES + 1]
            if m == 0:
                state["head", h] = res
            else:
                o = (state.pop(("head", h)) - lam * res).T
                o_ref[:, h * LANES:(h + 1) * LANES] = (_rms(o) * out_gain).astype(BF16)

        for lo, w in chunks:
            pass1(0, lo, w)
        for u in range(1, len(units)):
            for lo, w in chunks:
                pass1(u, lo, w)
                pass2(u - 1, lo, w)
            finish(u - 1)
        for lo, w in chunks:
            pass2(len(units) - 1, lo, w)
        finish(len(units) - 1)

    @pl.when(j == 0)
    def _():
        attend(nctx)

    @pl.when(j > 0)
    def _():
        attend(nt)


def _attn_a_call(aqt, ak, avt, lam_rows, subln, nctx):
    b, nt, width = ak.shape
    k_chunk = 512
    assert (nt - nctx) % k_chunk == 0
    return pl.pallas_call(
        functools.partial(_attn_a_body, nctx=nctx, k_chunk=k_chunk),
        grid=(b, nt // TM),
        in_specs=[pl.BlockSpec(lam_rows.shape, lambda bb, j: (0, 0)),
                  pl.BlockSpec((None, width, TM), lambda bb, j: (bb, 0, j)),
                  pl.BlockSpec((None, nt, width), lambda bb, j: (bb, 0, 0)),
                  pl.BlockSpec((None, avt.shape[1], nt), lambda bb, j: (bb, 0, 0)),
                  pl.BlockSpec((1, LANES), lambda bb, j: (0, 0))],
        out_specs=_tok_spec(width),
        out_shape=jax.ShapeDtypeStruct((b, nt, width), BF16),
        scratch_shapes=[pltpu.VMEM((2, nt, TM), F32)],
        compiler_params=_cparams(2),
        name="attn_a",
    )(lam_rows, aqt, ak, avt, subln)


def _dot_nt(a, b):
    return lax.dot_general(a, b, (((1,), (1,)), ((), ())), preferred_element_type=F32)


def _gla_body(q_ref, k_ref, v_ref, vt_ref, g_ref, gw_ref, gb_ref, gn_ref, o_ref, rev_ref, st_ref, *, nctx):
    nt = q_ref.shape[0]
    ch = GLA_CHUNK
    per_tile = TM // ch
    n_tiles = nt // TM
    n_ctx_tiles = nctx // TM

    row = lax.broadcasted_iota(jnp.int32, (TM, TM), 0)
    col = lax.broadcasted_iota(jnp.int32, (TM, TM), 1)
    same_chunk = (row // ch) == (col // ch)
    chunk_ones = jnp.where(same_chunk, 1.0, 0.0).astype(BF16)
    tok_chunk = lax.broadcasted_iota(jnp.int32, (TM, 1), 0) // ch
    srow = lax.broadcasted_iota(jnp.int32, (2 * B_VDIM, 2 * B_KDIM), 0)
    scol = lax.broadcasted_iota(jnp.int32, (2 * B_VDIM, 2 * B_KDIM), 1)
    same_head = (srow < B_VDIM) == (scol < B_KDIM)
    lane = lax.broadcasted_iota(jnp.int32, (1, LANES), 1)

    dirs = (0, 1)
    earlier = [jnp.logical_and(same_chunk, (col <= row) if d == 0 else (col >= row)) for d in dirs]
    tri = [jnp.where(e, 1.0, 0.0).astype(BF16) for e in earlier]
    half = [lane < B_KDIM, lane >= B_KDIM]

    out_refs = (o_ref, rev_ref)

    def advance(work):
        n = range(len(work))
        dr = [d for d, _ in work]
        rows = [pl.ds(t * TM if isinstance(t, int) else pl.multiple_of(t * TM, TM), TM) for _, t in work]
        z = [_dot(g_ref[rows[i], :], gw_ref[dr[i]]) + gb_ref[dr[i]] for i in n]
        la = [(jnp.minimum(x, 0.0) - jnp.log(1.0 + jnp.exp(-jnp.abs(x)))) * (1.0 / GATE_TAU) for x in z]
        hi = [x.astype(BF16) for x in la]
        pieces = [jnp.concatenate([hi[i], (la[i] - hi[i].astype(F32)).astype(BF16)], axis=1) for i in n]
        cum = [_dot(tri[dr[i]], pieces[i]) for i in n]
        cum = [x[:, :LANES] + x[:, LANES:] for x in cum]
        tot = [_dot(chunk_ones, pieces[i]) for i in n]
        tot = [x[:, :LANES] + x[:, LANES:] for x in tot]
        qf = [q_ref[rows[i], :].astype(F32) for i in n]
        kf = [k_ref[rows[i], :].astype(F32) for i in n]
        v = [v_ref[rows[i], :] for i in n]
        vt = [vt_ref[:, rows[i]] for i in n]
        qe = [qf[i] * (B_KDIM ** -0.5) * jnp.exp(cum[i]) for i in n]
        ke = [(kf[i] * jnp.exp(-cum[i])).astype(BF16) for i in n]
        kd = [(kf[i] * jnp.exp(tot[i] - cum[i])).astype(BF16) for i in n]
        decay = [jnp.exp(x) for x in tot]
        q_h = [[jnp.where(half[h], qe[i], 0.0).astype(BF16) for h in range(2)] for i in n]
        att = [[jnp.where(earlier[dr[i]], _dot_nt(q_h[i][h], ke[i]), 0.0).astype(BF16) for h in range(2)]
               for i in n]
        intra = [[_dot(att[i][h], v[i][:, h * B_VDIM:(h + 1) * B_VDIM]) for h in range(2)] for i in n]
        kd_c = [[jnp.where(tok_chunk == c, kd[i], jnp.zeros_like(kd[i])) for c in range(per_tile)] for i in n]
        inc2 = [[_dot(vt[i], jnp.concatenate(kd_c[i][c:c + 2], axis=1)) for c in range(0, per_tile, 2)]
                for i in n]
        incs = [[jnp.where(same_head, inc2[i][c // 2][:, (c % 2) * LANES:(c % 2 + 1) * LANES], 0.0)
                 for c in range(per_tile)] for i in n]
        state = [st_ref[d] for d in dirs]
        inter = [[None] * per_tile for _ in n]
        for i in n:
            d = dr[i]
            for c in (range(per_tile) if d == 0 else reversed(range(per_tile))):
                inter[i][c] = _dot_nt(qe[i][c * ch:(c + 1) * ch].astype(BF16), state[d].astype(BF16))
                state[d] = state[d] * decay[i][c * ch:c * ch + 1, :] + incs[i][c]
        for d in dirs:
            st_ref[d] = state[d]
        for i in n:
            out_refs[dr[i]][rows[i], :] = jnp.concatenate(intra[i], axis=1) + jnp.concatenate(inter[i], axis=0)

    st_ref[...] = jnp.zeros(st_ref.shape, F32)
    for i in range(n_ctx_tiles):
        advance([(0, i), (1, n_ctx_tiles - 1 - i)])

    def scan(i, carry):
        t_fwd = n_ctx_tiles + 2 * i
        t_rev = n_tiles - 1 - 2 * i
        advance([(0, t_fwd), (1, t_rev), (0, t_fwd + 1), (1, t_rev - 1)])
        return carry

    lax.fori_loop(0, (n_tiles - n_ctx_tiles) // 2, scan, 0)
    gn = gn_ref[...]

    def combine(t, carry):
        rows = pl.ds(pl.multiple_of(t * TM, TM), TM)
        tot = o_ref[rows, :] + rev_ref[rows, :]
        for h in range(2):
            lo = h * B_VDIM
            o_ref[rows, lo:lo + B_VDIM] = _rms(tot[:, lo:lo + B_VDIM]) * gn
        return carry

    lax.fori_loop(0, n_tiles, combine, 0)


def _gla_call(bq, bk, bv, bvt, bg, gw, gb, gn, nctx):
    b, nt, _ = bq.shape
    assert nctx % TM == 0 and (nt - nctx) % (2 * TM) == 0
    return pl.pallas_call(
        functools.partial(_gla_body, nctx=nctx),
        grid=(b, B_HEADS // 2),
        in_specs=[pl.BlockSpec((None, nt, LANES), lambda bb, p: (bb, 0, p)),
                  pl.BlockSpec((None, nt, LANES), lambda bb, p: (bb, 0, p)),
                  pl.BlockSpec((None, nt, 2 * B_VDIM), lambda bb, p: (bb, 0, p)),
                  pl.BlockSpec((None, 2 * B_VDIM, nt), lambda bb, p: (bb, p, 0)),
                  pl.BlockSpec((None, nt, LANES), lambda bb, p: (bb, 0, 0)),
                  pl.BlockSpec((2, LANES, LANES), lambda bb, p: (0, 0, p)),
                  pl.BlockSpec((2, 1, LANES), lambda bb, p: (0, 0, p)),
                  pl.BlockSpec((1, B_VDIM), lambda bb, p: (0, 0))],
        out_specs=pl.BlockSpec((None, nt, 2 * B_VDIM), lambda bb, p: (bb, 0, p)),
        out_shape=jax.ShapeDtypeStruct((b, nt, B_HEADS * B_VDIM), F32),
        scratch_shapes=[pltpu.VMEM((nt, 2 * B_VDIM), F32), pltpu.VMEM((2, 2 * B_VDIM, 2 * B_KDIM), F32)],
        compiler_params=_cparams(2),
        name="gla",
    )(bq, bk, bv, bvt, bg, gw, gb, gn)


def _attn_c_body(sink_ref, qt_ref, k_ref, vt_ref, o_ref, *, nctx, span):
    j = pl.program_id(1)
    nlat = k_ref.shape[0] - nctx
    per_kv = C_HEADS // C_KV_HEADS
    log2e = math.log2(math.e)

    def attend(k_rows, vt_cols, valid):
        k = jnp.concatenate([k_ref[pl.ds(s0, w), :] for s0, w in k_rows], axis=0)
        n_keys = k.shape[0]
        ones = jnp.ones((8, n_keys), BF16)
        groups = range(C_KV_HEADS)

        def q_stack(g):
            blocks = []
            for i in range(per_kv):
                h = g * per_kv + i
                qh = qt_ref[h * HEAD:(h + 1) * HEAD, :]
                zero = jnp.zeros_like(qh)
                blocks.append(jnp.concatenate([qh, zero] if g == 0 else [zero, qh], axis=0))
            return jnp.concatenate(blocks, axis=1)

        def masked(s):
            w_last = valid.shape[0]
            s_last = jnp.where(jnp.concatenate([valid] * per_kv, axis=1), s[n_keys - w_last:], NEG_INF)
            return jnp.concatenate([s[:n_keys - w_last], s_last], axis=0)

        s = [_dot(k, q_stack(g)) for g in groups]
        if valid is not None:
            s = [masked(x) for x in s]
        sink = [jnp.concatenate([jnp.full((1, TM), sink_ref[g * per_kv + i] * log2e, F32)
                                 for i in range(per_kv)], axis=1) for g in groups]
        mx = [jnp.maximum(jnp.max(s[g], axis=0, keepdims=True), sink[g]) for g in groups]
        e = [jnp.exp2(s[g] - mx[g]).astype(BF16) for g in groups]
        vt = [jnp.concatenate([vt_ref[g * HEAD:(g + 1) * HEAD, pl.ds(s0, w)] for s0, w in vt_cols], axis=1)
              for g in groups]
        pv = [_dot(jnp.concatenate([vt[g], ones], axis=0), e[g]) for g in groups]
        outs = []
        for g in groups:
            o = pv[g][0:HEAD] / (pv[g][HEAD:HEAD + 1] + jnp.exp2(sink[g] - mx[g]))
            outs += [o[:, i * TM:(i + 1) * TM] for i in range(per_kv)]
        o_ref[...] = jnp.concatenate(outs, axis=0).T.astype(BF16)

    @pl.when(j == 0)
    def _():
        attend([(0, nctx)], [(0, nctx)], None)

    @pl.when(j > 0)
    def _():
        q0 = (j - 1) * TM
        k0 = jnp.clip(q0 - WINDOW, 0, nlat - span)
        start = pl.multiple_of(nctx + k0, LANES)
        kpos = k0 + lax.broadcasted_iota(jnp.int32, (span, TM), 0)
        qpos = q0 + lax.broadcasted_iota(jnp.int32, (span, TM), 1)
        valid = jnp.abs(qpos - kpos) <= WINDOW
        attend([(0, nctx), (start, span)], [(0, nctx), (start, span)], valid)


def _attn_c_call(cqt, ck, cvt, sink, nctx):
    b, nt, kv_width = ck.shape
    width = cqt.shape[1]
    span = TM + 2 * WINDOW
    assert nt - nctx >= span
    return pl.pallas_call(
        functools.partial(_attn_c_body, nctx=nctx, span=span),
        grid=(b, nt // TM),
        in_specs=[pl.BlockSpec(memory_space=pltpu.SMEM),
                  pl.BlockSpec((None, width, TM), lambda bb, j: (bb, 0, j)),
                  pl.BlockSpec((None, nt, kv_width), lambda bb, j: (bb, 0, 0)),
                  pl.BlockSpec((None, kv_width, nt), lambda bb, j: (bb, 0, 0))],
        out_specs=_tok_spec(width),
        out_shape=jax.ShapeDtypeStruct((b, nt, width), BF16),
        compiler_params=_cparams(2),
        name="attn_c",
    )(sink, cqt, ck, cvt)


def _tail_body(x_ref, mod_ref, g1_ref, wg_ref, wa_ref, wb_ref, wc_ref, wo_ref, oa_ref, ob_ref, oc_ref,
               g2_ref, wu_ref, wv_ref, wd_ref, *rest, final_norm):
    o_ref = rest[-1]
    x = x_ref[...]
    d = x.shape[1]
    mod = mod_ref[...]
    hb = _norm_mod(x, g1_ref[...], mod[3:4], mod[4:5]).astype(BF16)
    r = _dot(hb, wg_ref[:, 3 * d:])
    ob = (ob_ref[...] * jax.nn.silu(r)).astype(BF16)
    y = jax.nn.sigmoid(_dot(hb, wg_ref[:, 0:d])) * _dot(oa_ref[...], wa_ref[...])
    y = y + jax.nn.sigmoid(_dot(hb, wg_ref[:, d:2 * d])) * _dot(ob, wb_ref[...])
    y = y + jax.nn.sigmoid(_dot(hb, wg_ref[:, 2 * d:3 * d])) * _dot(oc_ref[...], wc_ref[...])
    x = x + mod[5:6] * _dot(y.astype(BF16), wo_ref[...])
    x = _swiglu_step(x, mod, g2_ref[...], wu_ref, wv_ref, wd_ref, 6)
    if final_norm:
        x = _rms(x) * rest[0][...]
    o_ref[...] = x


def _tail_call(t, mod, g1, wg, wa, wb, wc, wo, oa, ob, oc, g2, wu, wv, wd, ctx_row, skip_tiles=0, final_g=None):
    b, nt, d = t.shape
    n_tiles = nt // TM - skip_tiles
    extra, extra_specs = ([], []) if final_g is None else ([final_g], [_const_spec((1, d))])
    return pl.pallas_call(
        functools.partial(_tail_body, final_norm=final_g is not None),
        grid=(b, n_tiles),
        in_specs=[_tok_spec(d, skip_tiles), _mod_spec(d, ctx_row), _const_spec((1, d)), _const_spec(wg.shape),
                  _const_spec(wa.shape), _const_spec(wb.shape), _const_spec(wc.shape), _const_spec(wo.shape),
                  _tok_spec(oa.shape[2], skip_tiles), _tok_spec(ob.shape[2], skip_tiles),
                  _tok_spec(oc.shape[2], skip_tiles),
                  _const_spec((1, d)), _const_spec(wu.shape), _const_spec(wv.shape), _const_spec(wd.shape)]
        + extra_specs,
        out_specs=_tok_spec(d),
        out_shape=jax.ShapeDtypeStruct((b, n_tiles * TM, d), F32),
        compiler_params=_cparams(2),
        name="tail",
    )(t, mod, g1, wg, wa, wb, wc, wo, oa, ob, oc, g2, wu, wv, wd, *extra)


def _rope_tables(n_lat, nctx):
    rows = n_lat // GRID_W
    row = jnp.repeat(jnp.arange(rows, dtype=F32), GRID_W)
    col = jnp.tile(jnp.arange(GRID_W, dtype=F32), rows)
    n_freq = HEAD // 4
    freqs = jnp.power(ROPE_BASE, -jnp.arange(n_freq, dtype=F32) / n_freq)
    ar = row[:, None] * freqs
    ac = col[:, None] * freqs
    ang = jnp.concatenate([ar, ar, ac, ac], axis=-1)
    ang = jnp.concatenate([ang, ang], axis=-1)
    first = (jnp.arange(LANES) % 32) < 16
    cos, sin = jnp.cos(ang), jnp.sin(ang)
    sin_a = jnp.where(first, -sin, 0.0)
    sin_b = jnp.where(first, 0.0, sin)
    pad = lambda a, v: jnp.concatenate([jnp.full((nctx, LANES), v, F32), a], axis=0)
    return pad(cos, 1.0), pad(sin_a, 0.0), pad(sin_b, 0.0)


def _mixin_weight(w):
    aq, ak, av = w[:, 0:512], w[:, 512:1024], w[:, 1024:1536]
    bq, bk, bv = w[:, 1536:1792], w[:, 1792:2048], w[:, 2048:2560]
    bg = jnp.pad(w[:, 2560:2560 + 2 * GATE_RANK], ((0, 0), (0, LANES - 2 * GATE_RANK)))
    cq, ck, cv = w[:, 3104:3616], w[:, 3616:3744], w[:, 3744:3872]
    return jnp.concatenate([aq, ak, cq, ck, av, bq, bk, bv, bg, cv], axis=1).astype(BF16)


def kernel(x, c, ctx, c_ctx, w_ada, b_ada, norm_g, w_ffn1_in, w_ffn1_out, w_ffn2_in, w_ffn2_out,
           w_mix_in, diff_lambda, diff_subln, gla_gate_w, gla_gate_b, gla_norm, swa_sink,
           w_br_a, w_br_b, w_br_c, w_mix_out, final_g):
    bsz, n, d = x.shape
    nctx = ctx.shape[1]
    depth = w_ada.shape[0]
    ffn = w_ffn1_out.shape[1]
    assert nctx == TM and n % 512 == 0 and d % LANES == 0

    t = (ctx, x)
    ctx_row = bsz
    n_rows = -(-(bsz + 1) // 8) * 8
    cvec = jnp.concatenate([c, c_ctx[None, :], jnp.zeros((n_rows - bsz - 1, d), F32)], axis=0)
    mod_all = _ada_call(cvec, w_ada.astype(BF16), b_ada).reshape(depth, n_rows, N_MOD, d)
    cos, sin_a, sin_b = _rope_tables(n, nctx)
    gate_cols = 2560 + 2 * GATE_RANK
    merge_cols = 3872

    for l in range(depth):
        lam_init = 0.8 - 0.6 * math.exp(-0.3 * l)
        mod = mod_all[l]
        ng = norm_g[l]
        t, aqt, ak, cqt, ck, avt, bq, bk, bv, bg, cvt, bvt = _head_call(
            t, mod, ng[0:1], w_ffn1_in[l, :, :ffn].astype(BF16), w_ffn1_in[l, :, ffn:].astype(BF16),
            w_ffn1_out[l].astype(BF16), ng[1:2], _mixin_weight(w_mix_in[l]), cos, sin_a, sin_b, ctx_row)
        lam_rows = jnp.concatenate([diff_lambda[l], jnp.full((4, HEAD), lam_init, F32)], axis=0)
        oa = _attn_a_call(aqt, ak, avt, lam_rows, diff_subln[l][None, :], nctx)
        gw = jnp.zeros((2, LANES, B_HEADS * B_KDIM), F32)
        for dd in range(2):
            gw = gw.at[dd, dd * GATE_RANK:(dd + 1) * GATE_RANK].set(gla_gate_w[l, dd])
        ob = _gla_call(bq, bk, bv, bvt, bg, gw.astype(BF16), gla_gate_b[l][:, None, :], gla_norm[l][None, :], nctx)
        oc = _attn_c_call(cqt, ck, cvt, swa_sink[l], nctx)
        wg = jnp.concatenate([w_mix_in[l, :, merge_cols:], w_mix_in[l, :, gate_cols:gate_cols + 512]],
                             axis=1).astype(BF16)
        last = l == depth - 1
        tail_row, skip = (None, nctx // TM) if last else (ctx_row, 0)
        t = _tail_call(t, mod, ng[1:2], wg, w_br_a[l].astype(BF16), w_br_b[l].astype(BF16),
                       w_br_c[l].astype(BF16), w_mix_out[l].astype(BF16), oa, ob, oc,
                       ng[2:3], w_ffn2_in[l, :, :ffn].astype(BF16), w_ffn2_in[l, :, ffn:].astype(BF16),
                       w_ffn2_out[l].astype(BF16), tail_row, skip, final_g[None, :] if last else None)
    return t
```

```python
import functools
import math

import jax
import jax.numpy as jnp
from jax import lax
from jax.experimental import pallas as pl
from jax.experimental.pallas import tpu as pltpu

F32 = jnp.float32
BF16 = jnp.bfloat16

HEAD = 64
ROPE_BASE = 10000.0
GRID_W = 64
N_MOD = 9
A_HEADS = 4
A_VROWS = 128 + 16
B_HEADS = 4
B_KDIM = 64
B_VDIM = 128
GATE_RANK = 16
GATE_TAU = 16.0
GLA_CHUNK = 64
C_HEADS = 8
C_KV_HEADS = 2
WINDOW = 128
NEG_INF = -1e30
EPS = 1e-6

TM = 256
LANES = 128
VMEM_LIMIT = 56 * 1024 * 1024

_AQ, _AK, _CQ, _CK, _AV, _BQ, _BK, _BV, _BG, _CV, _MIX_COLS = (
    0, 512, 1024, 1536, 1664, 2176, 2432, 2688, 3200, 3328, 3456)


def _cparams(n_grid, vmem=VMEM_LIMIT):
    return pltpu.CompilerParams(dimension_semantics=("arbitrary",) * n_grid, vmem_limit_bytes=vmem)


def _const_spec(shape):
    nd = len(shape)
    return pl.BlockSpec(shape, lambda *_: (0,) * nd, pipeline_mode=pl.Buffered(1))


def _rms(x):
    return x * lax.rsqrt(jnp.mean(x * x, axis=-1, keepdims=True) + EPS)


def _norm_mod(x, g, shift, scale):
    return _rms(x) * g * (1.0 + scale) + shift


def _dot(a, b):
    return jnp.dot(a, b, preferred_element_type=F32)


def _ada_body(c_ref, w_ref, b_ref, o_ref):
    sc = jax.nn.silu(c_ref[...]).astype(BF16)
    o_ref[...] = _dot(sc, w_ref[...]) + b_ref[...]


def _ada_call(cvec, w_ada, b_ada):
    depth, d, n_out = w_ada.shape
    rows = cvec.shape[0]
    tn = n_out // 4
    return pl.pallas_call(
        _ada_body,
        grid=(depth, n_out // tn),
        in_specs=[pl.BlockSpec((rows, d), lambda l, n: (0, 0)),
                  pl.BlockSpec((None, d, tn), lambda l, n: (l, 0, n)),
                  pl.BlockSpec((None, 1, tn), lambda l, n: (l, 0, n))],
        out_specs=pl.BlockSpec((None, rows, tn), lambda l, n: (l, 0, n)),
        out_shape=jax.ShapeDtypeStruct((depth, rows, n_out), F32),
        compiler_params=_cparams(2),
        name="adaln",
    )(cvec, w_ada, b_ada.reshape(depth, 1, n_out))


def _tok_spec(width, skip=0):
    return pl.BlockSpec((None, TM, width), lambda b, j: (b, j + skip, 0))


def _mod_spec(d, ctx_row):
    if ctx_row is None:
        return pl.BlockSpec((None, N_MOD, d), lambda b, j: (b, 0, 0))
    return pl.BlockSpec((None, N_MOD, d), lambda b, j: (jnp.where(j == 0, ctx_row, b), 0, 0))


def _swiglu_step(x, mod, g, wu_ref, wv_ref, wd_ref, mod0):
    hb = _norm_mod(x, g, mod[mod0:mod0 + 1], mod[mod0 + 1:mod0 + 2]).astype(BF16)
    f = wu_ref.shape[1]
    f_chunk = f // 2
    acc = jnp.zeros(x.shape, F32)
    for lo in range(0, f, f_chunk):
        u = _dot(hb, wu_ref[:, lo:lo + f_chunk])
        v = _dot(hb, wv_ref[:, lo:lo + f_chunk])
        a = (jax.nn.silu(u) * v).astype(BF16)
        acc = acc + _dot(a, wd_ref[lo:lo + f_chunk, :])
    return x + 0.5 * mod[mod0 + 2:mod0 + 3] * acc


def _rope128(x, cos, sin_a, sin_b):
    return x * cos + pltpu.roll(x, LANES - 16, 1) * sin_a + pltpu.roll(x, 16, 1) * sin_b


def _head_body(*refs, split_input):
    refs = list(refs)
    if split_input:
        ctx_ref, lat_ref = refs[0:2]
        x = jnp.where(pl.program_id(1) == 0, ctx_ref[...], lat_ref[...])
        refs = refs[2:]
    else:
        x = refs[0][...]
        refs = refs[1:]
    (mod_ref, g0_ref, wu_ref, wv_ref, wd_ref, g_ref, w_ref, cos_ref, sa_ref, sb_ref, t_ref,
     aqt_ref, ak_ref, cqt_ref, ck_ref, avt_ref, bq_ref, bk_ref, bv_ref, bg_ref, cvt_ref, bvt_ref) = refs
    mod = mod_ref[...]
    x = _swiglu_step(x, mod, g0_ref[...], wu_ref, wv_ref, wd_ref, 0)
    t_ref[...] = x
    hb = _norm_mod(x, g_ref[...], mod[3:4], mod[4:5]).astype(BF16)
    cos, sin_a, sin_b = cos_ref[...], sa_ref[...], sb_ref[...]
    q_scale = HEAD ** -0.5 * math.log2(math.e)

    def proj(lo, hi):
        return _dot(hb, w_ref[:, lo:hi])

    def roped(p, i):
        return _rope128(p[:, i * LANES:(i + 1) * LANES], cos, sin_a, sin_b)

    p_aq, p_ak, p_cqk, p_av = proj(_AQ, _AK), proj(_AK, _CQ), proj(_CQ, _AV), proj(_AV, _BQ)
    for i in range(4):
        lo = i * LANES
        aqt_ref[lo:lo + LANES, :] = (roped(p_aq, i) * q_scale).T.astype(BF16)
        cqt_ref[lo:lo + LANES, :] = (roped(p_cqk, i) * q_scale).T.astype(BF16)
        ak_ref[:, lo:lo + LANES] = roped(p_ak, i).astype(BF16)
        avt_ref[i * A_VROWS:i * A_VROWS + LANES, :] = p_av[:, lo:lo + LANES].T.astype(BF16)
        avt_ref[i * A_VROWS + LANES:(i + 1) * A_VROWS, :] = jnp.ones((A_VROWS - LANES, TM), BF16)
    ck_ref[...] = roped(p_cqk, 4).astype(BF16)
    bq_ref[...] = proj(_BQ, _BK).astype(BF16)
    bk_ref[...] = proj(_BK, _BV).astype(BF16)
    bv = proj(_BV, _BG)
    bv_ref[...] = bv.astype(BF16)
    bvt_ref[...] = bv.T.astype(BF16)
    p_gc = proj(_BG, _MIX_COLS)
    bg_ref[...] = p_gc[:, 0:_CV - _BG].astype(BF16)
    cvt_ref[...] = p_gc[:, _CV - _BG:].T.astype(BF16)


def _head_call(t, mod, g0, wu, wv, wd, g1, w_mix, cos, sin_a, sin_b, ctx_row):
    split_input = isinstance(t, tuple)
    d, f = wu.shape
    if split_input:
        ctx, lat = t
        b, nt = lat.shape[0], ctx.shape[1] + lat.shape[1]
        acts = [ctx, lat]
        act_specs = [pl.BlockSpec((None, TM, d), lambda bb, j: (bb, 0, 0)),
                     pl.BlockSpec((None, TM, d), lambda bb, j: (bb, jnp.maximum(j - 1, 0), 0))]
    else:
        b, nt = t.shape[0], t.shape[1]
        acts, act_specs = [t], [_tok_spec(d)]
    tbl = pl.BlockSpec((TM, LANES), lambda bb, j: (j, 0))

    def tposed(rows):
        return pl.BlockSpec((None, rows, TM), lambda bb, j: (bb, 0, j))

    def sds(*shape):
        return jax.ShapeDtypeStruct(shape, BF16)

    return pl.pallas_call(
        functools.partial(_head_body, split_input=split_input),
        grid=(b, nt // TM),
        in_specs=act_specs + [_mod_spec(d, ctx_row), _const_spec((1, d)),
                              _const_spec((d, f)), _const_spec((d, f)), _const_spec((f, d)),
                              _const_spec((1, d)), _const_spec((d, _MIX_COLS)), tbl, tbl, tbl],
        out_specs=[_tok_spec(d),
                   tposed(512), _tok_spec(512), tposed(512), _tok_spec(128), tposed(A_HEADS * A_VROWS),
                   _tok_spec(256), _tok_spec(256), _tok_spec(512), _tok_spec(128), tposed(128), tposed(512)],
        out_shape=[jax.ShapeDtypeStruct((b, nt, d), F32),
                   sds(b, 512, nt), sds(b, nt, 512), sds(b, 512, nt), sds(b, nt, 128), sds(b, A_HEADS * A_VROWS, nt),
                   sds(b, nt, 256), sds(b, nt, 256), sds(b, nt, 512), sds(b, nt, 128), sds(b, 128, nt),
                   sds(b, 512, nt)],
        compiler_params=_cparams(2),
        name="head",
    )(*acts, mod, g0, wu, wv, wd, g1, w_mix, cos, sin_a, sin_b)


def _attn_a_body(lam_ref, qt_ref, qn_ref, k_ref, vt_ref, sub_ref, o_ref, s_ref, mx_ref, *, nctx, k_chunk):
    j = pl.program_id(1)
    last = pl.num_programs(1) - 1
    nt = k_ref.shape[0]
    row = lax.broadcasted_iota(jnp.int32, (LANES, 1), 0)
    lp = lam_ref[...]
    lam_init = lp[4:5, 0:1]
    lam = (jnp.exp(jnp.sum(lp[0:1] * lp[1:2], axis=-1, keepdims=True))
           - jnp.exp(jnp.sum(lp[2:3] * lp[3:4], axis=-1, keepdims=True)) + lam_init)
    out_gain = sub_ref[...] * (1.0 - lam_init)
    units = [(h, m) for h in range(A_HEADS) for m in range(2)]
    n_units = len(units)
    all_chunks = [(0, nctx)] + [(lo, k_chunk) for lo in range(nctx, nt, k_chunk)]

    def attend(n_keys, first_ready, prefetch):
        chunks = [c for c in all_chunks if c[0] < n_keys]
        state = {}

        def pass1(u, lo, w):
            nxt = u == n_units
            h, m = units[u % n_units]
            if lo == 0:
                qt = (qn_ref if nxt else qt_ref)[h * LANES:(h + 1) * LANES, :]
                state["q", u] = jnp.where((row < HEAD) if m == 0 else (row >= HEAD), qt, jnp.zeros_like(qt))
            s = _dot(k_ref[lo:lo + w, h * LANES:(h + 1) * LANES], state["q", u])
            s_ref[u % 2, lo:lo + w, :] = s
            pm = jnp.max(s, axis=0, keepdims=True)
            state["mx", u] = pm if lo == 0 else jnp.maximum(state["mx", u], pm)

        def pass2(u, lo, w):
            h, m = units[u]
            e = jnp.exp2(s_ref[u % 2, lo:lo + w, :] - state["mx", u])
            pv = _dot(vt_ref[h * A_VROWS:(h + 1) * A_VROWS, lo:lo + w], e.astype(BF16))
            state["acc", u] = pv if lo == 0 else state["acc", u] + pv

        def finish(u):
            h, m = units[u]
            acc = state.pop(("acc", u))
            res = acc[0:LANES] / acc[LANES:LANES + 1]
            if m == 0:
                state["head", h] = res
            else:
                o = (state.pop(("head", h)) - lam * res).T
                o_ref[:, h * LANES:(h + 1) * LANES] = (_rms(o) * out_gain).astype(BF16)

        if first_ready:
            state["mx", 0] = mx_ref[...]
        else:
            for lo, w in chunks:
                pass1(0, lo, w)
        for u in range(1, n_units):
            for lo, w in chunks:
                pass1(u, lo, w)
                pass2(u - 1, lo, w)
            finish(u - 1)
        if prefetch:
            for lo, w in all_chunks:
                pass1(n_units, lo, w)
                if (lo, w) in chunks:
                    pass2(n_units - 1, lo, w)
            mx_ref[...] = state["mx", n_units]
        else:
            for lo, w in chunks:
                pass2(n_units - 1, lo, w)
        finish(n_units - 1)

    @pl.when(j == 0)
    def _():
        attend(nctx, False, True)

    @pl.when(jnp.logical_and(j > 0, j < last))
    def _():
        attend(nt, True, True)

    @pl.when(j == last)
    def _():
        attend(nt, True, False)


def _attn_a_call(aqt, ak, avt, lam_rows, subln, nctx):
    b, nt, width = ak.shape
    k_chunk = 512
    n_tiles = nt // TM
    assert (nt - nctx) % k_chunk == 0 and n_tiles >= 2
    return pl.pallas_call(
        functools.partial(_attn_a_body, nctx=nctx, k_chunk=k_chunk),
        grid=(b, n_tiles),
        in_specs=[pl.BlockSpec(lam_rows.shape, lambda bb, j: (0, 0)),
                  pl.BlockSpec((None, width, TM), lambda bb, j: (bb, 0, j)),
                  pl.BlockSpec((None, width, TM), lambda bb, j: (bb, 0, jnp.minimum(j + 1, n_tiles - 1))),
                  pl.BlockSpec((None, nt, width), lambda bb, j: (bb, 0, 0)),
                  pl.BlockSpec((None, avt.shape[1], nt), lambda bb, j: (bb, 0, 0)),
                  pl.BlockSpec((1, LANES), lambda bb, j: (0, 0))],
        out_specs=_tok_spec(width),
        out_shape=jax.ShapeDtypeStruct((b, nt, width), BF16),
        scratch_shapes=[pltpu.VMEM((2, nt, TM), F32), pltpu.VMEM((1, TM), F32)],
        compiler_params=_cparams(2),
        name="attn_a",
    )(lam_rows, aqt, aqt, ak, avt, subln)


def _dot_nt(a, b):
    return lax.dot_general(a, b, (((1,), (1,)), ((), ())), preferred_element_type=F32)


def _gla_body(q_ref, k_ref, v_ref, vt_ref, g_ref, gw_ref, gb_ref, gn_ref, o_ref, rev_ref, st_ref, *, nctx):
    nt = q_ref.shape[0]
    ch = GLA_CHUNK
    per_tile = TM // ch
    n_tiles = nt // TM
    n_ctx_tiles = nctx // TM

    row = lax.broadcasted_iota(jnp.int32, (TM, TM), 0)
    col = lax.broadcasted_iota(jnp.int32, (TM, TM), 1)
    same_chunk = (row // ch) == (col // ch)
    chunk_ones = jnp.where(same_chunk, 1.0, 0.0).astype(BF16)
    tok_chunk = lax.broadcasted_iota(jnp.int32, (TM, 1), 0) // ch
    srow = lax.broadcasted_iota(jnp.int32, (2 * B_VDIM, 2 * B_KDIM), 0)
    scol = lax.broadcasted_iota(jnp.int32, (2 * B_VDIM, 2 * B_KDIM), 1)
    same_head = (srow < B_VDIM) == (scol < B_KDIM)
    lane = lax.broadcasted_iota(jnp.int32, (1, LANES), 1)

    dirs = (0, 1)
    earlier = [jnp.logical_and(same_chunk, (col <= row) if d == 0 else (col >= row)) for d in dirs]
    tri = [jnp.where(e, 1.0, 0.0).astype(BF16) for e in earlier]
    half = [lane < B_KDIM, lane >= B_KDIM]

    out_refs = (o_ref, rev_ref)

    def advance(work):
        n = range(len(work))
        dr = [d for d, _ in work]
        rows = [pl.ds(t * TM if isinstance(t, int) else pl.multiple_of(t * TM, TM), TM) for _, t in work]
        z = [_dot(g_ref[rows[i], :], gw_ref[dr[i]]) + gb_ref[dr[i]] for i in n]
        la = [(jnp.minimum(x, 0.0) - jnp.log(1.0 + jnp.exp(-jnp.abs(x)))) * (1.0 / GATE_TAU) for x in z]
        hi = [x.astype(BF16) for x in la]
        pieces = [jnp.concatenate([hi[i], (la[i] - hi[i].astype(F32)).astype(BF16)], axis=1) for i in n]
        cum = [_dot(tri[dr[i]], pieces[i]) for i in n]
        cum = [x[:, :LANES] + x[:, LANES:] for x in cum]
        tot = [_dot(chunk_ones, pieces[i]) for i in n]
        tot = [x[:, :LANES] + x[:, LANES:] for x in tot]
        qf = [q_ref[rows[i], :].astype(F32) for i in n]
        kf = [k_ref[rows[i], :].astype(F32) for i in n]
        v = [v_ref[rows[i], :] for i in n]
        vt = [vt_ref[:, rows[i]] for i in n]
        qe = [qf[i] * (B_KDIM ** -0.5) * jnp.exp(cum[i]) for i in n]
        ke = [(kf[i] * jnp.exp(-cum[i])).astype(BF16) for i in n]
        kd = [(kf[i] * jnp.exp(tot[i] - cum[i])).astype(BF16) for i in n]
        decay = [jnp.exp(x) for x in tot]
        q_h = [[jnp.where(half[h], qe[i], 0.0).astype(BF16) for h in range(2)] for i in n]
        att = [[jnp.where(earlier[dr[i]], _dot_nt(q_h[i][h], ke[i]), 0.0).astype(BF16) for h in range(2)]
               for i in n]
        intra = [[_dot(att[i][h], v[i][:, h * B_VDIM:(h + 1) * B_VDIM]) for h in range(2)] for i in n]
        kd_c = [[jnp.where(tok_chunk == c, kd[i], jnp.zeros_like(kd[i])) for c in range(per_tile)] for i in n]
        inc2 = [[_dot(vt[i], jnp.concatenate(kd_c[i][c:c + 2], axis=1)) for c in range(0, per_tile, 2)]
                for i in n]
        incs = [[jnp.where(same_head, inc2[i][c // 2][:, (c % 2) * LANES:(c % 2 + 1) * LANES], 0.0)
                 for c in range(per_tile)] for i in n]
        state = [st_ref[d] for d in dirs]
        inter = [[None] * per_tile for _ in n]
        for i in n:
            d = dr[i]
            for c in (range(per_tile) if d == 0 else reversed(range(per_tile))):
                inter[i][c] = _dot_nt(qe[i][c * ch:(c + 1) * ch].astype(BF16), state[d].astype(BF16))
                state[d] = state[d] * decay[i][c * ch:c * ch + 1, :] + incs[i][c]
        for d in dirs:
            st_ref[d] = state[d]
        for i in n:
            out_refs[dr[i]][rows[i], :] = jnp.concatenate(intra[i], axis=1) + jnp.concatenate(inter[i], axis=0)

    st_ref[...] = jnp.zeros(st_ref.shape, F32)
    for i in range(n_ctx_tiles):
        advance([(0, i), (1, n_ctx_tiles - 1 - i)])

    def scan(i, carry):
        t_fwd = n_ctx_tiles + 2 * i
        t_rev = n_tiles - 1 - 2 * i
        advance([(0, t_fwd), (1, t_rev), (0, t_fwd + 1), (1, t_rev - 1)])
        return carry

    lax.fori_loop(0, (n_tiles - n_ctx_tiles) // 2, scan, 0)
    gn = gn_ref[...]

    def combine(t, carry):
        rows = pl.ds(pl.multiple_of(t * TM, TM), TM)
        tot = o_ref[rows, :] + rev_ref[rows, :]
        for h in range(2):
            lo = h * B_VDIM
            o_ref[rows, lo:lo + B_VDIM] = _rms(tot[:, lo:lo + B_VDIM]) * gn
        return carry

    lax.fori_loop(0, n_tiles, combine, 0)


def _gla_call(bq, bk, bv, bvt, bg, gw, gb, gn, nctx):
    b, nt, _ = bq.shape
    assert nctx % TM == 0 and (nt - nctx) % (2 * TM) == 0
    return pl.pallas_call(
        functools.partial(_gla_body, nctx=nctx),
        grid=(b, B_HEADS // 2),
        in_specs=[pl.BlockSpec((None, nt, LANES), lambda bb, p: (bb, 0, p)),
                  pl.BlockSpec((None, nt, LANES), lambda bb, p: (bb, 0, p)),
                  pl.BlockSpec((None, nt, 2 * B_VDIM), lambda bb, p: (bb, 0, p)),
                  pl.BlockSpec((None, 2 * B_VDIM, nt), lambda bb, p: (bb, p, 0)),
                  pl.BlockSpec((None, nt, LANES), lambda bb, p: (bb, 0, 0)),
                  pl.BlockSpec((2, LANES, LANES), lambda bb, p: (0, 0, p)),
                  pl.BlockSpec((2, 1, LANES), lambda bb, p: (0, 0, p)),
                  pl.BlockSpec((1, B_VDIM), lambda bb, p: (0, 0))],
        out_specs=pl.BlockSpec((None, nt, 2 * B_VDIM), lambda bb, p: (bb, 0, p)),
        out_shape=jax.ShapeDtypeStruct((b, nt, B_HEADS * B_VDIM), F32),
        scratch_shapes=[pltpu.VMEM((nt, 2 * B_VDIM), F32), pltpu.VMEM((2, 2 * B_VDIM, 2 * B_KDIM), F32)],
        compiler_params=_cparams(2),
        name="gla",
    )(bq, bk, bv, bvt, bg, gw, gb, gn)


def _attn_c_body(sink_ref, qt_ref, k_ref, vt_ref, o_ref, *, nctx, span):
    j = pl.program_id(1)
    nlat = k_ref.shape[0] - nctx
    per_kv = C_HEADS // C_KV_HEADS
    log2e = math.log2(math.e)

    def attend(k_rows, vt_cols, valid):
        k = jnp.concatenate([k_ref[pl.ds(s0, w), :] for s0, w in k_rows], axis=0)
        n_keys = k.shape[0]
        ones = jnp.ones((8, n_keys), BF16)
        groups = range(C_KV_HEADS)

        def q_stack(g):
            blocks = []
            for i in range(per_kv):
                h = g * per_kv + i
                qh = qt_ref[h * HEAD:(h + 1) * HEAD, :]
                zero = jnp.zeros_like(qh)
                blocks.append(jnp.concatenate([qh, zero] if g == 0 else [zero, qh], axis=0))
            return jnp.concatenate(blocks, axis=1)

        def masked(s):
            w_last = valid.shape[0]
            s_last = jnp.where(jnp.concatenate([valid] * per_kv, axis=1), s[n_keys - w_last:], NEG_INF)
            return jnp.concatenate([s[:n_keys - w_last], s_last], axis=0)

        s = [_dot(k, q_stack(g)) for g in groups]
        if valid is not None:
            s = [masked(x) for x in s]
        sink = [jnp.concatenate([jnp.full((1, TM), sink_ref[g * per_kv + i] * log2e, F32)
                                 for i in range(per_kv)], axis=1) for g in groups]
        mx = [jnp.maximum(jnp.max(s[g], axis=0, keepdims=True), sink[g]) for g in groups]
        e = [jnp.exp2(s[g] - mx[g]).astype(BF16) for g in groups]
        vt = [jnp.concatenate([vt_ref[g * HEAD:(g + 1) * HEAD, pl.ds(s0, w)] for s0, w in vt_cols], axis=1)
              for g in groups]
        pv = [_dot(jnp.concatenate([vt[g], ones], axis=0), e[g]) for g in groups]
        outs = []
        for g in groups:
            o = pv[g][0:HEAD] / (pv[g][HEAD:HEAD + 1] + jnp.exp2(sink[g] - mx[g]))
            outs += [o[:, i * TM:(i + 1) * TM] for i in range(per_kv)]
        o_ref[...] = jnp.concatenate(outs, axis=0).T.astype(BF16)

    @pl.when(j == 0)
    def _():
        attend([(0, nctx)], [(0, nctx)], None)

    @pl.when(j > 0)
    def _():
        q0 = (j - 1) * TM
        k0 = jnp.clip(q0 - WINDOW, 0, nlat - span)
        start = pl.multiple_of(nctx + k0, LANES)
        kpos = k0 + lax.broadcasted_iota(jnp.int32, (span, TM), 0)
        qpos = q0 + lax.broadcasted_iota(jnp.int32, (span, TM), 1)
        valid = jnp.abs(qpos - kpos) <= WINDOW
        attend([(0, nctx), (start, span)], [(0, nctx), (start, span)], valid)


def _attn_c_call(cqt, ck, cvt, sink, nctx):
    b, nt, kv_width = ck.shape
    width = cqt.shape[1]
    span = TM + 2 * WINDOW
    assert nt - nctx >= span
    return pl.pallas_call(
        functools.partial(_attn_c_body, nctx=nctx, span=span),
        grid=(b, nt // TM),
        in_specs=[pl.BlockSpec(memory_space=pltpu.SMEM),
                  pl.BlockSpec((None, width, TM), lambda bb, j: (bb, 0, j)),
                  pl.BlockSpec((None, nt, kv_width), lambda bb, j: (bb, 0, 0)),
                  pl.BlockSpec((None, kv_width, nt), lambda bb, j: (bb, 0, 0))],
        out_specs=_tok_spec(width),
        out_shape=jax.ShapeDtypeStruct((b, nt, width), BF16),
        compiler_params=_cparams(2),
        name="attn_c",
    )(sink, cqt, ck, cvt)


def _tail_body(x_ref, mod_ref, g1_ref, wg_ref, wa_ref, wb_ref, wc_ref, wo_ref, oa_ref, ob_ref, oc_ref,
               g2_ref, wu_ref, wv_ref, wd_ref, *rest, final_norm):
    o_ref = rest[-1]
    x = x_ref[...]
    d = x.shape[1]
    mod = mod_ref[...]
    hb = _norm_mod(x, g1_ref[...], mod[3:4], mod[4:5]).astype(BF16)
    r = _dot(hb, wg_ref[:, 3 * d:])
    ob = (ob_ref[...] * jax.nn.silu(r)).astype(BF16)
    y = jax.nn.sigmoid(_dot(hb, wg_ref[:, 0:d])) * _dot(oa_ref[...], wa_ref[...])
    y = y + jax.nn.sigmoid(_dot(hb, wg_ref[:, d:2 * d])) * _dot(ob, wb_ref[...])
    y = y + jax.nn.sigmoid(_dot(hb, wg_ref[:, 2 * d:3 * d])) * _dot(oc_ref[...], wc_ref[...])
    x = x + mod[5:6] * _dot(y.astype(BF16), wo_ref[...])
    x = _swiglu_step(x, mod, g2_ref[...], wu_ref, wv_ref, wd_ref, 6)
    if final_norm:
        x = _rms(x) * rest[0][...]
    o_ref[...] = x


def _tail_call(t, mod, g1, wg, wa, wb, wc, wo, oa, ob, oc, g2, wu, wv, wd, ctx_row, skip_tiles=0, final_g=None):
    b, nt, d = t.shape
    n_tiles = nt // TM - skip_tiles
    extra, extra_specs = ([], []) if final_g is None else ([final_g], [_const_spec((1, d))])
    return pl.pallas_call(
        functools.partial(_tail_body, final_norm=final_g is not None),
        grid=(b, n_tiles),
        in_specs=[_tok_spec(d, skip_tiles), _mod_spec(d, ctx_row), _const_spec((1, d)), _const_spec(wg.shape),
                  _const_spec(wa.shape), _const_spec(wb.shape), _const_spec(wc.shape), _const_spec(wo.shape),
                  _tok_spec(oa.shape[2], skip_tiles), _tok_spec(ob.shape[2], skip_tiles),
                  _tok_spec(oc.shape[2], skip_tiles),
                  _const_spec((1, d)), _const_spec(wu.shape), _const_spec(wv.shape), _const_spec(wd.shape)]
        + extra_specs,
        out_specs=_tok_spec(d),
        out_shape=jax.ShapeDtypeStruct((b, n_tiles * TM, d), F32),
        compiler_params=_cparams(2),
        name="tail",
    )(t, mod, g1, wg, wa, wb, wc, wo, oa, ob, oc, g2, wu, wv, wd, *extra)


def _rope_tables(n_lat, nctx):
    rows = n_lat // GRID_W
    row = jnp.repeat(jnp.arange(rows, dtype=F32), GRID_W)
    col = jnp.tile(jnp.arange(GRID_W, dtype=F32), rows)
    n_freq = HEAD // 4
    freqs = jnp.power(ROPE_BASE, -jnp.arange(n_freq, dtype=F32) / n_freq)
    ar = row[:, None] * freqs
    ac = col[:, None] * freqs
    ang = jnp.concatenate([ar, ar, ac, ac], axis=-1)
    ang = jnp.concatenate([ang, ang], axis=-1)
    first = (jnp.arange(LANES) % 32) < 16
    cos, sin = jnp.cos(ang), jnp.sin(ang)
    sin_a = jnp.where(first, -sin, 0.0)
    sin_b = jnp.where(first, 0.0, sin)
    pad = lambda a, v: jnp.concatenate([jnp.full((nctx, LANES), v, F32), a], axis=0)
    return pad(cos, 1.0), pad(sin_a, 0.0), pad(sin_b, 0.0)


def _mixin_weight(w):
    aq, ak, av = w[:, 0:512], w[:, 512:1024], w[:, 1024:1536]
    bq, bk, bv = w[:, 1536:1792], w[:, 1792:2048], w[:, 2048:2560]
    bg = jnp.pad(w[:, 2560:2560 + 2 * GATE_RANK], ((0, 0), (0, LANES - 2 * GATE_RANK)))
    cq, ck, cv = w[:, 3104:3616], w[:, 3616:3744], w[:, 3744:3872]
    return jnp.concatenate([aq, ak, cq, ck, av, bq, bk, bv, bg, cv], axis=1).astype(BF16)


def kernel(x, c, ctx, c_ctx, w_ada, b_ada, norm_g, w_ffn1_in, w_ffn1_out, w_ffn2_in, w_ffn2_out,
           w_mix_in, diff_lambda, diff_subln, gla_gate_w, gla_gate_b, gla_norm, swa_sink,
           w_br_a, w_br_b, w_br_c, w_mix_out, final_g):
    bsz, n, d = x.shape
    nctx = ctx.shape[1]
    depth = w_ada.shape[0]
    ffn = w_ffn1_out.shape[1]
    assert nctx == TM and n % 512 == 0 and d % LANES == 0

    t = (ctx, x)
    ctx_row = bsz
    n_rows = -(-(bsz + 1) // 8) * 8
    cvec = jnp.concatenate([c, c_ctx[None, :], jnp.zeros((n_rows - bsz - 1, d), F32)], axis=0)
    mod_all = _ada_call(cvec, w_ada.astype(BF16), b_ada).reshape(depth, n_rows, N_MOD, d)
    cos, sin_a, sin_b = _rope_tables(n, nctx)
    gate_cols = 2560 + 2 * GATE_RANK
    merge_cols = 3872

    for l in range(depth):
        lam_init = 0.8 - 0.6 * math.exp(-0.3 * l)
        mod = mod_all[l]
        ng = norm_g[l]
        t, aqt, ak, cqt, ck, avt, bq, bk, bv, bg, cvt, bvt = _head_call(
            t, mod, ng[0:1], w_ffn1_in[l, :, :ffn].astype(BF16), w_ffn1_in[l, :, ffn:].astype(BF16),
            w_ffn1_out[l].astype(BF16), ng[1:2], _mixin_weight(w_mix_in[l]), cos, sin_a, sin_b, ctx_row)
        lam_rows = jnp.concatenate([diff_lambda[l], jnp.full((4, HEAD), lam_init, F32)], axis=0)
        oa = _attn_a_call(aqt, ak, avt, lam_rows, diff_subln[l][None, :], nctx)
        gw = jnp.zeros((2, LANES, B_HEADS * B_KDIM), F32)
        for dd in range(2):
            gw = gw.at[dd, dd * GATE_RANK:(dd + 1) * GATE_RANK].set(gla_gate_w[l, dd])
        ob = _gla_call(bq, bk, bv, bvt, bg, gw.astype(BF16), gla_gate_b[l][:, None, :], gla_norm[l][None, :], nctx)
        oc = _attn_c_call(cqt, ck, cvt, swa_sink[l], nctx)
        wg = jnp.concatenate([w_mix_in[l, :, merge_cols:], w_mix_in[l, :, gate_cols:gate_cols + 512]],
                             axis=1).astype(BF16)
        last = l == depth - 1
        tail_row, skip = (None, nctx // TM) if last else (ctx_row, 0)
        t = _tail_call(t, mod, ng[1:2], wg, w_br_a[l].astype(BF16), w_br_b[l].astype(BF16),
                       w_br_c[l].astype(BF16), w_mix_out[l].astype(BF16), oa, ob, oc,
                       ng[2:3], w_ffn2_in[l, :, :ffn].astype(BF16), w_ffn2_in[l, :, ffn:].astype(BF16),
                       w_ffn2_out[l].astype(BF16), tail_row, skip, final_g[None, :] if last else None)
    return t
```

```python
import functools
import math

import jax
import jax.numpy as jnp
from jax import lax
from jax.experimental import pallas as pl
from jax.experimental.pallas import tpu as pltpu

F32 = jnp.float32
BF16 = jnp.bfloat16

HEAD = 64
ROPE_BASE = 10000.0
GRID_W = 64
N_MOD = 9
A_HEADS = 4
A_VROWS = 128 + 16
B_HEADS = 4
B_KDIM = 64
B_VDIM = 128
GATE_RANK = 16
GATE_TAU = 16.0
GLA_CHUNK = 64
C_HEADS = 8
C_KV_HEADS = 2
WINDOW = 128
NEG_INF = -1e30
EPS = 1e-6

TM = 256
LANES = 128
VMEM_LIMIT = 56 * 1024 * 1024

_AQ, _AK, _CQ, _CK, _AV, _BQ, _BK, _BV, _BG, _CV, _MIX_COLS = (
    0, 512, 1024, 1536, 1664, 2176, 2432, 2688, 3200, 3328, 3456)


def _cparams(n_grid, vmem=VMEM_LIMIT):
    return pltpu.CompilerParams(dimension_semantics=("arbitrary",) * n_grid, vmem_limit_bytes=vmem)


def _const_spec(shape):
    nd = len(shape)
    return pl.BlockSpec(shape, lambda *_: (0,) * nd, pipeline_mode=pl.Buffered(1))


def _rms(x):
    return x * lax.rsqrt(jnp.mean(x * x, axis=-1, keepdims=True) + EPS)


def _norm_mod(x, g, shift, scale):
    return _rms(x) * g * (1.0 + scale) + shift


def _dot(a, b):
    return jnp.dot(a, b, preferred_element_type=F32)


def _ada_body(c_ref, w_ref, b_ref, o_ref):
    sc = jax.nn.silu(c_ref[...]).astype(BF16)
    o_ref[...] = _dot(sc, w_ref[...]) + b_ref[...]


def _ada_call(cvec, w_ada, b_ada):
    depth, d, n_out = w_ada.shape
    rows = cvec.shape[0]
    tn = n_out // 4
    return pl.pallas_call(
        _ada_body,
        grid=(depth, n_out // tn),
        in_specs=[pl.BlockSpec((rows, d), lambda l, n: (0, 0)),
                  pl.BlockSpec((None, d, tn), lambda l, n: (l, 0, n)),
                  pl.BlockSpec((None, 1, tn), lambda l, n: (l, 0, n))],
        out_specs=pl.BlockSpec((None, rows, tn), lambda l, n: (l, 0, n)),
        out_shape=jax.ShapeDtypeStruct((depth, rows, n_out), F32),
        compiler_params=_cparams(2),
        name="adaln",
    )(cvec, w_ada, b_ada.reshape(depth, 1, n_out))


def _tok_spec(width, skip=0):
    return pl.BlockSpec((None, TM, width), lambda b, j: (b, j + skip, 0))


def _mod_spec(d, ctx_row):
    if ctx_row is None:
        return pl.BlockSpec((None, N_MOD, d), lambda b, j: (b, 0, 0))
    return pl.BlockSpec((None, N_MOD, d), lambda b, j: (jnp.where(j == 0, ctx_row, b), 0, 0))


def _swiglu_step(x, mod, g, wu_ref, wv_ref, wd_ref, mod0):
    hb = _norm_mod(x, g, mod[mod0:mod0 + 1], mod[mod0 + 1:mod0 + 2]).astype(BF16)
    f = wu_ref.shape[1]
    f_chunk = f
    acc = jnp.zeros(x.shape, F32)
    for lo in range(0, f, f_chunk):
        u = _dot(hb, wu_ref[:, lo:lo + f_chunk])
        v = _dot(hb, wv_ref[:, lo:lo + f_chunk])
        a = (jax.nn.silu(u) * v).astype(BF16)
        acc = acc + _dot(a, wd_ref[lo:lo + f_chunk, :])
    return x + 0.5 * mod[mod0 + 2:mod0 + 3] * acc


def _rope128(x, cos, sin_a, sin_b):
    return x * cos + pltpu.roll(x, LANES - 16, 1) * sin_a + pltpu.roll(x, 16, 1) * sin_b


def _head_body(*refs, split_input):
    refs = list(refs)
    if split_input:
        ctx_ref, lat_ref = refs[0:2]
        x = jnp.where(pl.program_id(1) == 0, ctx_ref[...], lat_ref[...])
        refs = refs[2:]
    else:
        x = refs[0][...]
        refs = refs[1:]
    (mod_ref, g0_ref, wu_ref, wv_ref, wd_ref, g_ref, w_ref, cos_ref, sa_ref, sb_ref, t_ref,
     aqt_ref, ak_ref, cqt_ref, ck_ref, avt_ref, bq_ref, bk_ref, bv_ref, bg_ref, cvt_ref, bvt_ref) = refs
    mod = mod_ref[...]
    x = _swiglu_step(x, mod, g0_ref[...], wu_ref, wv_ref, wd_ref, 0)
    t_ref[...] = x
    hb = _norm_mod(x, g_ref[...], mod[3:4], mod[4:5]).astype(BF16)
    cos, sin_a, sin_b = cos_ref[...], sa_ref[...], sb_ref[...]
    q_scale = HEAD ** -0.5 * math.log2(math.e)

    def proj(lo, hi):
        return _dot(hb, w_ref[:, lo:hi])

    def roped(p, i):
        return _rope128(p[:, i * LANES:(i + 1) * LANES], cos, sin_a, sin_b)

    p_aq, p_ak, p_cqk, p_av = proj(_AQ, _AK), proj(_AK, _CQ), proj(_CQ, _AV), proj(_AV, _BQ)
    for i in range(4):
        lo = i * LANES
        aqt_ref[lo:lo + LANES, :] = (roped(p_aq, i) * q_scale).T.astype(BF16)
        cqt_ref[lo:lo + LANES, :] = (roped(p_cqk, i) * q_scale).T.astype(BF16)
        ak_ref[:, lo:lo + LANES] = roped(p_ak, i).astype(BF16)
        avt_ref[i * A_VROWS:i * A_VROWS + LANES, :] = p_av[:, lo:lo + LANES].T.astype(BF16)
        avt_ref[i * A_VROWS + LANES:(i + 1) * A_VROWS, :] = jnp.ones((A_VROWS - LANES, TM), BF16)
    ck_ref[...] = roped(p_cqk, 4).astype(BF16)
    bq_ref[...] = proj(_BQ, _BK).astype(BF16)
    bk_ref[...] = proj(_BK, _BV).astype(BF16)
    bv = proj(_BV, _BG)
    bv_ref[...] = bv.astype(BF16)
    bvt_ref[...] = bv.T.astype(BF16)
    p_gc = proj(_BG, _MIX_COLS)
    bg_ref[...] = p_gc[:, 0:_CV - _BG].astype(BF16)
    cvt_ref[...] = p_gc[:, _CV - _BG:].T.astype(BF16)


def _head_call(t, mod, g0, wu, wv, wd, g1, w_mix, cos, sin_a, sin_b, ctx_row):
    split_input = isinstance(t, tuple)
    d, f = wu.shape
    if split_input:
        ctx, lat = t
        b, nt = lat.shape[0], ctx.shape[1] + lat.shape[1]
        acts = [ctx, lat]
        act_specs = [pl.BlockSpec((None, TM, d), lambda bb, j: (bb, 0, 0)),
                     pl.BlockSpec((None, TM, d), lambda bb, j: (bb, jnp.maximum(j - 1, 0), 0))]
    else:
        b, nt = t.shape[0], t.shape[1]
        acts, act_specs = [t], [_tok_spec(d)]
    tbl = pl.BlockSpec((TM, LANES), lambda bb, j: (j, 0))

    def tposed(rows):
        return pl.BlockSpec((None, rows, TM), lambda bb, j: (bb, 0, j))

    def sds(*shape):
        return jax.ShapeDtypeStruct(shape, BF16)

    return pl.pallas_call(
        functools.partial(_head_body, split_input=split_input),
        grid=(b, nt // TM),
        in_specs=act_specs + [_mod_spec(d, ctx_row), _const_spec((1, d)),
                              _const_spec((d, f)), _const_spec((d, f)), _const_spec((f, d)),
                              _const_spec((1, d)), _const_spec((d, _MIX_COLS)), tbl, tbl, tbl],
        out_specs=[_tok_spec(d),
                   tposed(512), _tok_spec(512), tposed(512), _tok_spec(128), tposed(A_HEADS * A_VROWS),
                   _tok_spec(256), _tok_spec(256), _tok_spec(512), _tok_spec(128), tposed(128), tposed(512)],
        out_shape=[jax.ShapeDtypeStruct((b, nt, d), F32),
                   sds(b, 512, nt), sds(b, nt, 512), sds(b, 512, nt), sds(b, nt, 128), sds(b, A_HEADS * A_VROWS, nt),
                   sds(b, nt, 256), sds(b, nt, 256), sds(b, nt, 512), sds(b, nt, 128), sds(b, 128, nt),
                   sds(b, 512, nt)],
        compiler_params=_cparams(2),
        name="head",
    )(*acts, mod, g0, wu, wv, wd, g1, w_mix, cos, sin_a, sin_b)


def _attn_a_body(lam_ref, qt_ref, qn_ref, k_ref, vt_ref, sub_ref, o_ref, s_ref, mx_ref, *, nctx, k_chunk):
    j = pl.program_id(1)
    last = pl.num_programs(1) - 1
    nt = k_ref.shape[0]
    row = lax.broadcasted_iota(jnp.int32, (LANES, 1), 0)
    lp = lam_ref[...]
    lam_init = lp[4:5, 0:1]
    lam = (jnp.exp(jnp.sum(lp[0:1] * lp[1:2], axis=-1, keepdims=True))
           - jnp.exp(jnp.sum(lp[2:3] * lp[3:4], axis=-1, keepdims=True)) + lam_init)
    out_gain = sub_ref[...] * (1.0 - lam_init)
    units = [(h, m) for h in range(A_HEADS) for m in range(2)]
    n_units = len(units)
    all_chunks = [(0, nctx)] + [(lo, k_chunk) for lo in range(nctx, nt, k_chunk)]

    def attend(n_keys, first_ready, prefetch):
        chunks = [c for c in all_chunks if c[0] < n_keys]
        state = {}

        def pass1(u, lo, w):
            nxt = u == n_units
            h, m = units[u % n_units]
            if lo == 0:
                qt = (qn_ref if nxt else qt_ref)[h * LANES:(h + 1) * LANES, :]
                state["q", u] = jnp.where((row < HEAD) if m == 0 else (row >= HEAD), qt, jnp.zeros_like(qt))
            s = _dot(k_ref[lo:lo + w, h * LANES:(h + 1) * LANES], state["q", u])
            s_ref[u % 2, lo:lo + w, :] = s
            pm = jnp.max(s, axis=0, keepdims=True)
            state["mx", u] = pm if lo == 0 else jnp.maximum(state["mx", u], pm)

        def pass2(u, lo, w):
            h, m = units[u]
            e = jnp.exp2(s_ref[u % 2, lo:lo + w, :] - state["mx", u])
            pv = _dot(vt_ref[h * A_VROWS:(h + 1) * A_VROWS, lo:lo + w], e.astype(BF16))
            state["acc", u] = pv if lo == 0 else state["acc", u] + pv

        def finish(u):
            h, m = units[u]
            acc = state.pop(("acc", u))
            res = acc[0:LANES] / acc[LANES:LANES + 1]
            if m == 0:
                state["head", h] = res
            else:
                o = (state.pop(("head", h)) - lam * res).T
                o_ref[:, h * LANES:(h + 1) * LANES] = (_rms(o) * out_gain).astype(BF16)

        if first_ready:
            state["mx", 0] = mx_ref[...]
        else:
            for lo, w in chunks:
                pass1(0, lo, w)
        for u in range(1, n_units):
            for lo, w in chunks:
                pass1(u, lo, w)
                pass2(u - 1, lo, w)
            finish(u - 1)
        if prefetch:
            for lo, w in all_chunks:
                pass1(n_units, lo, w)
                if (lo, w) in chunks:
                    pass2(n_units - 1, lo, w)
            mx_ref[...] = state["mx", n_units]
        else:
            for lo, w in chunks:
                pass2(n_units - 1, lo, w)
        finish(n_units - 1)

    @pl.when(j == 0)
    def _():
        attend(nctx, False, True)

    @pl.when(jnp.logical_and(j > 0, j < last))
    def _():
        attend(nt, True, True)

    @pl.when(j == last)
    def _():
        attend(nt, True, False)


def _attn_a_call(aqt, ak, avt, lam_rows, subln, nctx):
    b, nt, width = ak.shape
    k_chunk = 2048
    n_tiles = nt // TM
    assert (nt - nctx) % k_chunk == 0 and n_tiles >= 2
    return pl.pallas_call(
        functools.partial(_attn_a_body, nctx=nctx, k_chunk=k_chunk),
        grid=(b, n_tiles),
        in_specs=[pl.BlockSpec(lam_rows.shape, lambda bb, j: (0, 0)),
                  pl.BlockSpec((None, width, TM), lambda bb, j: (bb, 0, j)),
                  pl.BlockSpec((None, width, TM), lambda bb, j: (bb, 0, jnp.minimum(j + 1, n_tiles - 1))),
                  pl.BlockSpec((None, nt, width), lambda bb, j: (bb, 0, 0)),
                  pl.BlockSpec((None, avt.shape[1], nt), lambda bb, j: (bb, 0, 0)),
                  pl.BlockSpec((1, LANES), lambda bb, j: (0, 0))],
        out_specs=_tok_spec(width),
        out_shape=jax.ShapeDtypeStruct((b, nt, width), BF16),
        scratch_shapes=[pltpu.VMEM((2, nt, TM), F32), pltpu.VMEM((1, TM), F32)],
        compiler_params=_cparams(2),
        name="attn_a",
    )(lam_rows, aqt, aqt, ak, avt, subln)


def _dot_nt(a, b):
    return lax.dot_general(a, b, (((1,), (1,)), ((), ())), preferred_element_type=F32)


def _gla_body(q_ref, k_ref, v_ref, vt_ref, g_ref, gw_ref, gb_ref, gn_ref, o_ref, rev_ref, st_ref, *, nctx):
    nt = q_ref.shape[0]
    ch = GLA_CHUNK
    per_tile = TM // ch
    n_tiles = nt // TM
    n_ctx_tiles = nctx // TM

    row = lax.broadcasted_iota(jnp.int32, (TM, TM), 0)
    col = lax.broadcasted_iota(jnp.int32, (TM, TM), 1)
    same_chunk = (row // ch) == (col // ch)
    chunk_ones = jnp.where(same_chunk, 1.0, 0.0).astype(BF16)
    tok_chunk = lax.broadcasted_iota(jnp.int32, (TM, 1), 0) // ch
    srow = lax.broadcasted_iota(jnp.int32, (2 * B_VDIM, 2 * B_KDIM), 0)
    scol = lax.broadcasted_iota(jnp.int32, (2 * B_VDIM, 2 * B_KDIM), 1)
    same_head = (srow < B_VDIM) == (scol < B_KDIM)
    lane = lax.broadcasted_iota(jnp.int32, (1, LANES), 1)

    dirs = (0, 1)
    earlier = [jnp.logical_and(same_chunk, (col <= row) if d == 0 else (col >= row)) for d in dirs]
    tri = [jnp.where(e, 1.0, 0.0).astype(BF16) for e in earlier]
    half = [lane < B_KDIM, lane >= B_KDIM]

    out_refs = (o_ref, rev_ref)

    def advance(work):
        n = range(len(work))
        dr = [d for d, _ in work]
        rows = [pl.ds(t * TM if isinstance(t, int) else pl.multiple_of(t * TM, TM), TM) for _, t in work]
        z = [_dot(g_ref[rows[i], :], gw_ref[dr[i]]) + gb_ref[dr[i]] for i in n]
        la = [(jnp.minimum(x, 0.0) - jnp.log(1.0 + jnp.exp(-jnp.abs(x)))) * (1.0 / GATE_TAU) for x in z]
        hi = [x.astype(BF16) for x in la]
        pieces = [jnp.concatenate([hi[i], (la[i] - hi[i].astype(F32)).astype(BF16)], axis=1) for i in n]
        cum = [_dot(tri[dr[i]], pieces[i]) for i in n]
        cum = [x[:, :LANES] + x[:, LANES:] for x in cum]
        tot = [_dot(chunk_ones, pieces[i]) for i in n]
        tot = [x[:, :LANES] + x[:, LANES:] for x in tot]
        qf = [q_ref[rows[i], :].astype(F32) for i in n]
        kf = [k_ref[rows[i], :].astype(F32) for i in n]
        v = [v_ref[rows[i], :] for i in n]
        vt = [vt_ref[:, rows[i]] for i in n]
        qe = [qf[i] * (B_KDIM ** -0.5) * jnp.exp(cum[i]) for i in n]
        ke = [(kf[i] * jnp.exp(-cum[i])).astype(BF16) for i in n]
        kd = [(kf[i] * jnp.exp(tot[i] - cum[i])).astype(BF16) for i in n]
        decay = [jnp.exp(x) for x in tot]
        q_h = [[jnp.where(half[h], qe[i], 0.0).astype(BF16) for h in range(2)] for i in n]
        att = [[jnp.where(earlier[dr[i]], _dot_nt(q_h[i][h], ke[i]), 0.0).astype(BF16) for h in range(2)]
               for i in n]
        intra = [[_dot(att[i][h], v[i][:, h * B_VDIM:(h + 1) * B_VDIM]) for h in range(2)] for i in n]
        kd_c = [[jnp.where(tok_chunk == c, kd[i], jnp.zeros_like(kd[i])) for c in range(per_tile)] for i in n]
        inc2 = [[_dot(vt[i], jnp.concatenate(kd_c[i][c:c + 2], axis=1)) for c in range(0, per_tile, 2)]
                for i in n]
        incs = [[jnp.where(same_head, inc2[i][c // 2][:, (c % 2) * LANES:(c % 2 + 1) * LANES], 0.0)
                 for c in range(per_tile)] for i in n]
        state = [st_ref[d] for d in dirs]
        inter = [[None] * per_tile for _ in n]
        for i in n:
            d = dr[i]
            for c in (range(per_tile) if d == 0 else reversed(range(per_tile))):
                inter[i][c] = _dot_nt(qe[i][c * ch:(c + 1) * ch].astype(BF16), state[d].astype(BF16))
                state[d] = state[d] * decay[i][c * ch:c * ch + 1, :] + incs[i][c]
        for d in dirs:
            st_ref[d] = state[d]
        for i in n:
            out_refs[dr[i]][rows[i], :] = jnp.concatenate(intra[i], axis=1) + jnp.concatenate(inter[i], axis=0)

    st_ref[...] = jnp.zeros(st_ref.shape, F32)
    for i in range(n_ctx_tiles):
        advance([(0, i), (1, n_ctx_tiles - 1 - i)])

    def scan(i, carry):
        t_fwd = n_ctx_tiles + 2 * i
        t_rev = n_tiles - 1 - 2 * i
        advance([(0, t_fwd), (1, t_rev), (0, t_fwd + 1), (1, t_rev - 1)])
        return carry

    lax.fori_loop(0, (n_tiles - n_ctx_tiles) // 2, scan, 0)
    gn = gn_ref[...]

    def combine(t, carry):
        rows = pl.ds(pl.multiple_of(t * TM, TM), TM)
        tot = o_ref[rows, :] + rev_ref[rows, :]
        for h in range(2):
            lo = h * B_VDIM
            o_ref[rows, lo:lo + B_VDIM] = _rms(tot[:, lo:lo + B_VDIM]) * gn
        return carry

    lax.fori_loop(0, n_tiles, combine, 0)


def _gla_call(bq, bk, bv, bvt, bg, gw, gb, gn, nctx):
    b, nt, _ = bq.shape
    assert nctx % TM == 0 and (nt - nctx) % (2 * TM) == 0
    return pl.pallas_call(
        functools.partial(_gla_body, nctx=nctx),
        grid=(b, B_HEADS // 2),
        in_specs=[pl.BlockSpec((None, nt, LANES), lambda bb, p: (bb, 0, p)),
                  pl.BlockSpec((None, nt, LANES), lambda bb, p: (bb, 0, p)),
                  pl.BlockSpec((None, nt, 2 * B_VDIM), lambda bb, p: (bb, 0, p)),
                  pl.BlockSpec((None, 2 * B_VDIM, nt), lambda bb, p: (bb, p, 0)),
                  pl.BlockSpec((None, nt, LANES), lambda bb, p: (bb, 0, 0)),
                  pl.BlockSpec((2, LANES, LANES), lambda bb, p: (0, 0, p)),
                  pl.BlockSpec((2, 1, LANES), lambda bb, p: (0, 0, p)),
                  pl.BlockSpec((1, B_VDIM), lambda bb, p: (0, 0))],
        out_specs=pl.BlockSpec((None, nt, 2 * B_VDIM), lambda bb, p: (bb, 0, p)),
        out_shape=jax.ShapeDtypeStruct((b, nt, B_HEADS * B_VDIM), F32),
        scratch_shapes=[pltpu.VMEM((nt, 2 * B_VDIM), F32), pltpu.VMEM((2, 2 * B_VDIM, 2 * B_KDIM), F32)],
        compiler_params=_cparams(2),
        name="gla",
    )(bq, bk, bv, bvt, bg, gw, gb, gn)


def _attn_c_body(sink_ref, qt_ref, k_ref, vt_ref, o_ref, *, nctx, span):
    j = pl.program_id(1)
    nlat = k_ref.shape[0] - nctx
    per_kv = C_HEADS // C_KV_HEADS
    log2e = math.log2(math.e)

    def attend(k_rows, vt_cols, valid):
        k = jnp.concatenate([k_ref[pl.ds(s0, w), :] for s0, w in k_rows], axis=0)
        n_keys = k.shape[0]
        ones = jnp.ones((8, n_keys), BF16)
        groups = range(C_KV_HEADS)

        def q_stack(g):
            blocks = []
            for i in range(per_kv):
                h = g * per_kv + i
                qh = qt_ref[h * HEAD:(h + 1) * HEAD, :]
                zero = jnp.zeros_like(qh)
                blocks.append(jnp.concatenate([qh, zero] if g == 0 else [zero, qh], axis=0))
            return jnp.concatenate(blocks, axis=1)

        def masked(s):
            w_last = valid.shape[0]
            s_last = jnp.where(jnp.concatenate([valid] * per_kv, axis=1), s[n_keys - w_last:], NEG_INF)
            return jnp.concatenate([s[:n_keys - w_last], s_last], axis=0)

        s = [_dot(k, q_stack(g)) for g in groups]
        if valid is not None:
            s = [masked(x) for x in s]
        sink = [jnp.concatenate([jnp.full((1, TM), sink_ref[g * per_kv + i] * log2e, F32)
                                 for i in range(per_kv)], axis=1) for g in groups]
        mx = [jnp.maximum(jnp.max(s[g], axis=0, keepdims=True), sink[g]) for g in groups]
        e = [jnp.exp2(s[g] - mx[g]).astype(BF16) for g in groups]
        vt = [jnp.concatenate([vt_ref[g * HEAD:(g + 1) * HEAD, pl.ds(s0, w)] for s0, w in vt_cols], axis=1)
              for g in groups]
        pv = [_dot(jnp.concatenate([vt[g], ones], axis=0), e[g]) for g in groups]
        outs = []
        for g in groups:
            o = pv[g][0:HEAD] / (pv[g][HEAD:HEAD + 1] + jnp.exp2(sink[g] - mx[g]))
            outs += [o[:, i * TM:(i + 1) * TM] for i in range(per_kv)]
        o_ref[...] = jnp.concatenate(outs, axis=0).T.astype(BF16)

    @pl.when(j == 0)
    def _():
        attend([(0, nctx)], [(0, nctx)], None)

    @pl.when(j > 0)
    def _():
        q0 = (j - 1) * TM
        k0 = jnp.clip(q0 - WINDOW, 0, nlat - span)
        start = pl.multiple_of(nctx + k0, LANES)
        kpos = k0 + lax.broadcasted_iota(jnp.int32, (span, TM), 0)
        qpos = q0 + lax.broadcasted_iota(jnp.int32, (span, TM), 1)
        valid = jnp.abs(qpos - kpos) <= WINDOW
        attend([(0, nctx), (start, span)], [(0, nctx), (start, span)], valid)


def _attn_c_call(cqt, ck, cvt, sink, nctx):
    b, nt, kv_width = ck.shape
    width = cqt.shape[1]
    span = TM + 2 * WINDOW
    assert nt - nctx >= span
    return pl.pallas_call(
        functools.partial(_attn_c_body, nctx=nctx, span=span),
        grid=(b, nt // TM),
        in_specs=[pl.BlockSpec(memory_space=pltpu.SMEM),
                  pl.BlockSpec((None, width, TM), lambda bb, j: (bb, 0, j)),
                  pl.BlockSpec((None, nt, kv_width), lambda bb, j: (bb, 0, 0)),
                  pl.BlockSpec((None, kv_width, nt), lambda bb, j: (bb, 0, 0))],
        out_specs=_tok_spec(width),
        out_shape=jax.ShapeDtypeStruct((b, nt, width), BF16),
        compiler_params=_cparams(2),
        name="attn_c",
    )(sink, cqt, ck, cvt)


def _tail_body(x_ref, mod_ref, g1_ref, wg_ref, wa_ref, wb_ref, wc_ref, wo_ref, oa_ref, ob_ref, oc_ref,
               g2_ref, wu_ref, wv_ref, wd_ref, *rest, final_norm):
    o_ref = rest[-1]
    x = x_ref[...]
    d = x.shape[1]
    mod = mod_ref[...]
    hb = _norm_mod(x, g1_ref[...], mod[3:4], mod[4:5]).astype(BF16)
    r = _dot(hb, wg_ref[:, 3 * d:])
    ob = (ob_ref[...] * jax.nn.silu(r)).astype(BF16)
    y = jax.nn.sigmoid(_dot(hb, wg_ref[:, 0:d])) * _dot(oa_ref[...], wa_ref[...])
    y = y + jax.nn.sigmoid(_dot(hb, wg_ref[:, d:2 * d])) * _dot(ob, wb_ref[...])
    y = y + jax.nn.sigmoid(_dot(hb, wg_ref[:, 2 * d:3 * d])) * _dot(oc_ref[...], wc_ref[...])
    x = x + mod[5:6] * _dot(y.astype(BF16), wo_ref[...])
    x = _swiglu_step(x, mod, g2_ref[...], wu_ref, wv_ref, wd_ref, 6)
    if final_norm:
        x = _rms(x) * rest[0][...]
    o_ref[...] = x


def _tail_call(t, mod, g1, wg, wa, wb, wc, wo, oa, ob, oc, g2, wu, wv, wd, ctx_row, skip_tiles=0, final_g=None):
    b, nt, d = t.shape
    n_tiles = nt // TM - skip_tiles
    extra, extra_specs = ([], []) if final_g is None else ([final_g], [_const_spec((1, d))])
    return pl.pallas_call(
        functools.partial(_tail_body, final_norm=final_g is not None),
        grid=(b, n_tiles),
        in_specs=[_tok_spec(d, skip_tiles), _mod_spec(d, ctx_row), _const_spec((1, d)), _const_spec(wg.shape),
                  _const_spec(wa.shape), _const_spec(wb.shape), _const_spec(wc.shape), _const_spec(wo.shape),
                  _tok_spec(oa.shape[2], skip_tiles), _tok_spec(ob.shape[2], skip_tiles),
                  _tok_spec(oc.shape[2], skip_tiles),
                  _const_spec((1, d)), _const_spec(wu.shape), _const_spec(wv.shape), _const_spec(wd.shape)]
        + extra_specs,
        out_specs=_tok_spec(d),
        out_shape=jax.ShapeDtypeStruct((b, n_tiles * TM, d), F32),
        compiler_params=_cparams(2),
        name="tail",
    )(t, mod, g1, wg, wa, wb, wc, wo, oa, ob, oc, g2, wu, wv, wd, *extra)


def _rope_tables(n_lat, nctx):
    rows = n_lat // GRID_W
    row = jnp.repeat(jnp.arange(rows, dtype=F32), GRID_W)
    col = jnp.tile(jnp.arange(GRID_W, dtype=F32), rows)
    n_freq = HEAD // 4
    freqs = jnp.power(ROPE_BASE, -jnp.arange(n_freq, dtype=F32) / n_freq)
    ar = row[:, None] * freqs
    ac = col[:, None] * freqs
    ang = jnp.concatenate([ar, ar, ac, ac], axis=-1)
    ang = jnp.concatenate([ang, ang], axis=-1)
    first = (jnp.arange(LANES) % 32) < 16
    cos, sin = jnp.cos(ang), jnp.sin(ang)
    sin_a = jnp.where(first, -sin, 0.0)
    sin_b = jnp.where(first, 0.0, sin)
    pad = lambda a, v: jnp.concatenate([jnp.full((nctx, LANES), v, F32), a], axis=0)
    return pad(cos, 1.0), pad(sin_a, 0.0), pad(sin_b, 0.0)


def _mixin_weight(w):
    aq, ak, av = w[:, 0:512], w[:, 512:1024], w[:, 1024:1536]
    bq, bk, bv = w[:, 1536:1792], w[:, 1792:2048], w[:, 2048:2560]
    bg = jnp.pad(w[:, 2560:2560 + 2 * GATE_RANK], ((0, 0), (0, LANES - 2 * GATE_RANK)))
    cq, ck, cv = w[:, 3104:3616], w[:, 3616:3744], w[:, 3744:3872]
    return jnp.concatenate([aq, ak, cq, ck, av, bq, bk, bv, bg, cv], axis=1).astype(BF16)


def kernel(x, c, ctx, c_ctx, w_ada, b_ada, norm_g, w_ffn1_in, w_ffn1_out, w_ffn2_in, w_ffn2_out,
           w_mix_in, diff_lambda, diff_subln, gla_gate_w, gla_gate_b, gla_norm, swa_sink,
           w_br_a, w_br_b, w_br_c, w_mix_out, final_g):
    bsz, n, d = x.shape
    nctx = ctx.shape[1]
    depth = w_ada.shape[0]
    ffn = w_ffn1_out.shape[1]
    assert nctx == TM and n % 512 == 0 and d % LANES == 0

    t = (ctx, x)
    ctx_row = bsz
    n_rows = -(-(bsz + 1) // 8) * 8
    cvec = jnp.concatenate([c, c_ctx[None, :], jnp.zeros((n_rows - bsz - 1, d), F32)], axis=0)
    mod_all = _ada_call(cvec, w_ada.astype(BF16), b_ada).reshape(depth, n_rows, N_MOD, d)
    cos, sin_a, sin_b = _rope_tables(n, nctx)
    gate_cols = 2560 + 2 * GATE_RANK
    merge_cols = 3872

    for l in range(depth):
        lam_init = 0.8 - 0.6 * math.exp(-0.3 * l)
        mod = mod_all[l]
        ng = norm_g[l]
        t, aqt, ak, cqt, ck, avt, bq, bk, bv, bg, cvt, bvt = _head_call(
            t, mod, ng[0:1], w_ffn1_in[l, :, :ffn].astype(BF16), w_ffn1_in[l, :, ffn:].astype(BF16),
            w_ffn1_out[l].astype(BF16), ng[1:2], _mixin_weight(w_mix_in[l]), cos, sin_a, sin_b, ctx_row)
        lam_rows = jnp.concatenate([diff_lambda[l], jnp.full((4, HEAD), lam_init, F32)], axis=0)
        oa = _attn_a_call(aqt, ak, avt, lam_rows, diff_subln[l][None, :], nctx)
        gw = jnp.zeros((2, LANES, B_HEADS * B_KDIM), F32)
        for dd in range(2):
            gw = gw.at[dd, dd * GATE_RANK:(dd + 1) * GATE_RANK].set(gla_gate_w[l, dd])
        ob = _gla_call(bq, bk, bv, bvt, bg, gw.astype(BF16), gla_gate_b[l][:, None, :], gla_norm[l][None, :], nctx)
        oc = _attn_c_call(cqt, ck, cvt, swa_sink[l], nctx)
        wg = jnp.concatenate([w_mix_in[l, :, merge_cols:], w_mix_in[l, :, gate_cols:gate_cols + 512]],
                             axis=1).astype(BF16)
        last = l == depth - 1
        tail_row, skip = (None, nctx // TM) if last else (ctx_row, 0)
        t = _tail_call(t, mod, ng[1:2], wg, w_br_a[l].astype(BF16), w_br_b[l].astype(BF16),
                       w_br_c[l].astype(BF16), w_mix_out[l].astype(BF16), oa, ob, oc,
                       ng[2:3], w_ffn2_in[l, :, :ffn].astype(BF16), w_ffn2_in[l, :, ffn:].astype(BF16),
                       w_ffn2_out[l].astype(BF16), tail_row, skip, final_g[None, :] if last else None)
    return t
```

```python
import functools
import math

import jax
import jax.numpy as jnp
from jax import lax
from jax.experimental import pallas as pl
from jax.experimental.pallas import tpu as pltpu

F32 = jnp.float32
BF16 = jnp.bfloat16

HEAD = 64
ROPE_BASE = 10000.0
GRID_W = 64
N_MOD = 9
A_HEADS = 4
A_VROWS = 128 + 16
B_HEADS = 4
B_KDIM = 64
B_VDIM = 128
GATE_RANK = 16
GATE_TAU = 16.0
GLA_CHUNK = 64
C_HEADS = 8
C_KV_HEADS = 2
WINDOW = 128
NEG_INF = -1e30
EPS = 1e-6

TM = 256
LANES = 128
VMEM_LIMIT = 56 * 1024 * 1024

_AQ, _AK, _CQ, _CK, _AV, _BQ, _BK, _BV, _BG, _CV, _MIX_COLS = (
    0, 512, 1024, 1536, 1664, 2176, 2432, 2688, 3200, 3328, 3456)


def _cparams(n_grid, vmem=VMEM_LIMIT):
    return pltpu.CompilerParams(dimension_semantics=("arbitrary",) * n_grid, vmem_limit_bytes=vmem)


def _const_spec(shape):
    nd = len(shape)
    return pl.BlockSpec(shape, lambda *_: (0,) * nd, pipeline_mode=pl.Buffered(1))


def _rms(x):
    return x * lax.rsqrt(jnp.mean(x * x, axis=-1, keepdims=True) + EPS)


def _norm_mod(x, g, shift, scale):
    return _rms(x) * g * (1.0 + scale) + shift


def _dot(a, b):
    return jnp.dot(a, b, preferred_element_type=F32)


def _ada_body(c_ref, w_ref, b_ref, o_ref):
    sc = jax.nn.silu(c_ref[...]).astype(BF16)
    o_ref[...] = _dot(sc, w_ref[...]) + b_ref[...]


def _ada_call(cvec, w_ada, b_ada):
    depth, d, n_out = w_ada.shape
    rows = cvec.shape[0]
    tn = n_out // 4
    return pl.pallas_call(
        _ada_body,
        grid=(depth, n_out // tn),
        in_specs=[pl.BlockSpec((rows, d), lambda l, n: (0, 0)),
                  pl.BlockSpec((None, d, tn), lambda l, n: (l, 0, n)),
                  pl.BlockSpec((None, 1, tn), lambda l, n: (l, 0, n))],
        out_specs=pl.BlockSpec((None, rows, tn), lambda l, n: (l, 0, n)),
        out_shape=jax.ShapeDtypeStruct((depth, rows, n_out), F32),
        compiler_params=_cparams(2),
        name="adaln",
    )(cvec, w_ada, b_ada.reshape(depth, 1, n_out))


def _tok_spec(width, skip=0):
    return pl.BlockSpec((None, TM, width), lambda b, j: (b, j + skip, 0))


def _mod_spec(d, ctx_row):
    if ctx_row is None:
        return pl.BlockSpec((None, N_MOD, d), lambda b, j: (b, 0, 0))
    return pl.BlockSpec((None, N_MOD, d), lambda b, j: (jnp.where(j == 0, ctx_row, b), 0, 0))


def _swiglu_step(x, mod, g, wu_ref, wv_ref, wd_ref, mod0):
    hb = _norm_mod(x, g, mod[mod0:mod0 + 1], mod[mod0 + 1:mod0 + 2]).astype(BF16)
    a = (jax.nn.silu(_dot(hb, wu_ref[...])) * _dot(hb, wv_ref[...])).astype(BF16)
    return x + 0.5 * mod[mod0 + 2:mod0 + 3] * _dot(a, wd_ref[...])


def _rope128(x, cos, sin_a, sin_b):
    return x * cos + pltpu.roll(x, LANES - 16, 1) * sin_a + pltpu.roll(x, 16, 1) * sin_b


def _head_body(*refs, split_input):
    refs = list(refs)
    if split_input:
        ctx_ref, lat_ref = refs[0:2]
        x = jnp.where(pl.program_id(1) == 0, ctx_ref[...], lat_ref[...])
        refs = refs[2:]
    else:
        x = refs[0][...]
        refs = refs[1:]
    (mod_ref, g0_ref, wu_ref, wv_ref, wd_ref, g_ref, w_ref, cos_ref, sa_ref, sb_ref, t_ref,
     aqt_ref, ak_ref, cqt_ref, ck_ref, avt_ref, bq_ref, bk_ref, bv_ref, bg_ref, cvt_ref, bvt_ref) = refs
    mod = mod_ref[...]
    x = _swiglu_step(x, mod, g0_ref[...], wu_ref, wv_ref, wd_ref, 0)
    t_ref[...] = x
    hb = _norm_mod(x, g_ref[...], mod[3:4], mod[4:5]).astype(BF16)
    cos, sin_a, sin_b = cos_ref[...], sa_ref[...], sb_ref[...]
    q_scale = HEAD ** -0.5 * math.log2(math.e)

    def proj(lo, hi):
        return _dot(hb, w_ref[:, lo:hi])

    def roped(p, i):
        return _rope128(p[:, i * LANES:(i + 1) * LANES], cos, sin_a, sin_b)

    p_aq, p_ak, p_cqk, p_av = proj(_AQ, _AK), proj(_AK, _CQ), proj(_CQ, _AV), proj(_AV, _BQ)
    for i in range(4):
        lo = i * LANES
        aqt_ref[lo:lo + LANES, :] = (roped(p_aq, i) * q_scale).T.astype(BF16)
        cqt_ref[lo:lo + LANES, :] = (roped(p_cqk, i) * q_scale).T.astype(BF16)
        ak_ref[:, lo:lo + LANES] = roped(p_ak, i).astype(BF16)
        avt_ref[i * A_VROWS:i * A_VROWS + LANES, :] = p_av[:, lo:lo + LANES].T.astype(BF16)
        avt_ref[i * A_VROWS + LANES:(i + 1) * A_VROWS, :] = jnp.ones((A_VROWS - LANES, TM), BF16)
    ck_ref[...] = roped(p_cqk, 4).astype(BF16)
    bq_ref[...] = proj(_BQ, _BK).astype(BF16)
    bk_ref[...] = proj(_BK, _BV).astype(BF16)
    bv = proj(_BV, _BG)
    bv_ref[...] = bv.astype(BF16)
    bvt_ref[...] = bv.T.astype(BF16)
    p_gc = proj(_BG, _MIX_COLS)
    bg_ref[...] = p_gc[:, 0:_CV - _BG].astype(BF16)
    cvt_ref[...] = p_gc[:, _CV - _BG:].T.astype(BF16)


def _head_call(t, mod, g0, wu, wv, wd, g1, w_mix, cos, sin_a, sin_b, ctx_row):
    split_input = isinstance(t, tuple)
    d, f = wu.shape
    if split_input:
        ctx, lat = t
        b, nt = lat.shape[0], ctx.shape[1] + lat.shape[1]
        acts = [ctx, lat]
        act_specs = [pl.BlockSpec((None, TM, d), lambda bb, j: (bb, 0, 0)),
                     pl.BlockSpec((None, TM, d), lambda bb, j: (bb, jnp.maximum(j - 1, 0), 0))]
    else:
        b, nt = t.shape[0], t.shape[1]
        acts, act_specs = [t], [_tok_spec(d)]
    tbl = pl.BlockSpec((TM, LANES), lambda bb, j: (j, 0))

    def tposed(rows):
        return pl.BlockSpec((None, rows, TM), lambda bb, j: (bb, 0, j))

    def sds(*shape):
        return jax.ShapeDtypeStruct(shape, BF16)

    return pl.pallas_call(
        functools.partial(_head_body, split_input=split_input),
        grid=(b, nt // TM),
        in_specs=act_specs + [_mod_spec(d, ctx_row), _const_spec((1, d)),
                              _const_spec((d, f)), _const_spec((d, f)), _const_spec((f, d)),
                              _const_spec((1, d)), _const_spec((d, _MIX_COLS)), tbl, tbl, tbl],
        out_specs=[_tok_spec(d),
                   tposed(512), _tok_spec(512), tposed(512), _tok_spec(128), tposed(A_HEADS * A_VROWS),
                   _tok_spec(256), _tok_spec(256), _tok_spec(512), _tok_spec(128), tposed(128), tposed(512)],
        out_shape=[jax.ShapeDtypeStruct((b, nt, d), F32),
                   sds(b, 512, nt), sds(b, nt, 512), sds(b, 512, nt), sds(b, nt, 128), sds(b, A_HEADS * A_VROWS, nt),
                   sds(b, nt, 256), sds(b, nt, 256), sds(b, nt, 512), sds(b, nt, 128), sds(b, 128, nt),
                   sds(b, 512, nt)],
        compiler_params=_cparams(2),
        name="head",
    )(*acts, mod, g0, wu, wv, wd, g1, w_mix, cos, sin_a, sin_b)


def _attn_a_body(lam_ref, qt_ref, qn_ref, k_ref, vt_ref, sub_ref, o_ref, s_ref, mx_ref, *, nctx, k_chunk):
    j = pl.program_id(1)
    last = pl.num_programs(1) - 1
    nt = k_ref.shape[0]
    row = lax.broadcasted_iota(jnp.int32, (LANES, 1), 0)
    lp = lam_ref[...]
    lam_init = lp[4:5, 0:1]
    lam = (jnp.exp(jnp.sum(lp[0:1] * lp[1:2], axis=-1, keepdims=True))
           - jnp.exp(jnp.sum(lp[2:3] * lp[3:4], axis=-1, keepdims=True)) + lam_init)
    out_gain = sub_ref[...] * (1.0 - lam_init)
    units = [(h, m) for h in range(A_HEADS) for m in range(2)]
    n_units = len(units)
    all_chunks = [(0, nctx)] + [(lo, k_chunk) for lo in range(nctx, nt, k_chunk)]

    def attend(n_keys, first_ready, prefetch):
        chunks = [c for c in all_chunks if c[0] < n_keys]
        state = {}

        def pass1(u, lo, w):
            nxt = u == n_units
            h, m = units[u % n_units]
            if lo == 0:
                qt = (qn_ref if nxt else qt_ref)[h * LANES:(h + 1) * LANES, :]
                state["q", u] = jnp.where((row < HEAD) if m == 0 else (row >= HEAD), qt, jnp.zeros_like(qt))
            s = _dot(k_ref[lo:lo + w, h * LANES:(h + 1) * LANES], state["q", u])
            s_ref[u % 2, lo:lo + w, :] = s
            pm = jnp.max(s, axis=0, keepdims=True)
            state["mx", u] = pm if lo == 0 else jnp.maximum(state["mx", u], pm)

        def pass2(u, lo, w):
            h, m = units[u]
            e = jnp.exp2(s_ref[u % 2, lo:lo + w, :] - state["mx", u])
            pv = _dot(vt_ref[h * A_VROWS:(h + 1) * A_VROWS, lo:lo + w], e.astype(BF16))
            state["acc", u] = pv if lo == 0 else state["acc", u] + pv

        def finish(u):
            h, m = units[u]
            acc = state.pop(("acc", u))
            res = acc[0:LANES] / acc[LANES:LANES + 1]
            if m == 0:
                state["head", h] = res
            else:
                o = (state.pop(("head", h)) - lam * res).T
                o_ref[:, h * LANES:(h + 1) * LANES] = (_rms(o) * out_gain).astype(BF16)

        if first_ready:
            state["mx", 0] = mx_ref[...]
        else:
            for lo, w in chunks:
                pass1(0, lo, w)
        for u in range(1, n_units):
            for lo, w in chunks:
                pass1(u, lo, w)
                pass2(u - 1, lo, w)
            finish(u - 1)
        if prefetch:
            for lo, w in all_chunks:
                pass1(n_units, lo, w)
                if (lo, w) in chunks:
                    pass2(n_units - 1, lo, w)
            mx_ref[...] = state["mx", n_units]
        else:
            for lo, w in chunks:
                pass2(n_units - 1, lo, w)
        finish(n_units - 1)

    @pl.when(j == 0)
    def _():
        attend(nctx, False, True)

    @pl.when(jnp.logical_and(j > 0, j < last))
    def _():
        attend(nt, True, True)

    @pl.when(j == last)
    def _():
        attend(nt, True, False)


def _attn_a_call(aqt, ak, avt, lam_rows, subln, nctx):
    b, nt, width = ak.shape
    k_chunk = 2048
    n_tiles = nt // TM
    assert (nt - nctx) % k_chunk == 0 and n_tiles >= 2
    return pl.pallas_call(
        functools.partial(_attn_a_body, nctx=nctx, k_chunk=k_chunk),
        grid=(b, n_tiles),
        in_specs=[pl.BlockSpec(lam_rows.shape, lambda bb, j: (0, 0)),
                  pl.BlockSpec((None, width, TM), lambda bb, j: (bb, 0, j)),
                  pl.BlockSpec((None, width, TM), lambda bb, j: (bb, 0, jnp.minimum(j + 1, n_tiles - 1))),
                  pl.BlockSpec((None, nt, width), lambda bb, j: (bb, 0, 0)),
                  pl.BlockSpec((None, avt.shape[1], nt), lambda bb, j: (bb, 0, 0)),
                  pl.BlockSpec((1, LANES), lambda bb, j: (0, 0))],
        out_specs=_tok_spec(width),
        out_shape=jax.ShapeDtypeStruct((b, nt, width), BF16),
        scratch_shapes=[pltpu.VMEM((2, nt, TM), F32), pltpu.VMEM((1, TM), F32)],
        compiler_params=_cparams(2),
        name="attn_a",
    )(lam_rows, aqt, aqt, ak, avt, subln)


def _dot_nt(a, b):
    return lax.dot_general(a, b, (((1,), (1,)), ((), ())), preferred_element_type=F32)


def _gla_body(q_ref, k_ref, v_ref, vt_ref, g_ref, gw_ref, gb_ref, gn_ref, o_ref, rev_ref, st_ref, *, nctx):
    nt = q_ref.shape[0]
    ch = GLA_CHUNK
    per_tile = TM // ch
    n_tiles = nt // TM
    n_ctx_tiles = nctx // TM

    row = lax.broadcasted_iota(jnp.int32, (TM, TM), 0)
    col = lax.broadcasted_iota(jnp.int32, (TM, TM), 1)
    same_chunk = (row // ch) == (col // ch)
    tok_chunk = lax.broadcasted_iota(jnp.int32, (TM, 1), 0) // ch
    srow = lax.broadcasted_iota(jnp.int32, (2 * B_VDIM, 2 * B_KDIM), 0)
    scol = lax.broadcasted_iota(jnp.int32, (2 * B_VDIM, 2 * B_KDIM), 1)
    same_head = (srow < B_VDIM) == (scol < B_KDIM)
    lane = lax.broadcasted_iota(jnp.int32, (1, LANES), 1)

    dirs = (0, 1)
    earlier = [jnp.logical_and(same_chunk, (col <= row) if d == 0 else (col >= row)) for d in dirs]
    tri = [jnp.where(e, 1.0, 0.0).astype(BF16) for e in earlier]
    half = [lane < B_KDIM, lane >= B_KDIM]

    out_refs = (o_ref, rev_ref)

    def advance(work):
        n = range(len(work))
        dr = [d for d, _ in work]
        rows = [pl.ds(t * TM if isinstance(t, int) else pl.multiple_of(t * TM, TM), TM) for _, t in work]
        z = [_dot(g_ref[rows[i], :], gw_ref[dr[i]]) + gb_ref[dr[i]] for i in n]
        la = [(jnp.minimum(x, 0.0) - jnp.log(1.0 + jnp.exp(-jnp.abs(x)))) * (1.0 / GATE_TAU) for x in z]
        hi = [x.astype(BF16) for x in la]
        pieces = [jnp.concatenate([hi[i], (la[i] - hi[i].astype(F32)).astype(BF16)], axis=1) for i in n]
        cum = [_dot(tri[dr[i]], pieces[i]) for i in n]
        cum = [x[:, :LANES] + x[:, LANES:] for x in cum]
        ends = [x.reshape(per_tile, ch, LANES) for x in cum]
        ends = [ends[i][:, ch - 1:ch, :] if dr[i] == 0 else ends[i][:, 0:1, :] for i in n]
        tot = [jnp.broadcast_to(x, (per_tile, ch, LANES)).reshape(TM, LANES) for x in ends]
        qf = [q_ref[rows[i], :].astype(F32) for i in n]
        kf = [k_ref[rows[i], :].astype(F32) for i in n]
        v = [v_ref[rows[i], :] for i in n]
        vt = [vt_ref[:, rows[i]] for i in n]
        qe = [qf[i] * (B_KDIM ** -0.5) * jnp.exp(cum[i]) for i in n]
        ke = [(kf[i] * jnp.exp(-cum[i])).astype(BF16) for i in n]
        kd = [(kf[i] * jnp.exp(tot[i] - cum[i])).astype(BF16) for i in n]
        decay = [jnp.exp(x) for x in tot]
        q_h = [[jnp.where(half[h], qe[i], 0.0).astype(BF16) for h in range(2)] for i in n]
        att = [[jnp.where(earlier[dr[i]], _dot_nt(q_h[i][h], ke[i]), 0.0).astype(BF16) for h in range(2)]
               for i in n]
        intra = [[_dot(att[i][h], v[i][:, h * B_VDIM:(h + 1) * B_VDIM]) for h in range(2)] for i in n]
        kd_c = [[jnp.where(tok_chunk == c, kd[i], jnp.zeros_like(kd[i])) for c in range(per_tile)] for i in n]
        inc2 = [[_dot(vt[i], jnp.concatenate(kd_c[i][c:c + 2], axis=1)) for c in range(0, per_tile, 2)]
                for i in n]
        incs = [[jnp.where(same_head, inc2[i][c // 2][:, (c % 2) * LANES:(c % 2 + 1) * LANES], 0.0)
                 for c in range(per_tile)] for i in n]
        state = [st_ref[d] for d in dirs]
        inter = [[None] * per_tile for _ in n]
        for i in n:
            d = dr[i]
            for c in (range(per_tile) if d == 0 else reversed(range(per_tile))):
                inter[i][c] = _dot_nt(qe[i][c * ch:(c + 1) * ch].astype(BF16), state[d].astype(BF16))
                state[d] = state[d] * decay[i][c * ch:c * ch + 1, :] + incs[i][c]
        for d in dirs:
            st_ref[d] = state[d]
        for i in n:
            out_refs[dr[i]][rows[i], :] = jnp.concatenate(intra[i], axis=1) + jnp.concatenate(inter[i], axis=0)

    st_ref[...] = jnp.zeros(st_ref.shape, F32)
    for i in range(n_ctx_tiles):
        advance([(0, i), (1, n_ctx_tiles - 1 - i)])

    def scan(i, carry):
        t_fwd = n_ctx_tiles + 2 * i
        t_rev = n_tiles - 1 - 2 * i
        advance([(0, t_fwd), (1, t_rev), (0, t_fwd + 1), (1, t_rev - 1)])
        return carry

    lax.fori_loop(0, (n_tiles - n_ctx_tiles) // 2, scan, 0)
    gn = gn_ref[...]

    def combine(t, carry):
        rows = pl.ds(pl.multiple_of(t * TM, TM), TM)
        tot = o_ref[rows, :] + rev_ref[rows, :]
        for h in range(2):
            lo = h * B_VDIM
            o_ref[rows, lo:lo + B_VDIM] = _rms(tot[:, lo:lo + B_VDIM]) * gn
        return carry

    lax.fori_loop(0, n_tiles, combine, 0)


def _gla_call(bq, bk, bv, bvt, bg, gw, gb, gn, nctx):
    b, nt, _ = bq.shape
    assert nctx % TM == 0 and (nt - nctx) % (2 * TM) == 0
    return pl.pallas_call(
        functools.partial(_gla_body, nctx=nctx),
        grid=(b, B_HEADS // 2),
        in_specs=[pl.BlockSpec((None, nt, LANES), lambda bb, p: (bb, 0, p)),
                  pl.BlockSpec((None, nt, LANES), lambda bb, p: (bb, 0, p)),
                  pl.BlockSpec((None, nt, 2 * B_VDIM), lambda bb, p: (bb, 0, p)),
                  pl.BlockSpec((None, 2 * B_VDIM, nt), lambda bb, p: (bb, p, 0)),
                  pl.BlockSpec((None, nt, LANES), lambda bb, p: (bb, 0, 0)),
                  pl.BlockSpec((2, LANES, LANES), lambda bb, p: (0, 0, p)),
                  pl.BlockSpec((2, 1, LANES), lambda bb, p: (0, 0, p)),
                  pl.BlockSpec((1, B_VDIM), lambda bb, p: (0, 0))],
        out_specs=pl.BlockSpec((None, nt, 2 * B_VDIM), lambda bb, p: (bb, 0, p)),
        out_shape=jax.ShapeDtypeStruct((b, nt, B_HEADS * B_VDIM), F32),
        scratch_shapes=[pltpu.VMEM((nt, 2 * B_VDIM), F32), pltpu.VMEM((2, 2 * B_VDIM, 2 * B_KDIM), F32)],
        compiler_params=_cparams(2),
        name="gla",
    )(bq, bk, bv, bvt, bg, gw, gb, gn)


def _attn_c_body(sink_ref, qt_ref, k_ref, vt_ref, o_ref, *, nctx, span):
    j = pl.program_id(1)
    nlat = k_ref.shape[0] - nctx
    per_kv = C_HEADS // C_KV_HEADS
    log2e = math.log2(math.e)

    def attend(windows, valid):
        k = jnp.concatenate([k_ref[pl.ds(s0, w), :] for s0, w in windows], axis=0)
        n_keys = k.shape[0]
        ones = jnp.ones((8, n_keys), BF16)
        groups = range(C_KV_HEADS)

        def q_stack(g):
            blocks = []
            for i in range(per_kv):
                h = g * per_kv + i
                qh = qt_ref[h * HEAD:(h + 1) * HEAD, :]
                zero = jnp.zeros_like(qh)
                blocks.append(jnp.concatenate([qh, zero] if g == 0 else [zero, qh], axis=0))
            return jnp.concatenate(blocks, axis=1)

        def masked(s):
            w_last = valid.shape[0]
            s_last = jnp.where(jnp.concatenate([valid] * per_kv, axis=1), s[n_keys - w_last:], NEG_INF)
            return jnp.concatenate([s[:n_keys - w_last], s_last], axis=0)

        s = [_dot(k, q_stack(g)) for g in groups]
        if valid is not None:
            s = [masked(x) for x in s]
        sink = [jnp.concatenate([jnp.full((1, TM), sink_ref[g * per_kv + i] * log2e, F32)
                                 for i in range(per_kv)], axis=1) for g in groups]
        mx = [jnp.maximum(jnp.max(s[g], axis=0, keepdims=True), sink[g]) for g in groups]
        e = [jnp.exp2(s[g] - mx[g]).astype(BF16) for g in groups]
        vt = [jnp.concatenate([vt_ref[g * HEAD:(g + 1) * HEAD, pl.ds(s0, w)] for s0, w in windows], axis=1)
              for g in groups]
        pv = [_dot(jnp.concatenate([vt[g], ones], axis=0), e[g]) for g in groups]
        outs = []
        for g in groups:
            o = pv[g][0:HEAD] / (pv[g][HEAD:HEAD + 1] + jnp.exp2(sink[g] - mx[g]))
            outs += [o[:, i * TM:(i + 1) * TM] for i in range(per_kv)]
        o_ref[...] = jnp.concatenate(outs, axis=0).T.astype(BF16)

    @pl.when(j == 0)
    def _():
        attend([(0, nctx)], None)

    @pl.when(j > 0)
    def _():
        q0 = (j - 1) * TM
        k0 = jnp.clip(q0 - WINDOW, 0, nlat - span)
        start = pl.multiple_of(nctx + k0, LANES)
        kpos = k0 + lax.broadcasted_iota(jnp.int32, (span, TM), 0)
        qpos = q0 + lax.broadcasted_iota(jnp.int32, (span, TM), 1)
        valid = jnp.abs(qpos - kpos) <= WINDOW
        attend([(0, nctx), (start, span)], valid)


def _attn_c_call(cqt, ck, cvt, sink, nctx):
    b, nt, kv_width = ck.shape
    width = cqt.shape[1]
    span = TM + 2 * WINDOW
    assert nt - nctx >= span
    return pl.pallas_call(
        functools.partial(_attn_c_body, nctx=nctx, span=span),
        grid=(b, nt // TM),
        in_specs=[pl.BlockSpec(memory_space=pltpu.SMEM),
                  pl.BlockSpec((None, width, TM), lambda bb, j: (bb, 0, j)),
                  pl.BlockSpec((None, nt, kv_width), lambda bb, j: (bb, 0, 0)),
                  pl.BlockSpec((None, kv_width, nt), lambda bb, j: (bb, 0, 0))],
        out_specs=_tok_spec(width),
        out_shape=jax.ShapeDtypeStruct((b, nt, width), BF16),
        compiler_params=_cparams(2),
        name="attn_c",
    )(sink, cqt, ck, cvt)


def _tail_body(x_ref, mod_ref, g1_ref, wg_ref, wa_ref, wb_ref, wc_ref, wo_ref, oa_ref, ob_ref, oc_ref,
               g2_ref, wu_ref, wv_ref, wd_ref, *rest, final_norm):
    o_ref = rest[-1]
    x = x_ref[...]
    d = x.shape[1]
    mod = mod_ref[...]
    hb = _norm_mod(x, g1_ref[...], mod[3:4], mod[4:5]).astype(BF16)
    r = _dot(hb, wg_ref[:, 3 * d:])
    ob = (ob_ref[...] * jax.nn.silu(r)).astype(BF16)
    y = jax.nn.sigmoid(_dot(hb, wg_ref[:, 0:d])) * _dot(oa_ref[...], wa_ref[...])
    y = y + jax.nn.sigmoid(_dot(hb, wg_ref[:, d:2 * d])) * _dot(ob, wb_ref[...])
    y = y + jax.nn.sigmoid(_dot(hb, wg_ref[:, 2 * d:3 * d])) * _dot(oc_ref[...], wc_ref[...])
    x = x + mod[5:6] * _dot(y.astype(BF16), wo_ref[...])
    x = _swiglu_step(x, mod, g2_ref[...], wu_ref, wv_ref, wd_ref, 6)
    if final_norm:
        x = _rms(x) * rest[0][...]
    o_ref[...] = x


def _tail_call(t, mod, g1, wg, wa, wb, wc, wo, oa, ob, oc, g2, wu, wv, wd, ctx_row, skip_tiles=0, final_g=None):
    b, nt, d = t.shape
    n_tiles = nt // TM - skip_tiles
    extra, extra_specs = ([], []) if final_g is None else ([final_g], [_const_spec((1, d))])
    return pl.pallas_call(
        functools.partial(_tail_body, final_norm=final_g is not None),
        grid=(b, n_tiles),
        in_specs=[_tok_spec(d, skip_tiles), _mod_spec(d, ctx_row), _const_spec((1, d)), _const_spec(wg.shape),
                  _const_spec(wa.shape), _const_spec(wb.shape), _const_spec(wc.shape), _const_spec(wo.shape),
                  _tok_spec(oa.shape[2], skip_tiles), _tok_spec(ob.shape[2], skip_tiles),
                  _tok_spec(oc.shape[2], skip_tiles),
                  _const_spec((1, d)), _const_spec(wu.shape), _const_spec(wv.shape), _const_spec(wd.shape)]
        + extra_specs,
        out_specs=_tok_spec(d),
        out_shape=jax.ShapeDtypeStruct((b, n_tiles * TM, d), F32),
        compiler_params=_cparams(2),
        name="tail",
    )(t, mod, g1, wg, wa, wb, wc, wo, oa, ob, oc, g2, wu, wv, wd, *extra)


def _rope_tables(n_lat, nctx):
    rows = n_lat // GRID_W
    row = jnp.repeat(jnp.arange(rows, dtype=F32), GRID_W)
    col = jnp.tile(jnp.arange(GRID_W, dtype=F32), rows)
    n_freq = HEAD // 4
    freqs = jnp.power(ROPE_BASE, -jnp.arange(n_freq, dtype=F32) / n_freq)
    ar = row[:, None] * freqs
    ac = col[:, None] * freqs
    ang = jnp.concatenate([ar, ar, ac, ac], axis=-1)
    ang = jnp.concatenate([ang, ang], axis=-1)
    first = (jnp.arange(LANES) % 32) < 16
    cos, sin = jnp.cos(ang), jnp.sin(ang)
    sin_a = jnp.where(first, -sin, 0.0)
    sin_b = jnp.where(first, 0.0, sin)
    pad = lambda a, v: jnp.concatenate([jnp.full((nctx, LANES), v, F32), a], axis=0)
    return pad(cos, 1.0), pad(sin_a, 0.0), pad(sin_b, 0.0)


def _mixin_weight(w):
    aq, ak, av = w[:, 0:512], w[:, 512:1024], w[:, 1024:1536]
    bq, bk, bv = w[:, 1536:1792], w[:, 1792:2048], w[:, 2048:2560]
    bg = jnp.pad(w[:, 2560:2560 + 2 * GATE_RANK], ((0, 0), (0, LANES - 2 * GATE_RANK)))
    cq, ck, cv = w[:, 3104:3616], w[:, 3616:3744], w[:, 3744:3872]
    return jnp.concatenate([aq, ak, cq, ck, av, bq, bk, bv, bg, cv], axis=1).astype(BF16)


def kernel(x, c, ctx, c_ctx, w_ada, b_ada, norm_g, w_ffn1_in, w_ffn1_out, w_ffn2_in, w_ffn2_out,
           w_mix_in, diff_lambda, diff_subln, gla_gate_w, gla_gate_b, gla_norm, swa_sink,
           w_br_a, w_br_b, w_br_c, w_mix_out, final_g):
    bsz, n, d = x.shape
    nctx = ctx.shape[1]
    depth = w_ada.shape[0]
    ffn = w_ffn1_out.shape[1]
    assert nctx == TM and n % 512 == 0 and d % LANES == 0

    t = (ctx, x)
    ctx_row = bsz
    n_rows = -(-(bsz + 1) // 8) * 8
    cvec = jnp.concatenate([c, c_ctx[None, :], jnp.zeros((n_rows - bsz - 1, d), F32)], axis=0)
    mod_all = _ada_call(cvec, w_ada.astype(BF16), b_ada).reshape(depth, n_rows, N_MOD, d)
    cos, sin_a, sin_b = _rope_tables(n, nctx)
    gate_cols = 2560 + 2 * GATE_RANK
    merge_cols = 3872

    for l in range(depth):
        lam_init = 0.8 - 0.6 * math.exp(-0.3 * l)
        mod = mod_all[l]
        ng = norm_g[l]
        t, aqt, ak, cqt, ck, avt, bq, bk, bv, bg, cvt, bvt = _head_call(
            t, mod, ng[0:1], w_ffn1_in[l, :, :ffn].astype(BF16), w_ffn1_in[l, :, ffn:].astype(BF16),
            w_ffn1_out[l].astype(BF16), ng[1:2], _mixin_weight(w_mix_in[l]), cos, sin_a, sin_b, ctx_row)
        lam_rows = jnp.concatenate([diff_lambda[l], jnp.full((4, HEAD), lam_init, F32)], axis=0)
        oa = _attn_a_call(aqt, ak, avt, lam_rows, diff_subln[l][None, :], nctx)
        gw = jnp.zeros((2, LANES, B_HEADS * B_KDIM), F32)
        for dd in range(2):
            gw = gw.at[dd, dd * GATE_RANK:(dd + 1) * GATE_RANK].set(gla_gate_w[l, dd])
        ob = _gla_call(bq, bk, bv, bvt, bg, gw.astype(BF16), gla_gate_b[l][:, None, :], gla_norm[l][None, :], nctx)
        oc = _attn_c_call(cqt, ck, cvt, swa_sink[l], nctx)
        wg = jnp.concatenate([w_mix_in[l, :, merge_cols:], w_mix_in[l, :, gate_cols:gate_cols + 512]],
                             axis=1).astype(BF16)
        last = l == depth - 1
        tail_row, skip = (None, nctx // TM) if last else (ctx_row, 0)
        t = _tail_call(t, mod, ng[1:2], wg, w_br_a[l].astype(BF16), w_br_b[l].astype(BF16),
                       w_br_c[l].astype(BF16), w_mix_out[l].astype(BF16), oa, ob, oc,
                       ng[2:3], w_ffn2_in[l, :, :ffn].astype(BF16), w_ffn2_in[l, :, ffn:].astype(BF16),
                       w_ffn2_out[l].astype(BF16), tail_row, skip, final_g[None, :] if last else None)
    return t
```

```python
import functools
import math

import jax
import jax.numpy as jnp
from jax import lax
from jax.experimental import pallas as pl
from jax.experimental.pallas import tpu as pltpu

F32 = jnp.float32
BF16 = jnp.bfloat16

HEAD = 64
ROPE_BASE = 10000.0
GRID_W = 64
N_MOD = 9
A_HEADS = 4
A_VROWS = 128 + 16
B_HEADS = 4
B_KDIM = 64
B_VDIM = 128
GATE_RANK = 16
GATE_TAU = 16.0
GLA_CHUNK = 64
C_HEADS = 8
C_KV_HEADS = 2
WINDOW = 128
NEG_INF = -1e30
EPS = 1e-6

TM = 256
LANES = 128
VMEM_LIMIT = 56 * 1024 * 1024

_AQ, _AK, _CQ, _CK, _AV, _BQ, _BK, _BV, _BG, _CV, _MIX_COLS = (
    0, 512, 1024, 1536, 1664, 2176, 2432, 2688, 3200, 3328, 3456)


def _cparams(n_grid, vmem=VMEM_LIMIT):
    return pltpu.CompilerParams(dimension_semantics=("arbitrary",) * n_grid, vmem_limit_bytes=vmem)


def _const_spec(shape):
    nd = len(shape)
    return pl.BlockSpec(shape, lambda *_: (0,) * nd, pipeline_mode=pl.Buffered(1))


def _rms(x):
    return x * lax.rsqrt(jnp.mean(x * x, axis=-1, keepdims=True) + EPS)


def _norm_mod(x, g, shift, scale):
    return _rms(x) * g * (1.0 + scale) + shift


def _dot(a, b):
    return jnp.dot(a, b, preferred_element_type=F32)


def _ada_body(c_ref, w_ref, b_ref, o_ref):
    sc = jax.nn.silu(c_ref[...]).astype(BF16)
    o_ref[...] = _dot(sc, w_ref[...]) + b_ref[...]


def _ada_call(cvec, w_ada, b_ada):
    depth, d, n_out = w_ada.shape
    rows = cvec.shape[0]
    tn = n_out // 4
    return pl.pallas_call(
        _ada_body,
        grid=(depth, n_out // tn),
        in_specs=[pl.BlockSpec((rows, d), lambda l, n: (0, 0)),
                  pl.BlockSpec((None, d, tn), lambda l, n: (l, 0, n)),
                  pl.BlockSpec((None, 1, tn), lambda l, n: (l, 0, n))],
        out_specs=pl.BlockSpec((None, rows, tn), lambda l, n: (l, 0, n)),
        out_shape=jax.ShapeDtypeStruct((depth, rows, n_out), F32),
        compiler_params=_cparams(2),
        name="adaln",
    )(cvec, w_ada, b_ada.reshape(depth, 1, n_out))


def _tok_spec(width, skip=0):
    return pl.BlockSpec((None, TM, width), lambda b, j: (b, j + skip, 0))


def _mod_spec(d, ctx_row):
    if ctx_row is None:
        return pl.BlockSpec((None, N_MOD, d), lambda b, j: (b, 0, 0))
    return pl.BlockSpec((None, N_MOD, d), lambda b, j: (jnp.where(j == 0, ctx_row, b), 0, 0))


def _swiglu_step(x, mod, g, wu_ref, wv_ref, wd_ref, mod0):
    hb = _norm_mod(x, g, mod[mod0:mod0 + 1], mod[mod0 + 1:mod0 + 2]).astype(BF16)
    a = (jax.nn.silu(_dot(hb, wu_ref[...])) * _dot(hb, wv_ref[...])).astype(BF16)
    return x + 0.5 * mod[mod0 + 2:mod0 + 3] * _dot(a, wd_ref[...])


def _rope128(x, cos, sin_a, sin_b):
    return x * cos + pltpu.roll(x, LANES - 16, 1) * sin_a + pltpu.roll(x, 16, 1) * sin_b


def _head_body(*refs, split_input):
    refs = list(refs)
    if split_input:
        ctx_ref, lat_ref = refs[0:2]
        x = jnp.where(pl.program_id(1) == 0, ctx_ref[...], lat_ref[...])
        refs = refs[2:]
    else:
        x = refs[0][...]
        refs = refs[1:]
    (mod_ref, g0_ref, wu_ref, wv_ref, wd_ref, g_ref, w_ref, cos_ref, sa_ref, sb_ref, t_ref,
     aqt_ref, ak_ref, cqt_ref, ck_ref, avt_ref, bq_ref, bk_ref, bv_ref, bg_ref, cvt_ref, bvt_ref) = refs
    mod = mod_ref[...]
    x = _swiglu_step(x, mod, g0_ref[...], wu_ref, wv_ref, wd_ref, 0)
    t_ref[...] = x
    hb = _norm_mod(x, g_ref[...], mod[3:4], mod[4:5]).astype(BF16)
    cos, sin_a, sin_b = cos_ref[...], sa_ref[...], sb_ref[...]
    q_scale = HEAD ** -0.5 * math.log2(math.e)

    def proj(lo, hi):
        return _dot(hb, w_ref[:, lo:hi])

    def roped(p, i):
        return _rope128(p[:, i * LANES:(i + 1) * LANES], cos, sin_a, sin_b)

    p_aq, p_ak, p_cqk, p_av = proj(_AQ, _AK), proj(_AK, _CQ), proj(_CQ, _AV), proj(_AV, _BQ)
    for i in range(4):
        lo = i * LANES
        aqt_ref[lo:lo + LANES, :] = (roped(p_aq, i) * q_scale).T.astype(BF16)
        cqt_ref[lo:lo + LANES, :] = (roped(p_cqk, i) * q_scale).T.astype(BF16)
        ak_ref[:, lo:lo + LANES] = roped(p_ak, i).astype(BF16)
        avt_ref[i * A_VROWS:i * A_VROWS + LANES, :] = p_av[:, lo:lo + LANES].T.astype(BF16)
        avt_ref[i * A_VROWS + LANES:(i + 1) * A_VROWS, :] = jnp.ones((A_VROWS - LANES, TM), BF16)
    ck_ref[...] = roped(p_cqk, 4).astype(BF16)
    bq_ref[...] = proj(_BQ, _BK).astype(BF16)
    bk_ref[...] = proj(_BK, _BV).astype(BF16)
    bv = proj(_BV, _BG)
    bv_ref[...] = bv.astype(BF16)
    bvt_ref[...] = bv.T.astype(BF16)
    p_gc = proj(_BG, _MIX_COLS)
    bg_ref[...] = p_gc[:, 0:_CV - _BG].astype(BF16)
    cvt_ref[...] = p_gc[:, _CV - _BG:].T.astype(BF16)


def _head_call(t, mod, g0, wu, wv, wd, g1, w_mix, cos, sin_a, sin_b, ctx_row):
    split_input = isinstance(t, tuple)
    d, f = wu.shape
    if split_input:
        ctx, lat = t
        b, nt = lat.shape[0], ctx.shape[1] + lat.shape[1]
        acts = [ctx, lat]
        act_specs = [pl.BlockSpec((None, TM, d), lambda bb, j: (bb, 0, 0)),
                     pl.BlockSpec((None, TM, d), lambda bb, j: (bb, jnp.maximum(j - 1, 0), 0))]
    else:
        b, nt = t.shape[0], t.shape[1]
        acts, act_specs = [t], [_tok_spec(d)]
    tbl = pl.BlockSpec((TM, LANES), lambda bb, j: (j, 0))

    def tposed(rows):
        return pl.BlockSpec((None, rows, TM), lambda bb, j: (bb, 0, j))

    def sds(*shape):
        return jax.ShapeDtypeStruct(shape, BF16)

    return pl.pallas_call(
        functools.partial(_head_body, split_input=split_input),
        grid=(b, nt // TM),
        in_specs=act_specs + [_mod_spec(d, ctx_row), _const_spec((1, d)),
                              _const_spec((d, f)), _const_spec((d, f)), _const_spec((f, d)),
                              _const_spec((1, d)), _const_spec((d, _MIX_COLS)), tbl, tbl, tbl],
        out_specs=[_tok_spec(d),
                   tposed(512), _tok_spec(512), tposed(512), _tok_spec(128), tposed(A_HEADS * A_VROWS),
                   _tok_spec(256), _tok_spec(256), _tok_spec(512), _tok_spec(128), tposed(128), tposed(512)],
        out_shape=[jax.ShapeDtypeStruct((b, nt, d), F32),
                   sds(b, 512, nt), sds(b, nt, 512), sds(b, 512, nt), sds(b, nt, 128), sds(b, A_HEADS * A_VROWS, nt),
                   sds(b, nt, 256), sds(b, nt, 256), sds(b, nt, 512), sds(b, nt, 128), sds(b, 128, nt),
                   sds(b, 512, nt)],
        compiler_params=_cparams(2),
        name="head",
    )(*acts, mod, g0, wu, wv, wd, g1, w_mix, cos, sin_a, sin_b)


def _attn_a_body(lam_ref, qt_ref, qn_ref, k_ref, vt_ref, sub_ref, o_ref, s_ref, mx_ref, *, nctx, k_chunk):
    j = pl.program_id(1)
    last = pl.num_programs(1) - 1
    nt = k_ref.shape[0]
    row = lax.broadcasted_iota(jnp.int32, (LANES, 1), 0)
    lp = lam_ref[...]
    lam_init = lp[4:5, 0:1]
    lam = (jnp.exp(jnp.sum(lp[0:1] * lp[1:2], axis=-1, keepdims=True))
           - jnp.exp(jnp.sum(lp[2:3] * lp[3:4], axis=-1, keepdims=True)) + lam_init)
    out_gain = sub_ref[...] * (1.0 - lam_init)
    units = [(h, m) for h in range(A_HEADS) for m in range(2)]
    n_units = len(units)
    all_chunks = [(0, nctx)] + [(lo, k_chunk) for lo in range(nctx, nt, k_chunk)]

    def attend(n_keys, first_ready, prefetch):
        chunks = [c for c in all_chunks if c[0] < n_keys]
        state = {}

        def pass1(u, lo, w):
            nxt = u == n_units
            h, m = units[u % n_units]
            if lo == 0:
                qt = (qn_ref if nxt else qt_ref)[h * LANES:(h + 1) * LANES, :]
                state["q", u] = jnp.where((row < HEAD) if m == 0 else (row >= HEAD), qt, jnp.zeros_like(qt))
            s = _dot(k_ref[lo:lo + w, h * LANES:(h + 1) * LANES], state["q", u])
            s_ref[u % 2, lo:lo + w, :] = s
            pm = jnp.max(s, axis=0, keepdims=True)
            state["mx", u] = pm if lo == 0 else jnp.maximum(state["mx", u], pm)

        def pass2(u, lo, w):
            h, m = units[u]
            e = jnp.exp2(s_ref[u % 2, lo:lo + w, :] - state["mx", u])
            pv = _dot(vt_ref[h * A_VROWS:(h + 1) * A_VROWS, lo:lo + w], e.astype(BF16))
            state["acc", u] = pv if lo == 0 else state["acc", u] + pv

        def finish(u):
            h, m = units[u]
            acc = state.pop(("acc", u))
            res = acc[0:LANES] / acc[LANES:LANES + 1]
            if m == 0:
                state["head", h] = res
            else:
                o = (state.pop(("head", h)) - lam * res).T
                o_ref[:, h * LANES:(h + 1) * LANES] = (_rms(o) * out_gain).astype(BF16)

        if first_ready:
            state["mx", 0] = mx_ref[...]
        else:
            for lo, w in chunks:
                pass1(0, lo, w)
        for u in range(1, n_units):
            for lo, w in chunks:
                pass1(u, lo, w)
                pass2(u - 1, lo, w)
            finish(u - 1)
        if prefetch:
            for lo, w in all_chunks:
                pass1(n_units, lo, w)
                if (lo, w) in chunks:
                    pass2(n_units - 1, lo, w)
            mx_ref[...] = state["mx", n_units]
        else:
            for lo, w in chunks:
                pass2(n_units - 1, lo, w)
        finish(n_units - 1)

    @pl.when(j == 0)
    def _():
        attend(nctx, False, True)

    @pl.when(jnp.logical_and(j > 0, j < last))
    def _():
        attend(nt, True, True)

    @pl.when(j == last)
    def _():
        attend(nt, True, False)


def _attn_a_call(aqt, ak, avt, lam_rows, subln, nctx):
    b, nt, width = ak.shape
    k_chunk = 2048
    n_tiles = nt // TM
    assert (nt - nctx) % k_chunk == 0 and n_tiles >= 2
    return pl.pallas_call(
        functools.partial(_attn_a_body, nctx=nctx, k_chunk=k_chunk),
        grid=(b, n_tiles),
        in_specs=[pl.BlockSpec(lam_rows.shape, lambda bb, j: (0, 0)),
                  pl.BlockSpec((None, width, TM), lambda bb, j: (bb, 0, j)),
                  pl.BlockSpec((None, width, TM), lambda bb, j: (bb, 0, jnp.minimum(j + 1, n_tiles - 1))),
                  pl.BlockSpec((None, nt, width), lambda bb, j: (bb, 0, 0)),
                  pl.BlockSpec((None, avt.shape[1], nt), lambda bb, j: (bb, 0, 0)),
                  pl.BlockSpec((1, LANES), lambda bb, j: (0, 0))],
        out_specs=_tok_spec(width),
        out_shape=jax.ShapeDtypeStruct((b, nt, width), BF16),
        scratch_shapes=[pltpu.VMEM((2, nt, TM), F32), pltpu.VMEM((1, TM), F32)],
        compiler_params=_cparams(2),
        name="attn_a",
    )(lam_rows, aqt, aqt, ak, avt, subln)


def _dot_nt(a, b):
    return lax.dot_general(a, b, (((1,), (1,)), ((), ())), preferred_element_type=F32)


def _gla_body(q_ref, k_ref, v_ref, vt_ref, g_ref, gw_ref, gb_ref, gn_ref, o_ref, rev_ref, st_ref, *, nctx):
    nt = q_ref.shape[0]
    ch = GLA_CHUNK
    per_tile = TM // ch
    n_tiles = nt // TM
    n_ctx_tiles = nctx // TM

    row = lax.broadcasted_iota(jnp.int32, (TM, TM), 0)
    col = lax.broadcasted_iota(jnp.int32, (TM, TM), 1)
    same_chunk = (row // ch) == (col // ch)
    tok_chunk = lax.broadcasted_iota(jnp.int32, (TM, 1), 0) // ch
    srow = lax.broadcasted_iota(jnp.int32, (2 * B_VDIM, 2 * B_KDIM), 0)
    scol = lax.broadcasted_iota(jnp.int32, (2 * B_VDIM, 2 * B_KDIM), 1)
    same_head = (srow < B_VDIM) == (scol < B_KDIM)
    lane = lax.broadcasted_iota(jnp.int32, (1, LANES), 1)

    dirs = (0, 1)
    earlier = [jnp.logical_and(same_chunk, (col <= row) if d == 0 else (col >= row)) for d in dirs]
    tri = [jnp.where(e, 1.0, 0.0).astype(BF16) for e in earlier]
    half = [lane < B_KDIM, lane >= B_KDIM]

    out_refs = (o_ref, rev_ref)

    def advance(work):
        n = range(len(work))
        dr = [d for d, _ in work]
        rows = [pl.ds(t * TM if isinstance(t, int) else pl.multiple_of(t * TM, TM), TM) for _, t in work]
        z = [_dot(g_ref[rows[i], :], gw_ref[dr[i]]) + gb_ref[dr[i]] for i in n]
        la = [(jnp.minimum(x, 0.0) - jnp.log(1.0 + jnp.exp(-jnp.abs(x)))) * (1.0 / GATE_TAU) for x in z]
        hi = [x.astype(BF16) for x in la]
        pieces = [jnp.concatenate([hi[i], (la[i] - hi[i].astype(F32)).astype(BF16)], axis=1) for i in n]
        cum = [_dot(tri[dr[i]], pieces[i]) for i in n]
        cum = [x[:, :LANES] + x[:, LANES:] for x in cum]
        ends = [x.reshape(per_tile, ch, LANES) for x in cum]
        ends = [ends[i][:, ch - 1:ch, :] if dr[i] == 0 else ends[i][:, 0:1, :] for i in n]
        tot = [jnp.broadcast_to(x, (per_tile, ch, LANES)).reshape(TM, LANES) for x in ends]
        qf = [q_ref[rows[i], :].astype(F32) for i in n]
        kf = [k_ref[rows[i], :].astype(F32) for i in n]
        v = [v_ref[rows[i], :] for i in n]
        vt = [vt_ref[:, rows[i]] for i in n]
        qe = [qf[i] * (B_KDIM ** -0.5) * jnp.exp(cum[i]) for i in n]
        ke = [(kf[i] * jnp.exp(-cum[i])).astype(BF16) for i in n]
        kd = [(kf[i] * jnp.exp(tot[i] - cum[i])).astype(BF16) for i in n]
        decay = [jnp.exp(x) for x in tot]
        q_h = [[jnp.where(half[h], qe[i], 0.0).astype(BF16) for h in range(2)] for i in n]
        att = [[jnp.where(earlier[dr[i]], _dot_nt(q_h[i][h], ke[i]), 0.0).astype(BF16) for h in range(2)]
               for i in n]
        intra = [[_dot(att[i][h], v[i][:, h * B_VDIM:(h + 1) * B_VDIM]) for h in range(2)] for i in n]
        kd_c = [[jnp.where(tok_chunk == c, kd[i], jnp.zeros_like(kd[i])) for c in range(per_tile)] for i in n]
        inc2 = [[_dot(vt[i], jnp.concatenate(kd_c[i][c:c + 2], axis=1)) for c in range(0, per_tile, 2)]
                for i in n]
        incs = [[jnp.where(same_head, inc2[i][c // 2][:, (c % 2) * LANES:(c % 2 + 1) * LANES], 0.0)
                 for c in range(per_tile)] for i in n]
        state = [st_ref[d] for d in dirs]
        inter = [[None] * per_tile for _ in n]
        for i in n:
            d = dr[i]
            for c in (range(per_tile) if d == 0 else reversed(range(per_tile))):
                inter[i][c] = _dot_nt(qe[i][c * ch:(c + 1) * ch].astype(BF16), state[d].astype(BF16))
                state[d] = state[d] * decay[i][c * ch:c * ch + 1, :] + incs[i][c]
        for d in dirs:
            st_ref[d] = state[d]
        for i in n:
            out_refs[dr[i]][rows[i], :] = jnp.concatenate(intra[i], axis=1) + jnp.concatenate(inter[i], axis=0)

    st_ref[...] = jnp.zeros(st_ref.shape, F32)
    for i in range(n_ctx_tiles):
        advance([(0, i), (1, n_ctx_tiles - 1 - i)])

    def scan(i, carry):
        t_fwd = n_ctx_tiles + 2 * i
        t_rev = n_tiles - 1 - 2 * i
        advance([(0, t_fwd), (1, t_rev), (0, t_fwd + 1), (1, t_rev - 1)])
        return carry

    lax.fori_loop(0, (n_tiles - n_ctx_tiles) // 2, scan, 0)
    gn = gn_ref[...]

    def combine(t, carry):
        rows = pl.ds(pl.multiple_of(t * TM, TM), TM)
        tot = o_ref[rows, :] + rev_ref[rows, :]
        for h in range(2):
            lo = h * B_VDIM
            o_ref[rows, lo:lo + B_VDIM] = _rms(tot[:, lo:lo + B_VDIM]) * gn
        return carry

    lax.fori_loop(0, n_tiles, combine, 0)


def _gla_call(bq, bk, bv, bvt, bg, gw, gb, gn, nctx):
    b, nt, _ = bq.shape
    assert nctx % TM == 0 and (nt - nctx) % (2 * TM) == 0
    return pl.pallas_call(
        functools.partial(_gla_body, nctx=nctx),
        grid=(b, B_HEADS // 2),
        in_specs=[pl.BlockSpec((None, nt, LANES), lambda bb, p: (bb, 0, p)),
                  pl.BlockSpec((None, nt, LANES), lambda bb, p: (bb, 0, p)),
                  pl.BlockSpec((None, nt, 2 * B_VDIM), lambda bb, p: (bb, 0, p)),
                  pl.BlockSpec((None, 2 * B_VDIM, nt), lambda bb, p: (bb, p, 0)),
                  pl.BlockSpec((None, nt, LANES), lambda bb, p: (bb, 0, 0)),
                  pl.BlockSpec((2, LANES, LANES), lambda bb, p: (0, 0, p)),
                  pl.BlockSpec((2, 1, LANES), lambda bb, p: (0, 0, p)),
                  pl.BlockSpec((1, B_VDIM), lambda bb, p: (0, 0))],
        out_specs=pl.BlockSpec((None, nt, 2 * B_VDIM), lambda bb, p: (bb, 0, p)),
        out_shape=jax.ShapeDtypeStruct((b, nt, B_HEADS * B_VDIM), F32),
        scratch_shapes=[pltpu.VMEM((nt, 2 * B_VDIM), F32), pltpu.VMEM((2, 2 * B_VDIM, 2 * B_KDIM), F32)],
        compiler_params=_cparams(2),
        name="gla",
    )(bq, bk, bv, bvt, bg, gw, gb, gn)


def _attn_c_body(sink_ref, qt_ref, qn_ref, k_ref, vt_ref, o_ref, s_ref, mx_ref, *, nctx, span, n_tiles):
    j = pl.program_id(1)
    last = pl.num_programs(1) - 1
    nlat = k_ref.shape[0] - nctx
    per_kv = C_HEADS // C_KV_HEADS
    log2e = math.log2(math.e)
    groups = range(C_KV_HEADS)
    ck = TM
    sink = [jnp.concatenate([jnp.full((1, TM), sink_ref[g * per_kv + i] * log2e, F32)
                             for i in range(per_kv)], axis=1) for g in groups]

    def chunks_of(tile):
        if isinstance(tile, int) and tile == 0:
            return [(0, None)]
        q0 = (tile - 1) * TM
        k0 = jnp.clip(q0 - WINDOW, 0, nlat - span)
        start = nctx + k0
        kpos = k0 + lax.broadcasted_iota(jnp.int32, (span, TM), 0)
        qpos = q0 + lax.broadcasted_iota(jnp.int32, (span, TM), 1)
        valid = jnp.abs(qpos - kpos) <= WINDOW
        return [(0, None)] + [(pl.multiple_of(start + off, LANES), valid[off:off + ck])
                              for off in range(0, span, ck)]

    def pass1_ops(tile, q_ref, slot):
        chunks = chunks_of(tile)
        state = {}

        def q_stack(g):
            blocks = []
            for i in range(per_kv):
                h = g * per_kv + i
                qh = q_ref[h * HEAD:(h + 1) * HEAD, :]
                zero = jnp.zeros_like(qh)
                blocks.append(jnp.concatenate([qh, zero] if g == 0 else [zero, qh], axis=0))
            return jnp.concatenate(blocks, axis=1)

        def op(g, c):
            if c == 0:
                state["q", g] = q_stack(g)
            row0, ok = chunks[c]
            s = _dot(k_ref[pl.ds(row0, ck), :], state["q", g])
            if ok is not None:
                s = jnp.where(jnp.concatenate([ok] * per_kv, axis=1), s, NEG_INF)
            s_ref[slot, g, c * ck:(c + 1) * ck, :] = s
            pm = jnp.max(s, axis=0, keepdims=True)
            state["mx", g] = jnp.maximum(sink[g], pm) if c == 0 else jnp.maximum(state["mx", g], pm)
            if c == len(chunks) - 1:
                mx_ref[slot, g] = state["mx", g]

        return [functools.partial(op, g, c) for c in range(len(chunks)) for g in groups]

    def pass2_ops(tile, slot):
        chunks = chunks_of(tile)
        state = {}

        def op(g, c):
            if c == 0:
                state["mx", g] = mx_ref[slot, g]
            e = jnp.exp2(s_ref[slot, g, c * ck:(c + 1) * ck, :] - state["mx", g]).astype(BF16)
            vt = jnp.concatenate([vt_ref[g * HEAD:(g + 1) * HEAD, pl.ds(chunks[c][0], ck)],
                                  jnp.ones((8, ck), BF16)], axis=0)
            pv = _dot(vt, e)
            state["acc", g] = pv if c == 0 else state["acc", g] + pv

        def finish():
            outs = []
            for g in groups:
                acc = state["acc", g]
                o = acc[0:HEAD] / (acc[HEAD:HEAD + 1] + jnp.exp2(sink[g] - state["mx", g]))
                outs += [o[:, i * TM:(i + 1) * TM] for i in range(per_kv)]
            o_ref[...] = jnp.concatenate(outs, axis=0).T.astype(BF16)

        return [functools.partial(op, g, c) for c in range(len(chunks)) for g in groups], finish

    def run(first_ops, second_ops):
        for i in range(max(len(first_ops), len(second_ops))):
            for ops in (first_ops, second_ops):
                if i < len(ops):
                    ops[i]()

    @pl.when(j == 0)
    def _():
        run(pass1_ops(0, qt_ref, 0), [])
        p2, finish = pass2_ops(0, 0)
        run(pass1_ops(1, qn_ref, 1), p2)
        finish()

    for slot in range(2):
        @pl.when(jnp.logical_and(jnp.logical_and(j > 0, j < last), j % 2 == slot))
        def _():
            p2, finish = pass2_ops(j, slot)
            run(pass1_ops(j + 1, qn_ref, 1 - slot), p2)
            finish()

    @pl.when(j == last)
    def _():
        p2, finish = pass2_ops(j, (n_tiles - 1) % 2)
        run([], p2)
        finish()


def _attn_c_call(cqt, ck, cvt, sink, nctx):
    b, nt, kv_width = ck.shape
    width = cqt.shape[1]
    span = TM + 2 * WINDOW
    n_tiles = nt // TM
    assert nt - nctx >= span and nctx == TM and n_tiles >= 2
    lanes = (C_HEADS // C_KV_HEADS) * TM
    return pl.pallas_call(
        functools.partial(_attn_c_body, nctx=nctx, span=span, n_tiles=n_tiles),
        grid=(b, n_tiles),
        in_specs=[pl.BlockSpec(memory_space=pltpu.SMEM),
                  pl.BlockSpec((None, width, TM), lambda bb, j: (bb, 0, j)),
                  pl.BlockSpec((None, width, TM), lambda bb, j: (bb, 0, jnp.minimum(j + 1, n_tiles - 1))),
                  pl.BlockSpec((None, nt, kv_width), lambda bb, j: (bb, 0, 0)),
                  pl.BlockSpec((None, kv_width, nt), lambda bb, j: (bb, 0, 0))],
        out_specs=_tok_spec(width),
        out_shape=jax.ShapeDtypeStruct((b, nt, width), BF16),
        scratch_shapes=[pltpu.VMEM((2, C_KV_HEADS, nctx + span, lanes), F32),
                        pltpu.VMEM((2, C_KV_HEADS, 1, lanes), F32)],
        compiler_params=_cparams(2),
        name="attn_c",
    )(sink, cqt, cqt, ck, cvt)


def _tail_body(x_ref, mod_ref, g1_ref, wg_ref, wa_ref, wb_ref, wc_ref, wo_ref, oa_ref, ob_ref, oc_ref,
               g2_ref, wu_ref, wv_ref, wd_ref, *rest, final_norm):
    o_ref = rest[-1]
    x = x_ref[...]
    d = x.shape[1]
    mod = mod_ref[...]
    hb = _norm_mod(x, g1_ref[...], mod[3:4], mod[4:5]).astype(BF16)
    r = _dot(hb, wg_ref[:, 3 * d:])
    ob = (ob_ref[...] * jax.nn.silu(r)).astype(BF16)
    y = jax.nn.sigmoid(_dot(hb, wg_ref[:, 0:d])) * _dot(oa_ref[...], wa_ref[...])
    y = y + jax.nn.sigmoid(_dot(hb, wg_ref[:, d:2 * d])) * _dot(ob, wb_ref[...])
    y = y + jax.nn.sigmoid(_dot(hb, wg_ref[:, 2 * d:3 * d])) * _dot(oc_ref[...], wc_ref[...])
    x = x + mod[5:6] * _dot(y.astype(BF16), wo_ref[...])
    x = _swiglu_step(x, mod, g2_ref[...], wu_ref, wv_ref, wd_ref, 6)
    if final_norm:
        x = _rms(x) * rest[0][...]
    o_ref[...] = x


def _tail_call(t, mod, g1, wg, wa, wb, wc, wo, oa, ob, oc, g2, wu, wv, wd, ctx_row, skip_tiles=0, final_g=None):
    b, nt, d = t.shape
    n_tiles = nt // TM - skip_tiles
    extra, extra_specs = ([], []) if final_g is None else ([final_g], [_const_spec((1, d))])
    return pl.pallas_call(
        functools.partial(_tail_body, final_norm=final_g is not None),
        grid=(b, n_tiles),
        in_specs=[_tok_spec(d, skip_tiles), _mod_spec(d, ctx_row), _const_spec((1, d)), _const_spec(wg.shape),
                  _const_spec(wa.shape), _const_spec(wb.shape), _const_spec(wc.shape), _const_spec(wo.shape),
                  _tok_spec(oa.shape[2], skip_tiles), _tok_spec(ob.shape[2], skip_tiles),
                  _tok_spec(oc.shape[2], skip_tiles),
                  _const_spec((1, d)), _const_spec(wu.shape), _const_spec(wv.shape), _const_spec(wd.shape)]
        + extra_specs,
        out_specs=_tok_spec(d),
        out_shape=jax.ShapeDtypeStruct((b, n_tiles * TM, d), F32),
        compiler_params=_cparams(2),
        name="tail",
    )(t, mod, g1, wg, wa, wb, wc, wo, oa, ob, oc, g2, wu, wv, wd, *extra)


def _rope_tables(n_lat, nctx):
    rows = n_lat // GRID_W
    row = jnp.repeat(jnp.arange(rows, dtype=F32), GRID_W)
    col = jnp.tile(jnp.arange(GRID_W, dtype=F32), rows)
    n_freq = HEAD // 4
    freqs = jnp.power(ROPE_BASE, -jnp.arange(n_freq, dtype=F32) / n_freq)
    ar = row[:, None] * freqs
    ac = col[:, None] * freqs
    ang = jnp.concatenate([ar, ar, ac, ac], axis=-1)
    ang = jnp.concatenate([ang, ang], axis=-1)
    first = (jnp.arange(LANES) % 32) < 16
    cos, sin = jnp.cos(ang), jnp.sin(ang)
    sin_a = jnp.where(first, -sin, 0.0)
    sin_b = jnp.where(first, 0.0, sin)
    pad = lambda a, v: jnp.concatenate([jnp.full((nctx, LANES), v, F32), a], axis=0)
    return pad(cos, 1.0), pad(sin_a, 0.0), pad(sin_b, 0.0)


def _mixin_weight(w):
    aq, ak, av = w[:, 0:512], w[:, 512:1024], w[:, 1024:1536]
    bq, bk, bv = w[:, 1536:1792], w[:, 1792:2048], w[:, 2048:2560]
    bg = jnp.pad(w[:, 2560:2560 + 2 * GATE_RANK], ((0, 0), (0, LANES - 2 * GATE_RANK)))
    cq, ck, cv = w[:, 3104:3616], w[:, 3616:3744], w[:, 3744:3872]
    return jnp.concatenate([aq, ak, cq, ck, av, bq, bk, bv, bg, cv], axis=1).astype(BF16)


def kernel(x, c, ctx, c_ctx, w_ada, b_ada, norm_g, w_ffn1_in, w_ffn1_out, w_ffn2_in, w_ffn2_out,
           w_mix_in, diff_lambda, diff_subln, gla_gate_w, gla_gate_b, gla_norm, swa_sink,
           w_br_a, w_br_b, w_br_c, w_mix_out, final_g):
    bsz, n, d = x.shape
    nctx = ctx.shape[1]
    depth = w_ada.shape[0]
    ffn = w_ffn1_out.shape[1]
    assert nctx == TM and n % 512 == 0 and d % LANES == 0

    t = (ctx, x)
    ctx_row = bsz
    n_rows = -(-(bsz + 1) // 8) * 8
    cvec = jnp.concatenate([c, c_ctx[None, :], jnp.zeros((n_rows - bsz - 1, d), F32)], axis=0)
    mod_all = _ada_call(cvec, w_ada.astype(BF16), b_ada).reshape(depth, n_rows, N_MOD, d)
    cos, sin_a, sin_b = _rope_tables(n, nctx)
    gate_cols = 2560 + 2 * GATE_RANK
    merge_cols = 3872

    for l in range(depth):
        lam_init = 0.8 - 0.6 * math.exp(-0.3 * l)
        mod = mod_all[l]
        ng = norm_g[l]
        t, aqt, ak, cqt, ck, avt, bq, bk, bv, bg, cvt, bvt = _head_call(
            t, mod, ng[0:1], w_ffn1_in[l, :, :ffn].astype(BF16), w_ffn1_in[l, :, ffn:].astype(BF16),
            w_ffn1_out[l].astype(BF16), ng[1:2], _mixin_weight(w_mix_in[l]), cos, sin_a, sin_b, ctx_row)
        lam_rows = jnp.concatenate([diff_lambda[l], jnp.full((4, HEAD), lam_init, F32)], axis=0)
        oa = _attn_a_call(aqt, ak, avt, lam_rows, diff_subln[l][None, :], nctx)
        gw = jnp.zeros((2, LANES, B_HEADS * B_KDIM), F32)
        for dd in range(2):
            gw = gw.at[dd, dd * GATE_RANK:(dd + 1) * GATE_RANK].set(gla_gate_w[l, dd])
        ob = _gla_call(bq, bk, bv, bvt, bg, gw.astype(BF16), gla_gate_b[l][:, None, :], gla_norm[l][None, :], nctx)
        oc = _attn_c_call(cqt, ck, cvt, swa_sink[l], nctx)
        wg = jnp.concatenate([w_mix_in[l, :, merge_cols:], w_mix_in[l, :, gate_cols:gate_cols + 512]],
                             axis=1).astype(BF16)
        last = l == depth - 1
        tail_row, skip = (None, nctx // TM) if last else (ctx_row, 0)
        t = _tail_call(t, mod, ng[1:2], wg, w_br_a[l].astype(BF16), w_br_b[l].astype(BF16),
                       w_br_c[l].astype(BF16), w_mix_out[l].astype(BF16), oa, ob, oc,
                       ng[2:3], w_ffn2_in[l, :, :ffn].astype(BF16), w_ffn2_in[l, :, ffn:].astype(BF16),
                       w_ffn2_out[l].astype(BF16), tail_row, skip, final_g[None, :] if last else None)
    return t
```

```python
import functools
import math

import jax
import jax.numpy as jnp
from jax import lax
from jax.experimental import pallas as pl
from jax.experimental.pallas import tpu as pltpu

F32 = jnp.float32
BF16 = jnp.bfloat16

HEAD = 64
ROPE_BASE = 10000.0
GRID_W = 64
N_MOD = 9
A_HEADS = 4
A_VROWS = 128 + 16
B_HEADS = 4
B_KDIM = 64
B_VDIM = 128
GATE_RANK = 16
GATE_TAU = 16.0
GLA_CHUNK = 64
C_HEADS = 8
C_KV_HEADS = 2
WINDOW = 128
NEG_INF = -1e30
EPS = 1e-6

TM = 256
LANES = 128
VMEM_LIMIT = 56 * 1024 * 1024

_AQ, _AK, _CQ, _CK, _AV, _BQ, _BK, _BV, _BG, _CV, _MIX_COLS = (
    0, 512, 1024, 1536, 1664, 2176, 2432, 2688, 3200, 3328, 3456)


def _cparams(n_grid, vmem=VMEM_LIMIT):
    return pltpu.CompilerParams(dimension_semantics=("arbitrary",) * n_grid, vmem_limit_bytes=vmem)


def _const_spec(shape):
    nd = len(shape)
    return pl.BlockSpec(shape, lambda *_: (0,) * nd, pipeline_mode=pl.Buffered(1))


def _rms(x):
    return x * lax.rsqrt(jnp.mean(x * x, axis=-1, keepdims=True) + EPS)


def _norm_mod(x, g, shift, scale):
    return _rms(x) * g * (1.0 + scale) + shift


def _dot(a, b):
    return jnp.dot(a, b, preferred_element_type=F32)


def _ada_body(c_ref, w_ref, b_ref, o_ref):
    sc = jax.nn.silu(c_ref[...]).astype(BF16)
    o_ref[...] = _dot(sc, w_ref[...]) + b_ref[...]


def _ada_call(cvec, w_ada, b_ada):
    depth, d, n_out = w_ada.shape
    rows = cvec.shape[0]
    tn = n_out // 4
    return pl.pallas_call(
        _ada_body,
        grid=(depth, n_out // tn),
        in_specs=[pl.BlockSpec((rows, d), lambda l, n: (0, 0)),
                  pl.BlockSpec((None, d, tn), lambda l, n: (l, 0, n)),
                  pl.BlockSpec((None, 1, tn), lambda l, n: (l, 0, n))],
        out_specs=pl.BlockSpec((None, rows, tn), lambda l, n: (l, 0, n)),
        out_shape=jax.ShapeDtypeStruct((depth, rows, n_out), F32),
        compiler_params=_cparams(2),
        name="adaln",
    )(cvec, w_ada, b_ada.reshape(depth, 1, n_out))


def _tok_spec(width, skip=0):
    return pl.BlockSpec((None, TM, width), lambda b, j: (b, j + skip, 0))


def _mod_spec(d, ctx_row):
    if ctx_row is None:
        return pl.BlockSpec((None, N_MOD, d), lambda b, j: (b, 0, 0))
    return pl.BlockSpec((None, N_MOD, d), lambda b, j: (jnp.where(j == 0, ctx_row, b), 0, 0))


def _swiglu_step(x, mod, g, wu_ref, wv_ref, wd_ref, mod0):
    hb = _norm_mod(x, g, mod[mod0:mod0 + 1], mod[mod0 + 1:mod0 + 2]).astype(BF16)
    a = (jax.nn.silu(_dot(hb, wu_ref[...])) * _dot(hb, wv_ref[...])).astype(BF16)
    return x + 0.5 * mod[mod0 + 2:mod0 + 3] * _dot(a, wd_ref[...])


def _rope128(x, cos, sin_a, sin_b):
    return x * cos + pltpu.roll(x, LANES - 16, 1) * sin_a + pltpu.roll(x, 16, 1) * sin_b


def _head_body(*refs, split_input):
    refs = list(refs)
    if split_input:
        ctx_ref, lat_ref = refs[0:2]
        x = jnp.where(pl.program_id(1) == 0, ctx_ref[...], lat_ref[...])
        refs = refs[2:]
    else:
        x = refs[0][...]
        refs = refs[1:]
    (mod_ref, g0_ref, wu_ref, wv_ref, wd_ref, g_ref, w_ref, cos_ref, sa_ref, sb_ref, t_ref,
     aqt_ref, ak_ref, cqt_ref, ck_ref, avt_ref, bq_ref, bk_ref, bv_ref, bg_ref, cvt_ref, bvt_ref) = refs
    mod = mod_ref[...]
    x = _swiglu_step(x, mod, g0_ref[...], wu_ref, wv_ref, wd_ref, 0)
    t_ref[...] = x
    hb = _norm_mod(x, g_ref[...], mod[3:4], mod[4:5]).astype(BF16)
    cos, sin_a, sin_b = cos_ref[...], sa_ref[...], sb_ref[...]
    q_scale = HEAD ** -0.5 * math.log2(math.e)

    def proj(lo, hi):
        return _dot(hb, w_ref[:, lo:hi])

    def roped(p, i):
        return _rope128(p[:, i * LANES:(i + 1) * LANES], cos, sin_a, sin_b)

    p_aq, p_ak, p_cqk, p_av = proj(_AQ, _AK), proj(_AK, _CQ), proj(_CQ, _AV), proj(_AV, _BQ)
    for i in range(4):
        lo = i * LANES
        aqt_ref[lo:lo + LANES, :] = (roped(p_aq, i) * q_scale).T.astype(BF16)
        cqt_ref[lo:lo + LANES, :] = (roped(p_cqk, i) * q_scale).T.astype(BF16)
        ak_ref[:, lo:lo + LANES] = roped(p_ak, i).astype(BF16)
        avt_ref[i * A_VROWS:i * A_VROWS + LANES, :] = p_av[:, lo:lo + LANES].T.astype(BF16)
        avt_ref[i * A_VROWS + LANES:(i + 1) * A_VROWS, :] = jnp.ones((A_VROWS - LANES, TM), BF16)
    ck_ref[...] = roped(p_cqk, 4).astype(BF16)
    bq_ref[...] = proj(_BQ, _BK).astype(BF16)
    bk_ref[...] = proj(_BK, _BV).astype(BF16)
    bv = proj(_BV, _BG)
    bv_ref[...] = bv.astype(BF16)
    bvt_ref[...] = bv.T.astype(BF16)
    p_gc = proj(_BG, _MIX_COLS)
    bg_ref[...] = p_gc[:, 0:_CV - _BG].astype(BF16)
    cvt_ref[...] = p_gc[:, _CV - _BG:].T.astype(BF16)


def _head_call(t, mod, g0, wu, wv, wd, g1, w_mix, cos, sin_a, sin_b, ctx_row):
    split_input = isinstance(t, tuple)
    d, f = wu.shape
    if split_input:
        ctx, lat = t
        b, nt = lat.shape[0], ctx.shape[1] + lat.shape[1]
        acts = [ctx, lat]
        act_specs = [pl.BlockSpec((None, TM, d), lambda bb, j: (bb, 0, 0)),
                     pl.BlockSpec((None, TM, d), lambda bb, j: (bb, jnp.maximum(j - 1, 0), 0))]
    else:
        b, nt = t.shape[0], t.shape[1]
        acts, act_specs = [t], [_tok_spec(d)]
    tbl = pl.BlockSpec((TM, LANES), lambda bb, j: (j, 0))

    def tposed(rows):
        return pl.BlockSpec((None, rows, TM), lambda bb, j: (bb, 0, j))

    def sds(*shape):
        return jax.ShapeDtypeStruct(shape, BF16)

    return pl.pallas_call(
        functools.partial(_head_body, split_input=split_input),
        grid=(b, nt // TM),
        in_specs=act_specs + [_mod_spec(d, ctx_row), _const_spec((1, d)),
                              _const_spec((d, f)), _const_spec((d, f)), _const_spec((f, d)),
                              _const_spec((1, d)), _const_spec((d, _MIX_COLS)), tbl, tbl, tbl],
        out_specs=[_tok_spec(d),
                   tposed(512), _tok_spec(512), tposed(512), _tok_spec(128), tposed(A_HEADS * A_VROWS),
                   _tok_spec(256), _tok_spec(256), _tok_spec(512), _tok_spec(128), tposed(128), tposed(512)],
        out_shape=[jax.ShapeDtypeStruct((b, nt, d), F32),
                   sds(b, 512, nt), sds(b, nt, 512), sds(b, 512, nt), sds(b, nt, 128), sds(b, A_HEADS * A_VROWS, nt),
                   sds(b, nt, 256), sds(b, nt, 256), sds(b, nt, 512), sds(b, nt, 128), sds(b, 128, nt),
                   sds(b, 512, nt)],
        compiler_params=_cparams(2),
        name="head",
    )(*acts, mod, g0, wu, wv, wd, g1, w_mix, cos, sin_a, sin_b)


def _attn_a_body(lam_ref, qt_ref, qn_ref, k_ref, vt_ref, sub_ref, o_ref, s_ref, mx_ref, *, nctx, k_chunk):
    j = pl.program_id(1)
    last = pl.num_programs(1) - 1
    nt = k_ref.shape[0]
    row = lax.broadcasted_iota(jnp.int32, (LANES, 1), 0)
    lp = lam_ref[...]
    lam_init = lp[4:5, 0:1]
    lam = (jnp.exp(jnp.sum(lp[0:1] * lp[1:2], axis=-1, keepdims=True))
           - jnp.exp(jnp.sum(lp[2:3] * lp[3:4], axis=-1, keepdims=True)) + lam_init)
    out_gain = sub_ref[...] * (1.0 - lam_init)
    units = [(h, m) for h in range(A_HEADS) for m in range(2)]
    n_units = len(units)
    all_chunks = [(0, nctx)] + [(lo, k_chunk) for lo in range(nctx, nt, k_chunk)]

    def attend(n_keys, first_ready, prefetch):
        chunks = [c for c in all_chunks if c[0] < n_keys]
        state = {}

        def pass1(u, lo, w):
            nxt = u == n_units
            h, m = units[u % n_units]
            if lo == 0:
                qt = (qn_ref if nxt else qt_ref)[h * LANES:(h + 1) * LANES, :]
                state["q", u] = jnp.where((row < HEAD) if m == 0 else (row >= HEAD), qt, jnp.zeros_like(qt))
            s = _dot(k_ref[lo:lo + w, h * LANES:(h + 1) * LANES], state["q", u])
            s_ref[u % 2, lo:lo + w, :] = s
            pm = jnp.max(s, axis=0, keepdims=True)
            state["mx", u] = pm if lo == 0 else jnp.maximum(state["mx", u], pm)

        def pass2(u, lo, w):
            h, m = units[u]
            e = jnp.exp2(s_ref[u % 2, lo:lo + w, :] - state["mx", u])
            pv = _dot(vt_ref[h * A_VROWS:(h + 1) * A_VROWS, lo:lo + w], e.astype(BF16))
            state["acc", u] = pv if lo == 0 else state["acc", u] + pv

        def finish(u):
            h, m = units[u]
            acc = state.pop(("acc", u))
            res = acc[0:LANES] / acc[LANES:LANES + 1]
            if m == 0:
                state["head", h] = res
            else:
                o = (state.pop(("head", h)) - lam * res).T
                o_ref[:, h * LANES:(h + 1) * LANES] = (_rms(o) * out_gain).astype(BF16)

        if first_ready:
            state["mx", 0] = mx_ref[...]
        else:
            for lo, w in chunks:
                pass1(0, lo, w)
        for u in range(1, n_units):
            for lo, w in chunks:
                pass1(u, lo, w)
                pass2(u - 1, lo, w)
            finish(u - 1)
        if prefetch:
            for lo, w in all_chunks:
                pass1(n_units, lo, w)
                if (lo, w) in chunks:
                    pass2(n_units - 1, lo, w)
            mx_ref[...] = state["mx", n_units]
        else:
            for lo, w in chunks:
                pass2(n_units - 1, lo, w)
        finish(n_units - 1)

    @pl.when(j == 0)
    def _():
        attend(nctx, False, True)

    @pl.when(jnp.logical_and(j > 0, j < last))
    def _():
        attend(nt, True, True)

    @pl.when(j == last)
    def _():
        attend(nt, True, False)


def _attn_a_call(aqt, ak, avt, lam_rows, subln, nctx):
    b, nt, width = ak.shape
    k_chunk = 2048
    n_tiles = nt // TM
    assert (nt - nctx) % k_chunk == 0 and n_tiles >= 2
    return pl.pallas_call(
        functools.partial(_attn_a_body, nctx=nctx, k_chunk=k_chunk),
        grid=(b, n_tiles),
        in_specs=[pl.BlockSpec(lam_rows.shape, lambda bb, j: (0, 0)),
                  pl.BlockSpec((None, width, TM), lambda bb, j: (bb, 0, j)),
                  pl.BlockSpec((None, width, TM), lambda bb, j: (bb, 0, jnp.minimum(j + 1, n_tiles - 1))),
                  pl.BlockSpec((None, nt, width), lambda bb, j: (bb, 0, 0)),
                  pl.BlockSpec((None, avt.shape[1], nt), lambda bb, j: (bb, 0, 0)),
                  pl.BlockSpec((1, LANES), lambda bb, j: (0, 0))],
        out_specs=_tok_spec(width),
        out_shape=jax.ShapeDtypeStruct((b, nt, width), BF16),
        scratch_shapes=[pltpu.VMEM((2, nt, TM), F32), pltpu.VMEM((1, TM), F32)],
        compiler_params=_cparams(2),
        name="attn_a",
    )(lam_rows, aqt, aqt, ak, avt, subln)


def _dot_nt(a, b):
    return lax.dot_general(a, b, (((1,), (1,)), ((), ())), preferred_element_type=F32)


def _gla_body(q_ref, k_ref, v_ref, vt_ref, g_ref, gw_ref, gb_ref, gn_ref, o_ref, rev_ref, st_ref, *, nctx):
    nt = q_ref.shape[0]
    ch = GLA_CHUNK
    per_tile = TM // ch
    n_tiles = nt // TM
    n_ctx_tiles = nctx // TM

    row = lax.broadcasted_iota(jnp.int32, (TM, TM), 0)
    col = lax.broadcasted_iota(jnp.int32, (TM, TM), 1)
    same_chunk = (row // ch) == (col // ch)
    tok_chunk = lax.broadcasted_iota(jnp.int32, (TM, 1), 0) // ch
    srow = lax.broadcasted_iota(jnp.int32, (2 * B_VDIM, 2 * B_KDIM), 0)
    scol = lax.broadcasted_iota(jnp.int32, (2 * B_VDIM, 2 * B_KDIM), 1)
    same_head = (srow < B_VDIM) == (scol < B_KDIM)
    lane = lax.broadcasted_iota(jnp.int32, (1, LANES), 1)

    dirs = (0, 1)
    earlier = [jnp.logical_and(same_chunk, (col <= row) if d == 0 else (col >= row)) for d in dirs]
    tri = [jnp.where(e, 1.0, 0.0).astype(BF16) for e in earlier]
    half = [lane < B_KDIM, lane >= B_KDIM]

    out_refs = (o_ref, rev_ref)

    def advance(work):
        n = range(len(work))
        dr = [d for d, _ in work]
        rows = [pl.ds(t * TM if isinstance(t, int) else pl.multiple_of(t * TM, TM), TM) for _, t in work]
        z = [_dot(g_ref[rows[i], :], gw_ref[dr[i]]) + gb_ref[dr[i]] for i in n]
        la = [(jnp.minimum(x, 0.0) - jnp.log(1.0 + jnp.exp(-jnp.abs(x)))) * (1.0 / GATE_TAU) for x in z]
        hi = [x.astype(BF16) for x in la]
        pieces = [jnp.concatenate([hi[i], (la[i] - hi[i].astype(F32)).astype(BF16)], axis=1) for i in n]
        cum = [_dot(tri[dr[i]], pieces[i]) for i in n]
        cum = [x[:, :LANES] + x[:, LANES:] for x in cum]
        ends = [x.reshape(per_tile, ch, LANES) for x in cum]
        ends = [ends[i][:, ch - 1:ch, :] if dr[i] == 0 else ends[i][:, 0:1, :] for i in n]
        tot = [jnp.broadcast_to(x, (per_tile, ch, LANES)).reshape(TM, LANES) for x in ends]
        qf = [q_ref[rows[i], :].astype(F32) for i in n]
        kf = [k_ref[rows[i], :].astype(F32) for i in n]
        v = [v_ref[rows[i], :] for i in n]
        vt = [vt_ref[:, rows[i]] for i in n]
        qe = [qf[i] * (B_KDIM ** -0.5) * jnp.exp(cum[i]) for i in n]
        ke = [(kf[i] * jnp.exp(-cum[i])).astype(BF16) for i in n]
        kd = [(kf[i] * jnp.exp(tot[i] - cum[i])).astype(BF16) for i in n]
        decay = [jnp.exp(x) for x in tot]
        q_h = [[jnp.where(half[h], qe[i], 0.0).astype(BF16) for h in range(2)] for i in n]
        att = [[jnp.where(earlier[dr[i]], _dot_nt(q_h[i][h], ke[i]), 0.0).astype(BF16) for h in range(2)]
               for i in n]
        intra = [[_dot(att[i][h], v[i][:, h * B_VDIM:(h + 1) * B_VDIM]) for h in range(2)] for i in n]
        kd_c = [[jnp.where(tok_chunk == c, kd[i], jnp.zeros_like(kd[i])) for c in range(per_tile)] for i in n]
        inc2 = [[_dot(vt[i], jnp.concatenate(kd_c[i][c:c + 2], axis=1)) for c in range(0, per_tile, 2)]
                for i in n]
        incs = [[jnp.where(same_head, inc2[i][c // 2][:, (c % 2) * LANES:(c % 2 + 1) * LANES], 0.0)
                 for c in range(per_tile)] for i in n]
        state = [st_ref[d] for d in dirs]
        inter = [[None] * per_tile for _ in n]
        for i in n:
            d = dr[i]
            for c in (range(per_tile) if d == 0 else reversed(range(per_tile))):
                inter[i][c] = _dot_nt(qe[i][c * ch:(c + 1) * ch].astype(BF16), state[d].astype(BF16))
                state[d] = state[d] * decay[i][c * ch:c * ch + 1, :] + incs[i][c]
        for d in dirs:
            st_ref[d] = state[d]
        for i in n:
            out_refs[dr[i]][rows[i], :] = jnp.concatenate(intra[i], axis=1) + jnp.concatenate(inter[i], axis=0)

    st_ref[...] = jnp.zeros(st_ref.shape, F32)
    for i in range(n_ctx_tiles):
        advance([(0, i), (1, n_ctx_tiles - 1 - i)])

    def scan(i, carry):
        t_fwd = n_ctx_tiles + 2 * i
        t_rev = n_tiles - 1 - 2 * i
        advance([(0, t_fwd), (1, t_rev), (0, t_fwd + 1), (1, t_rev - 1)])
        return carry

    lax.fori_loop(0, (n_tiles - n_ctx_tiles) // 2, scan, 0)
    gn = gn_ref[...]

    def combine(t, carry):
        rows = pl.ds(pl.multiple_of(t * TM, TM), TM)
        tot = o_ref[rows, :] + rev_ref[rows, :]
        for h in range(2):
            lo = h * B_VDIM
            o_ref[rows, lo:lo + B_VDIM] = _rms(tot[:, lo:lo + B_VDIM]) * gn
        return carry

    lax.fori_loop(0, n_tiles, combine, 0, unroll=True)


def _gla_call(bq, bk, bv, bvt, bg, gw, gb, gn, nctx):
    b, nt, _ = bq.shape
    assert nctx % TM == 0 and (nt - nctx) % (2 * TM) == 0
    return pl.pallas_call(
        functools.partial(_gla_body, nctx=nctx),
        grid=(b, B_HEADS // 2),
        in_specs=[pl.BlockSpec((None, nt, LANES), lambda bb, p: (bb, 0, p)),
                  pl.BlockSpec((None, nt, LANES), lambda bb, p: (bb, 0, p)),
                  pl.BlockSpec((None, nt, 2 * B_VDIM), lambda bb, p: (bb, 0, p)),
                  pl.BlockSpec((None, 2 * B_VDIM, nt), lambda bb, p: (bb, p, 0)),
                  pl.BlockSpec((None, nt, LANES), lambda bb, p: (bb, 0, 0)),
                  pl.BlockSpec((2, LANES, LANES), lambda bb, p: (0, 0, p)),
                  pl.BlockSpec((2, 1, LANES), lambda bb, p: (0, 0, p)),
                  pl.BlockSpec((1, B_VDIM), lambda bb, p: (0, 0))],
        out_specs=pl.BlockSpec((None, nt, 2 * B_VDIM), lambda bb, p: (bb, 0, p)),
        out_shape=jax.ShapeDtypeStruct((b, nt, B_HEADS * B_VDIM), F32),
        scratch_shapes=[pltpu.VMEM((nt, 2 * B_VDIM), F32), pltpu.VMEM((2, 2 * B_VDIM, 2 * B_KDIM), F32)],
        compiler_params=_cparams(2),
        name="gla",
    )(bq, bk, bv, bvt, bg, gw, gb, gn)


def _attn_c_body(sink_ref, qt_ref, qn_ref, k_ref, vt_ref, o_ref, s_ref, mx_ref, *, nctx, span, n_tiles):
    j = pl.program_id(1)
    last = pl.num_programs(1) - 1
    nlat = k_ref.shape[0] - nctx
    per_kv = C_HEADS // C_KV_HEADS
    log2e = math.log2(math.e)
    groups = range(C_KV_HEADS)
    ck = TM
    sink = [jnp.concatenate([jnp.full((1, TM), sink_ref[g * per_kv + i] * log2e, F32)
                             for i in range(per_kv)], axis=1) for g in groups]

    def chunks_of(tile):
        if isinstance(tile, int) and tile == 0:
            return [(0, None)]
        q0 = (tile - 1) * TM
        k0 = jnp.clip(q0 - WINDOW, 0, nlat - span)
        start = nctx + k0
        kpos = k0 + lax.broadcasted_iota(jnp.int32, (span, TM), 0)
        qpos = q0 + lax.broadcasted_iota(jnp.int32, (span, TM), 1)
        valid = jnp.abs(qpos - kpos) <= WINDOW
        return [(0, None)] + [(pl.multiple_of(start + off, LANES), valid[off:off + ck])
                              for off in range(0, span, ck)]

    def pass1_ops(tile, q_ref, slot):
        chunks = chunks_of(tile)
        state = {}

        def q_stack(g):
            blocks = []
            for i in range(per_kv):
                h = g * per_kv + i
                qh = q_ref[h * HEAD:(h + 1) * HEAD, :]
                zero = jnp.zeros_like(qh)
                blocks.append(jnp.concatenate([qh, zero] if g == 0 else [zero, qh], axis=0))
            return jnp.concatenate(blocks, axis=1)

        def op(g, c):
            if c == 0:
                state["q", g] = q_stack(g)
            row0, ok = chunks[c]
            s = _dot(k_ref[pl.ds(row0, ck), :], state["q", g])
            if ok is not None:
                s = jnp.where(jnp.concatenate([ok] * per_kv, axis=1), s, NEG_INF)
            s_ref[slot, g, c * ck:(c + 1) * ck, :] = s
            pm = jnp.max(s, axis=0, keepdims=True)
            state["mx", g] = jnp.maximum(sink[g], pm) if c == 0 else jnp.maximum(state["mx", g], pm)
            if c == len(chunks) - 1:
                mx_ref[slot, g] = state["mx", g]

        return [functools.partial(op, g, c) for c in range(len(chunks)) for g in groups]

    def pass2_ops(tile, slot):
        chunks = chunks_of(tile)
        state = {}

        def op(g, c):
            if c == 0:
                state["mx", g] = mx_ref[slot, g]
            e = jnp.exp2(s_ref[slot, g, c * ck:(c + 1) * ck, :] - state["mx", g]).astype(BF16)
            vt = jnp.concatenate([vt_ref[g * HEAD:(g + 1) * HEAD, pl.ds(chunks[c][0], ck)],
                                  jnp.ones((8, ck), BF16)], axis=0)
            pv = _dot(vt, e)
            state["acc", g] = pv if c == 0 else state["acc", g] + pv

        def finish():
            outs = []
            for g in groups:
                acc = state["acc", g]
                o = acc[0:HEAD] / (acc[HEAD:HEAD + 1] + jnp.exp2(sink[g] - state["mx", g]))
                outs += [o[:, i * TM:(i + 1) * TM] for i in range(per_kv)]
            o_ref[...] = jnp.concatenate(outs, axis=0).T.astype(BF16)

        return [functools.partial(op, g, c) for c in range(len(chunks)) for g in groups], finish

    def run(first_ops, second_ops):
        for i in range(max(len(first_ops), len(second_ops))):
            for ops in (first_ops, second_ops):
                if i < len(ops):
                    ops[i]()

    @pl.when(j == 0)
    def _():
        run(pass1_ops(0, qt_ref, 0), [])
        p2, finish = pass2_ops(0, 0)
        run(pass1_ops(1, qn_ref, 1), p2)
        finish()

    for slot in range(2):
        @pl.when(jnp.logical_and(jnp.logical_and(j > 0, j < last), j % 2 == slot))
        def _():
            p2, finish = pass2_ops(j, slot)
            run(pass1_ops(j + 1, qn_ref, 1 - slot), p2)
            finish()

    @pl.when(j == last)
    def _():
        p2, finish = pass2_ops(j, (n_tiles - 1) % 2)
        run([], p2)
        finish()


def _attn_c_call(cqt, ck, cvt, sink, nctx):
    b, nt, kv_width = ck.shape
    width = cqt.shape[1]
    span = TM + 2 * WINDOW
    n_tiles = nt // TM
    assert nt - nctx >= span and nctx == TM and n_tiles >= 2
    lanes = (C_HEADS // C_KV_HEADS) * TM
    return pl.pallas_call(
        functools.partial(_attn_c_body, nctx=nctx, span=span, n_tiles=n_tiles),
        grid=(b, n_tiles),
        in_specs=[pl.BlockSpec(memory_space=pltpu.SMEM),
                  pl.BlockSpec((None, width, TM), lambda bb, j: (bb, 0, j)),
                  pl.BlockSpec((None, width, TM), lambda bb, j: (bb, 0, jnp.minimum(j + 1, n_tiles - 1))),
                  pl.BlockSpec((None, nt, kv_width), lambda bb, j: (bb, 0, 0)),
                  pl.BlockSpec((None, kv_width, nt), lambda bb, j: (bb, 0, 0))],
        out_specs=_tok_spec(width),
        out_shape=jax.ShapeDtypeStruct((b, nt, width), BF16),
        scratch_shapes=[pltpu.VMEM((2, C_KV_HEADS, nctx + span, lanes), F32),
                        pltpu.VMEM((2, C_KV_HEADS, 1, lanes), F32)],
        compiler_params=_cparams(2),
        name="attn_c",
    )(sink, cqt, cqt, ck, cvt)


def _tail_body(x_ref, mod_ref, g1_ref, wg_ref, wa_ref, wb_ref, wc_ref, wo_ref, oa_ref, ob_ref, oc_ref,
               g2_ref, wu_ref, wv_ref, wd_ref, *rest, final_norm):
    o_ref = rest[-1]
    x = x_ref[...]
    d = x.shape[1]
    mod = mod_ref[...]
    hb = _norm_mod(x, g1_ref[...], mod[3:4], mod[4:5]).astype(BF16)
    r = _dot(hb, wg_ref[:, 3 * d:])
    ob = (ob_ref[...] * jax.nn.silu(r)).astype(BF16)
    y = jax.nn.sigmoid(_dot(hb, wg_ref[:, 0:d])) * _dot(oa_ref[...], wa_ref[...])
    y = y + jax.nn.sigmoid(_dot(hb, wg_ref[:, d:2 * d])) * _dot(ob, wb_ref[...])
    y = y + jax.nn.sigmoid(_dot(hb, wg_ref[:, 2 * d:3 * d])) * _dot(oc_ref[...], wc_ref[...])
    x = x + mod[5:6] * _dot(y.astype(BF16), wo_ref[...])
    x = _swiglu_step(x, mod, g2_ref[...], wu_ref, wv_ref, wd_ref, 6)
    if final_norm:
        x = _rms(x) * rest[0][...]
    o_ref[...] = x


def _tail_call(t, mod, g1, wg, wa, wb, wc, wo, oa, ob, oc, g2, wu, wv, wd, ctx_row, skip_tiles=0, final_g=None):
    b, nt, d = t.shape
    n_tiles = nt // TM - skip_tiles
    extra, extra_specs = ([], []) if final_g is None else ([final_g], [_const_spec((1, d))])
    return pl.pallas_call(
        functools.partial(_tail_body, final_norm=final_g is not None),
        grid=(b, n_tiles),
        in_specs=[_tok_spec(d, skip_tiles), _mod_spec(d, ctx_row), _const_spec((1, d)), _const_spec(wg.shape),
                  _const_spec(wa.shape), _const_spec(wb.shape), _const_spec(wc.shape), _const_spec(wo.shape),
                  _tok_spec(oa.shape[2], skip_tiles), _tok_spec(ob.shape[2], skip_tiles),
                  _tok_spec(oc.shape[2], skip_tiles),
                  _const_spec((1, d)), _const_spec(wu.shape), _const_spec(wv.shape), _const_spec(wd.shape)]
        + extra_specs,
        out_specs=_tok_spec(d),
        out_shape=jax.ShapeDtypeStruct((b, n_tiles * TM, d), F32),
        compiler_params=_cparams(2),
        name="tail",
    )(t, mod, g1, wg, wa, wb, wc, wo, oa, ob, oc, g2, wu, wv, wd, *extra)


def _rope_tables(n_lat, nctx):
    rows = n_lat // GRID_W
    row = jnp.repeat(jnp.arange(rows, dtype=F32), GRID_W)
    col = jnp.tile(jnp.arange(GRID_W, dtype=F32), rows)
    n_freq = HEAD // 4
    freqs = jnp.power(ROPE_BASE, -jnp.arange(n_freq, dtype=F32) / n_freq)
    ar = row[:, None] * freqs
    ac = col[:, None] * freqs
    ang = jnp.concatenate([ar, ar, ac, ac], axis=-1)
    ang = jnp.concatenate([ang, ang], axis=-1)
    first = (jnp.arange(LANES) % 32) < 16
    cos, sin = jnp.cos(ang), jnp.sin(ang)
    sin_a = jnp.where(first, -sin, 0.0)
    sin_b = jnp.where(first, 0.0, sin)
    pad = lambda a, v: jnp.concatenate([jnp.full((nctx, LANES), v, F32), a], axis=0)
    return pad(cos, 1.0), pad(sin_a, 0.0), pad(sin_b, 0.0)


def _mixin_weight(w):
    aq, ak, av = w[:, 0:512], w[:, 512:1024], w[:, 1024:1536]
    bq, bk, bv = w[:, 1536:1792], w[:, 1792:2048], w[:, 2048:2560]
    bg = jnp.pad(w[:, 2560:2560 + 2 * GATE_RANK], ((0, 0), (0, LANES - 2 * GATE_RANK)))
    cq, ck, cv = w[:, 3104:3616], w[:, 3616:3744], w[:, 3744:3872]
    return jnp.concatenate([aq, ak, cq, ck, av, bq, bk, bv, bg, cv], axis=1).astype(BF16)


def kernel(x, c, ctx, c_ctx, w_ada, b_ada, norm_g, w_ffn1_in, w_ffn1_out, w_ffn2_in, w_ffn2_out,
           w_mix_in, diff_lambda, diff_subln, gla_gate_w, gla_gate_b, gla_norm, swa_sink,
           w_br_a, w_br_b, w_br_c, w_mix_out, final_g):
    bsz, n, d = x.shape
    nctx = ctx.shape[1]
    depth = w_ada.shape[0]
    ffn = w_ffn1_out.shape[1]
    assert nctx == TM and n % 512 == 0 and d % LANES == 0

    t = (ctx, x)
    ctx_row = bsz
    n_rows = -(-(bsz + 1) // 8) * 8
    cvec = jnp.concatenate([c, c_ctx[None, :], jnp.zeros((n_rows - bsz - 1, d), F32)], axis=0)
    mod_all = _ada_call(cvec, w_ada.astype(BF16), b_ada).reshape(depth, n_rows, N_MOD, d)
    cos, sin_a, sin_b = _rope_tables(n, nctx)
    gate_cols = 2560 + 2 * GATE_RANK
    merge_cols = 3872

    for l in range(depth):
        lam_init = 0.8 - 0.6 * math.exp(-0.3 * l)
        mod = mod_all[l]
        ng = norm_g[l]
        t, aqt, ak, cqt, ck, avt, bq, bk, bv, bg, cvt, bvt = _head_call(
            t, mod, ng[0:1], w_ffn1_in[l, :, :ffn].astype(BF16), w_ffn1_in[l, :, ffn:].astype(BF16),
            w_ffn1_out[l].astype(BF16), ng[1:2], _mixin_weight(w_mix_in[l]), cos, sin_a, sin_b, ctx_row)
        lam_rows = jnp.concatenate([diff_lambda[l], jnp.full((4, HEAD), lam_init, F32)], axis=0)
        oa = _attn_a_call(aqt, ak, avt, lam_rows, diff_subln[l][None, :], nctx)
        gw = jnp.zeros((2, LANES, B_HEADS * B_KDIM), F32)
        for dd in range(2):
            gw = gw.at[dd, dd * GATE_RANK:(dd + 1) * GATE_RANK].set(gla_gate_w[l, dd])
        ob = _gla_call(bq, bk, bv, bvt, bg, gw.astype(BF16), gla_gate_b[l][:, None, :], gla_norm[l][None, :], nctx)
        oc = _attn_c_call(cqt, ck, cvt, swa_sink[l], nctx)
        wg = jnp.concatenate([w_mix_in[l, :, merge_cols:], w_mix_in[l, :, gate_cols:gate_cols + 512]],
                             axis=1).astype(BF16)
        last = l == depth - 1
        tail_row, skip = (None, nctx // TM) if last else (ctx_row, 0)
        t = _tail_call(t, mod, ng[1:2], wg, w_br_a[l].astype(BF16), w_br_b[l].astype(BF16),
                       w_br_c[l].astype(BF16), w_mix_out[l].astype(BF16), oa, ob, oc,
                       ng[2:3], w_ffn2_in[l, :, :ffn].astype(BF16), w_ffn2_in[l, :, ffn:].astype(BF16),
                       w_ffn2_out[l].astype(BF16), tail_row, skip, final_g[None, :] if last else None)
    return t
```

```python
import functools
import math

import jax
import jax.numpy as jnp
from jax import lax
from jax.experimental import pallas as pl
from jax.experimental.pallas import tpu as pltpu

F32 = jnp.float32
BF16 = jnp.bfloat16

HEAD = 64
ROPE_BASE = 10000.0
GRID_W = 64
N_MOD = 9
A_HEADS = 4
A_VROWS = 128 + 16
B_HEADS = 4
B_KDIM = 64
B_VDIM = 128
GATE_RANK = 16
GATE_TAU = 16.0
GLA_CHUNK = 64
C_HEADS = 8
C_KV_HEADS = 2
WINDOW = 128
NEG_INF = -1e30
EPS = 1e-6

TM = 256
LANES = 128
VMEM_LIMIT = 56 * 1024 * 1024

_AQ, _AK, _CQ, _CK, _AV, _BQ, _BK, _BV, _BG, _CV, _MIX_COLS = (
    0, 512, 1024, 1536, 1664, 2176, 2432, 2688, 3200, 3328, 3456)


def _cparams(n_grid, vmem=VMEM_LIMIT):
    return pltpu.CompilerParams(dimension_semantics=("arbitrary",) * n_grid, vmem_limit_bytes=vmem)


def _const_spec(shape):
    nd = len(shape)
    return pl.BlockSpec(shape, lambda *_: (0,) * nd, pipeline_mode=pl.Buffered(1))


def _rms(x):
    return x * lax.rsqrt(jnp.mean(x * x, axis=-1, keepdims=True) + EPS)


def _norm_mod(x, g, shift, scale):
    return _rms(x) * g * (1.0 + scale) + shift


def _dot(a, b):
    return jnp.dot(a, b, preferred_element_type=F32)


def _ada_body(c_ref, w_ref, b_ref, o_ref):
    sc = jax.nn.silu(c_ref[...]).astype(BF16)
    o_ref[...] = _dot(sc, w_ref[...]) + b_ref[...]


def _ada_call(cvec, w_ada, b_ada):
    depth, d, n_out = w_ada.shape
    rows = cvec.shape[0]
    tn = n_out // 4
    return pl.pallas_call(
        _ada_body,
        grid=(depth, n_out // tn),
        in_specs=[pl.BlockSpec((rows, d), lambda l, n: (0, 0)),
                  pl.BlockSpec((None, d, tn), lambda l, n: (l, 0, n)),
                  pl.BlockSpec((None, 1, tn), lambda l, n: (l, 0, n))],
        out_specs=pl.BlockSpec((None, rows, tn), lambda l, n: (l, 0, n)),
        out_shape=jax.ShapeDtypeStruct((depth, rows, n_out), F32),
        compiler_params=_cparams(2),
        name="adaln",
    )(cvec, w_ada, b_ada.reshape(depth, 1, n_out))


def _tok_spec(width, skip=0):
    return pl.BlockSpec((None, TM, width), lambda b, j: (b, j + skip, 0))


def _mod_spec(d, ctx_row):
    if ctx_row is None:
        return pl.BlockSpec((None, N_MOD, d), lambda b, j: (b, 0, 0))
    return pl.BlockSpec((None, N_MOD, d), lambda b, j: (jnp.where(j == 0, ctx_row, b), 0, 0))


def _swiglu_step(x, mod, g, wu_ref, wv_ref, wd_ref, mod0):
    hb = _norm_mod(x, g, mod[mod0:mod0 + 1], mod[mod0 + 1:mod0 + 2]).astype(BF16)
    a = (jax.nn.silu(_dot(hb, wu_ref[...])) * _dot(hb, wv_ref[...])).astype(BF16)
    return x + 0.5 * mod[mod0 + 2:mod0 + 3] * _dot(a, wd_ref[...])


def _rope128(x, cos, sin_a, sin_b):
    return x * cos + pltpu.roll(x, LANES - 16, 1) * sin_a + pltpu.roll(x, 16, 1) * sin_b


def _head_body(*refs, split_input):
    refs = list(refs)
    if split_input:
        ctx_ref, lat_ref = refs[0:2]
        x = jnp.where(pl.program_id(1) == 0, ctx_ref[...], lat_ref[...])
        refs = refs[2:]
    else:
        x = refs[0][...]
        refs = refs[1:]
    (mod_ref, g0_ref, wu_ref, wv_ref, wd_ref, g_ref, w_ref, cos_ref, sa_ref, sb_ref, t_ref,
     aqt_ref, ak_ref, cqt_ref, ck_ref, avt_ref, bq_ref, bk_ref, bv_ref, bg_ref, cvt_ref, bvt_ref) = refs
    mod = mod_ref[...]
    x = _swiglu_step(x, mod, g0_ref[...], wu_ref, wv_ref, wd_ref, 0)
    t_ref[...] = x
    hb = _norm_mod(x, g_ref[...], mod[3:4], mod[4:5]).astype(BF16)
    cos, sin_a, sin_b = cos_ref[...], sa_ref[...], sb_ref[...]
    q_scale = HEAD ** -0.5 * math.log2(math.e)

    def proj(lo, hi):
        return _dot(hb, w_ref[:, lo:hi])

    def roped(p, i):
        return _rope128(p[:, i * LANES:(i + 1) * LANES], cos, sin_a, sin_b)

    p_aq, p_ak, p_cqk, p_av = proj(_AQ, _AK), proj(_AK, _CQ), proj(_CQ, _AV), proj(_AV, _BQ)
    for i in range(4):
        lo = i * LANES
        aqt_ref[lo:lo + LANES, :] = (roped(p_aq, i) * q_scale).T.astype(BF16)
        cqt_ref[lo:lo + LANES, :] = (roped(p_cqk, i) * q_scale).T.astype(BF16)
        ak_ref[:, lo:lo + LANES] = roped(p_ak, i).astype(BF16)
        avt_ref[i * A_VROWS:i * A_VROWS + LANES, :] = p_av[:, lo:lo + LANES].T.astype(BF16)
        avt_ref[i * A_VROWS + LANES:(i + 1) * A_VROWS, :] = jnp.ones((A_VROWS - LANES, TM), BF16)
    ck_ref[...] = roped(p_cqk, 4).astype(BF16)
    bq_ref[...] = proj(_BQ, _BK).astype(BF16)
    bk_ref[...] = proj(_BK, _BV).astype(BF16)
    bv = proj(_BV, _BG)
    bv_ref[...] = bv.astype(BF16)
    bvt_ref[...] = bv.T.astype(BF16)
    p_gc = proj(_BG, _MIX_COLS)
    bg_ref[...] = p_gc[:, 0:_CV - _BG].astype(BF16)
    cvt_ref[...] = p_gc[:, _CV - _BG:].T.astype(BF16)


def _head_call(t, mod, g0, wu, wv, wd, g1, w_mix, cos, sin_a, sin_b, ctx_row):
    split_input = isinstance(t, tuple)
    d, f = wu.shape
    if split_input:
        ctx, lat = t
        b, nt = lat.shape[0], ctx.shape[1] + lat.shape[1]
        acts = [ctx, lat]
        act_specs = [pl.BlockSpec((None, TM, d), lambda bb, j: (bb, 0, 0)),
                     pl.BlockSpec((None, TM, d), lambda bb, j: (bb, jnp.maximum(j - 1, 0), 0))]
    else:
        b, nt = t.shape[0], t.shape[1]
        acts, act_specs = [t], [_tok_spec(d)]
    tbl = pl.BlockSpec((TM, LANES), lambda bb, j: (j, 0))

    def tposed(rows):
        return pl.BlockSpec((None, rows, TM), lambda bb, j: (bb, 0, j))

    def sds(*shape):
        return jax.ShapeDtypeStruct(shape, BF16)

    return pl.pallas_call(
        functools.partial(_head_body, split_input=split_input),
        grid=(b, nt // TM),
        in_specs=act_specs + [_mod_spec(d, ctx_row), _const_spec((1, d)),
                              _const_spec((d, f)), _const_spec((d, f)), _const_spec((f, d)),
                              _const_spec((1, d)), _const_spec((d, _MIX_COLS)), tbl, tbl, tbl],
        out_specs=[_tok_spec(d),
                   tposed(512), _tok_spec(512), tposed(512), _tok_spec(128), tposed(A_HEADS * A_VROWS),
                   _tok_spec(256), _tok_spec(256), _tok_spec(512), _tok_spec(128), tposed(128), tposed(512)],
        out_shape=[jax.ShapeDtypeStruct((b, nt, d), F32),
                   sds(b, 512, nt), sds(b, nt, 512), sds(b, 512, nt), sds(b, nt, 128), sds(b, A_HEADS * A_VROWS, nt),
                   sds(b, nt, 256), sds(b, nt, 256), sds(b, nt, 512), sds(b, nt, 128), sds(b, 128, nt),
                   sds(b, 512, nt)],
        compiler_params=_cparams(2),
        name="head",
    )(*acts, mod, g0, wu, wv, wd, g1, w_mix, cos, sin_a, sin_b)


def _attn_a_body(lam_ref, qt_ref, qn_ref, k_hbm, vt_hbm, sub_ref, o_ref, s_ref, mx_ref, k_buf, vt_buf, sem,
                 *, nctx, k_chunk):
    bi = pl.program_id(0)
    j = pl.program_id(1)
    last = pl.num_programs(1) - 1
    slot = bi % 2

    def kv_copies(batch, dst):
        return (pltpu.make_async_copy(k_hbm.at[batch], k_buf.at[dst], sem.at[0, dst]),
                pltpu.make_async_copy(vt_hbm.at[batch], vt_buf.at[dst], sem.at[1, dst]))

    @pl.when(jnp.logical_and(bi == 0, j == 0))
    def _():
        for cp in kv_copies(0, 0):
            cp.start()

    @pl.when(j == 0)
    def _():
        for cp in kv_copies(bi, slot):
            cp.wait()

    @pl.when(jnp.logical_and(j == 1, bi + 1 < pl.num_programs(0)))
    def _():
        for cp in kv_copies(bi + 1, 1 - slot):
            cp.start()

    k_ref = k_buf.at[slot]
    vt_ref = vt_buf.at[slot]
    nt = k_ref.shape[0]
    row = lax.broadcasted_iota(jnp.int32, (LANES, 1), 0)
    lp = lam_ref[...]
    lam_init = lp[4:5, 0:1]
    lam = (jnp.exp(jnp.sum(lp[0:1] * lp[1:2], axis=-1, keepdims=True))
           - jnp.exp(jnp.sum(lp[2:3] * lp[3:4], axis=-1, keepdims=True)) + lam_init)
    out_gain = sub_ref[...] * (1.0 - lam_init)
    units = [(h, m) for h in range(A_HEADS) for m in range(2)]
    n_units = len(units)
    all_chunks = [(0, nctx)] + [(lo, k_chunk) for lo in range(nctx, nt, k_chunk)]

    def attend(n_keys, first_ready, prefetch):
        chunks = [c for c in all_chunks if c[0] < n_keys]
        state = {}

        def pass1(u, lo, w):
            nxt = u == n_units
            h, m = units[u % n_units]
            if lo == 0:
                qt = (qn_ref if nxt else qt_ref)[h * LANES:(h + 1) * LANES, :]
                state["q", u] = jnp.where((row < HEAD) if m == 0 else (row >= HEAD), qt, jnp.zeros_like(qt))
            s = _dot(k_ref[lo:lo + w, h * LANES:(h + 1) * LANES], state["q", u])
            s_ref[u % 2, lo:lo + w, :] = s
            pm = jnp.max(s, axis=0, keepdims=True)
            state["mx", u] = pm if lo == 0 else jnp.maximum(state["mx", u], pm)

        def pass2(u, lo, w):
            h, m = units[u]
            e = jnp.exp2(s_ref[u % 2, lo:lo + w, :] - state["mx", u])
            pv = _dot(vt_ref[h * A_VROWS:(h + 1) * A_VROWS, lo:lo + w], e.astype(BF16))
            state["acc", u] = pv if lo == 0 else state["acc", u] + pv

        def finish(u):
            h, m = units[u]
            acc = state.pop(("acc", u))
            res = acc[0:LANES] / acc[LANES:LANES + 1]
            if m == 0:
                state["head", h] = res
            else:
                o = (state.pop(("head", h)) - lam * res).T
                o_ref[:, h * LANES:(h + 1) * LANES] = (_rms(o) * out_gain).astype(BF16)

        if first_ready:
            state["mx", 0] = mx_ref[...]
        else:
            for lo, w in chunks:
                pass1(0, lo, w)
        for u in range(1, n_units):
            for lo, w in chunks:
                pass1(u, lo, w)
                pass2(u - 1, lo, w)
            finish(u - 1)
        if prefetch:
            for lo, w in all_chunks:
                pass1(n_units, lo, w)
                if (lo, w) in chunks:
                    pass2(n_units - 1, lo, w)
            mx_ref[...] = state["mx", n_units]
        else:
            for lo, w in chunks:
                pass2(n_units - 1, lo, w)
        finish(n_units - 1)

    @pl.when(j == 0)
    def _():
        attend(nctx, False, True)

    @pl.when(jnp.logical_and(j > 0, j < last))
    def _():
        attend(nt, True, True)

    @pl.when(j == last)
    def _():
        attend(nt, True, False)


def _attn_a_call(aqt, ak, avt, lam_rows, subln, nctx):
    b, nt, width = ak.shape
    k_chunk = 2048
    n_tiles = nt // TM
    assert (nt - nctx) % k_chunk == 0 and n_tiles >= 2
    return pl.pallas_call(
        functools.partial(_attn_a_body, nctx=nctx, k_chunk=k_chunk),
        grid=(b, n_tiles),
        in_specs=[pl.BlockSpec(lam_rows.shape, lambda bb, j: (0, 0)),
                  pl.BlockSpec((None, width, TM), lambda bb, j: (bb, 0, j)),
                  pl.BlockSpec((None, width, TM), lambda bb, j: (bb, 0, jnp.minimum(j + 1, n_tiles - 1))),
                  pl.BlockSpec(memory_space=pl.ANY),
                  pl.BlockSpec(memory_space=pl.ANY),
                  pl.BlockSpec((1, LANES), lambda bb, j: (0, 0))],
        out_specs=_tok_spec(width),
        out_shape=jax.ShapeDtypeStruct((b, nt, width), BF16),
        scratch_shapes=[pltpu.VMEM((2, nt, TM), F32), pltpu.VMEM((1, TM), F32),
                        pltpu.VMEM((2, nt, width), BF16), pltpu.VMEM((2, avt.shape[1], nt), BF16),
                        pltpu.SemaphoreType.DMA((2, 2))],
        compiler_params=_cparams(2),
        name="attn_a",
    )(lam_rows, aqt, aqt, ak, avt, subln)


def _dot_nt(a, b):
    return lax.dot_general(a, b, (((1,), (1,)), ((), ())), preferred_element_type=F32)


def _gla_body(q_ref, k_ref, v_ref, vt_ref, g_ref, gw_ref, gb_ref, gn_ref, o_ref, rev_ref, st_ref, *, nctx):
    nt = q_ref.shape[0]
    ch = GLA_CHUNK
    per_tile = TM // ch
    n_tiles = nt // TM
    n_ctx_tiles = nctx // TM

    row = lax.broadcasted_iota(jnp.int32, (TM, TM), 0)
    col = lax.broadcasted_iota(jnp.int32, (TM, TM), 1)
    same_chunk = (row // ch) == (col // ch)
    tok_chunk = lax.broadcasted_iota(jnp.int32, (TM, 1), 0) // ch
    srow = lax.broadcasted_iota(jnp.int32, (2 * B_VDIM, 2 * B_KDIM), 0)
    scol = lax.broadcasted_iota(jnp.int32, (2 * B_VDIM, 2 * B_KDIM), 1)
    same_head = (srow < B_VDIM) == (scol < B_KDIM)
    lane = lax.broadcasted_iota(jnp.int32, (1, LANES), 1)

    dirs = (0, 1)
    earlier = [jnp.logical_and(same_chunk, (col <= row) if d == 0 else (col >= row)) for d in dirs]
    tri = [jnp.where(e, 1.0, 0.0).astype(BF16) for e in earlier]
    half = [lane < B_KDIM, lane >= B_KDIM]

    out_refs = (o_ref, rev_ref)

    def advance(work):
        n = range(len(work))
        dr = [d for d, _ in work]
        rows = [pl.ds(t * TM if isinstance(t, int) else pl.multiple_of(t * TM, TM), TM) for _, t in work]
        z = [_dot(g_ref[rows[i], :], gw_ref[dr[i]]) + gb_ref[dr[i]] for i in n]
        la = [(jnp.minimum(x, 0.0) - jnp.log(1.0 + jnp.exp(-jnp.abs(x)))) * (1.0 / GATE_TAU) for x in z]
        hi = [x.astype(BF16) for x in la]
        pieces = [jnp.concatenate([hi[i], (la[i] - hi[i].astype(F32)).astype(BF16)], axis=1) for i in n]
        cum = [_dot(tri[dr[i]], pieces[i]) for i in n]
        cum = [x[:, :LANES] + x[:, LANES:] for x in cum]
        ends = [x.reshape(per_tile, ch, LANES) for x in cum]
        ends = [ends[i][:, ch - 1:ch, :] if dr[i] == 0 else ends[i][:, 0:1, :] for i in n]
        tot = [jnp.broadcast_to(x, (per_tile, ch, LANES)).reshape(TM, LANES) for x in ends]
        qf = [q_ref[rows[i], :].astype(F32) for i in n]
        kf = [k_ref[rows[i], :].astype(F32) for i in n]
        v = [v_ref[rows[i], :] for i in n]
        vt = [vt_ref[:, rows[i]] for i in n]
        qe = [qf[i] * (B_KDIM ** -0.5) * jnp.exp(cum[i]) for i in n]
        ke = [(kf[i] * jnp.exp(-cum[i])).astype(BF16) for i in n]
        kd = [(kf[i] * jnp.exp(tot[i] - cum[i])).astype(BF16) for i in n]
        decay = [jnp.exp(x) for x in tot]
        q_h = [[jnp.where(half[h], qe[i], 0.0).astype(BF16) for h in range(2)] for i in n]
        att = [[jnp.where(earlier[dr[i]], _dot_nt(q_h[i][h], ke[i]), 0.0).astype(BF16) for h in range(2)]
               for i in n]
        intra = [[_dot(att[i][h], v[i][:, h * B_VDIM:(h + 1) * B_VDIM]) for h in range(2)] for i in n]
        kd_c = [[jnp.where(tok_chunk == c, kd[i], jnp.zeros_like(kd[i])) for c in range(per_tile)] for i in n]
        inc2 = [[_dot(vt[i], jnp.concatenate(kd_c[i][c:c + 2], axis=1)) for c in range(0, per_tile, 2)]
                for i in n]
        incs = [[jnp.where(same_head, inc2[i][c // 2][:, (c % 2) * LANES:(c % 2 + 1) * LANES], 0.0)
                 for c in range(per_tile)] for i in n]
        state = [st_ref[d] for d in dirs]
        inter = [[None] * per_tile for _ in n]
        for i in n:
            d = dr[i]
            for c in (range(per_tile) if d == 0 else reversed(range(per_tile))):
                inter[i][c] = _dot_nt(qe[i][c * ch:(c + 1) * ch].astype(BF16), state[d].astype(BF16))
                state[d] = state[d] * decay[i][c * ch:c * ch + 1, :] + incs[i][c]
        for d in dirs:
            st_ref[d] = state[d]
        for i in n:
            out_refs[dr[i]][rows[i], :] = jnp.concatenate(intra[i], axis=1) + jnp.concatenate(inter[i], axis=0)

    st_ref[...] = jnp.zeros(st_ref.shape, F32)
    for i in range(n_ctx_tiles):
        advance([(0, i), (1, n_ctx_tiles - 1 - i)])

    def scan(i, carry):
        t_fwd = n_ctx_tiles + 2 * i
        t_rev = n_tiles - 1 - 2 * i
        advance([(0, t_fwd), (1, t_rev), (0, t_fwd + 1), (1, t_rev - 1)])
        return carry

    lax.fori_loop(0, (n_tiles - n_ctx_tiles) // 2, scan, 0)
    gn = gn_ref[...]

    def combine(t, carry):
        rows = pl.ds(pl.multiple_of(t * TM, TM), TM)
        tot = o_ref[rows, :] + rev_ref[rows, :]
        for h in range(2):
            lo = h * B_VDIM
            o_ref[rows, lo:lo + B_VDIM] = _rms(tot[:, lo:lo + B_VDIM]) * gn
        return carry

    lax.fori_loop(0, n_tiles, combine, 0, unroll=True)


def _gla_call(bq, bk, bv, bvt, bg, gw, gb, gn, nctx):
    b, nt, _ = bq.shape
    assert nctx % TM == 0 and (nt - nctx) % (2 * TM) == 0
    return pl.pallas_call(
        functools.partial(_gla_body, nctx=nctx),
        grid=(b, B_HEADS // 2),
        in_specs=[pl.BlockSpec((None, nt, LANES), lambda bb, p: (bb, 0, p)),
                  pl.BlockSpec((None, nt, LANES), lambda bb, p: (bb, 0, p)),
                  pl.BlockSpec((None, nt, 2 * B_VDIM), lambda bb, p: (bb, 0, p)),
                  pl.BlockSpec((None, 2 * B_VDIM, nt), lambda bb, p: (bb, p, 0)),
                  pl.BlockSpec((None, nt, LANES), lambda bb, p: (bb, 0, 0)),
                  pl.BlockSpec((2, LANES, LANES), lambda bb, p: (0, 0, p)),
                  pl.BlockSpec((2, 1, LANES), lambda bb, p: (0, 0, p)),
                  pl.BlockSpec((1, B_VDIM), lambda bb, p: (0, 0))],
        out_specs=pl.BlockSpec((None, nt, 2 * B_VDIM), lambda bb, p: (bb, 0, p)),
        out_shape=jax.ShapeDtypeStruct((b, nt, B_HEADS * B_VDIM), F32),
        scratch_shapes=[pltpu.VMEM((nt, 2 * B_VDIM), F32), pltpu.VMEM((2, 2 * B_VDIM, 2 * B_KDIM), F32)],
        compiler_params=_cparams(2),
        name="gla",
    )(bq, bk, bv, bvt, bg, gw, gb, gn)


def _attn_c_body(sink_ref, qt_ref, qn_ref, k_ref, vt_ref, o_ref, s_ref, mx_ref, *, nctx, span, n_tiles):
    j = pl.program_id(1)
    last = pl.num_programs(1) - 1
    nlat = k_ref.shape[0] - nctx
    per_kv = C_HEADS // C_KV_HEADS
    log2e = math.log2(math.e)
    groups = range(C_KV_HEADS)
    ck = TM
    sink = [jnp.concatenate([jnp.full((1, TM), sink_ref[g * per_kv + i] * log2e, F32)
                             for i in range(per_kv)], axis=1) for g in groups]

    def chunks_of(tile):
        if isinstance(tile, int) and tile == 0:
            return [(0, None)]
        q0 = (tile - 1) * TM
        k0 = jnp.clip(q0 - WINDOW, 0, nlat - span)
        start = nctx + k0
        kpos = k0 + lax.broadcasted_iota(jnp.int32, (span, TM), 0)
        qpos = q0 + lax.broadcasted_iota(jnp.int32, (span, TM), 1)
        valid = jnp.abs(qpos - kpos) <= WINDOW
        return [(0, None)] + [(pl.multiple_of(start + off, LANES), valid[off:off + ck])
                              for off in range(0, span, ck)]

    def pass1_ops(tile, q_ref, slot):
        chunks = chunks_of(tile)
        state = {}

        def q_stack(g):
            blocks = []
            for i in range(per_kv):
                h = g * per_kv + i
                qh = q_ref[h * HEAD:(h + 1) * HEAD, :]
                zero = jnp.zeros_like(qh)
                blocks.append(jnp.concatenate([qh, zero] if g == 0 else [zero, qh], axis=0))
            return jnp.concatenate(blocks, axis=1)

        def op(g, c):
            if c == 0:
                state["q", g] = q_stack(g)
            row0, ok = chunks[c]
            s = _dot(k_ref[pl.ds(row0, ck), :], state["q", g])
            if ok is not None:
                s = jnp.where(jnp.concatenate([ok] * per_kv, axis=1), s, NEG_INF)
            s_ref[slot, g, c * ck:(c + 1) * ck, :] = s
            pm = jnp.max(s, axis=0, keepdims=True)
            state["mx", g] = jnp.maximum(sink[g], pm) if c == 0 else jnp.maximum(state["mx", g], pm)
            if c == len(chunks) - 1:
                mx_ref[slot, g] = state["mx", g]

        return [functools.partial(op, g, c) for c in range(len(chunks)) for g in groups]

    def pass2_ops(tile, slot):
        chunks = chunks_of(tile)
        state = {}

        def op(g, c):
            if c == 0:
                state["mx", g] = mx_ref[slot, g]
            e = jnp.exp2(s_ref[slot, g, c * ck:(c + 1) * ck, :] - state["mx", g]).astype(BF16)
            vt = jnp.concatenate([vt_ref[g * HEAD:(g + 1) * HEAD, pl.ds(chunks[c][0], ck)],
                                  jnp.ones((8, ck), BF16)], axis=0)
            pv = _dot(vt, e)
            state["acc", g] = pv if c == 0 else state["acc", g] + pv

        def finish():
            outs = []
            for g in groups:
                acc = state["acc", g]
                o = acc[0:HEAD] / (acc[HEAD:HEAD + 1] + jnp.exp2(sink[g] - state["mx", g]))
                outs += [o[:, i * TM:(i + 1) * TM] for i in range(per_kv)]
            o_ref[...] = jnp.concatenate(outs, axis=0).T.astype(BF16)

        return [functools.partial(op, g, c) for c in range(len(chunks)) for g in groups], finish

    def run(first_ops, second_ops):
        for i in range(max(len(first_ops), len(second_ops))):
            for ops in (first_ops, second_ops):
                if i < len(ops):
                    ops[i]()

    @pl.when(j == 0)
    def _():
        run(pass1_ops(0, qt_ref, 0), [])
        p2, finish = pass2_ops(0, 0)
        run(pass1_ops(1, qn_ref, 1), p2)
        finish()

    for slot in range(2):
        @pl.when(jnp.logical_and(jnp.logical_and(j > 0, j < last), j % 2 == slot))
        def _():
            p2, finish = pass2_ops(j, slot)
            run(pass1_ops(j + 1, qn_ref, 1 - slot), p2)
            finish()

    @pl.when(j == last)
    def _():
        p2, finish = pass2_ops(j, (n_tiles - 1) % 2)
        run([], p2)
        finish()


def _attn_c_call(cqt, ck, cvt, sink, nctx):
    b, nt, kv_width = ck.shape
    width = cqt.shape[1]
    span = TM + 2 * WINDOW
    n_tiles = nt // TM
    assert nt - nctx >= span and nctx == TM and n_tiles >= 2
    lanes = (C_HEADS // C_KV_HEADS) * TM
    return pl.pallas_call(
        functools.partial(_attn_c_body, nctx=nctx, span=span, n_tiles=n_tiles),
        grid=(b, n_tiles),
        in_specs=[pl.BlockSpec(memory_space=pltpu.SMEM),
                  pl.BlockSpec((None, width, TM), lambda bb, j: (bb, 0, j)),
                  pl.BlockSpec((None, width, TM), lambda bb, j: (bb, 0, jnp.minimum(j + 1, n_tiles - 1))),
                  pl.BlockSpec((None, nt, kv_width), lambda bb, j: (bb, 0, 0)),
                  pl.BlockSpec((None, kv_width, nt), lambda bb, j: (bb, 0, 0))],
        out_specs=_tok_spec(width),
        out_shape=jax.ShapeDtypeStruct((b, nt, width), BF16),
        scratch_shapes=[pltpu.VMEM((2, C_KV_HEADS, nctx + span, lanes), F32),
                        pltpu.VMEM((2, C_KV_HEADS, 1, lanes), F32)],
        compiler_params=_cparams(2),
        name="attn_c",
    )(sink, cqt, cqt, ck, cvt)


def _tail_body(x_ref, mod_ref, g1_ref, wg_ref, wa_ref, wb_ref, wc_ref, wo_ref, oa_ref, ob_ref, oc_ref,
               g2_ref, wu_ref, wv_ref, wd_ref, *rest, final_norm):
    o_ref = rest[-1]
    x = x_ref[...]
    d = x.shape[1]
    mod = mod_ref[...]
    hb = _norm_mod(x, g1_ref[...], mod[3:4], mod[4:5]).astype(BF16)
    r = _dot(hb, wg_ref[:, 3 * d:])
    ob = (ob_ref[...] * jax.nn.silu(r)).astype(BF16)
    y = jax.nn.sigmoid(_dot(hb, wg_ref[:, 0:d])) * _dot(oa_ref[...], wa_ref[...])
    y = y + jax.nn.sigmoid(_dot(hb, wg_ref[:, d:2 * d])) * _dot(ob, wb_ref[...])
    y = y + jax.nn.sigmoid(_dot(hb, wg_ref[:, 2 * d:3 * d])) * _dot(oc_ref[...], wc_ref[...])
    x = x + mod[5:6] * _dot(y.astype(BF16), wo_ref[...])
    x = _swiglu_step(x, mod, g2_ref[...], wu_ref, wv_ref, wd_ref, 6)
    if final_norm:
        x = _rms(x) * rest[0][...]
    o_ref[...] = x


def _tail_call(t, mod, g1, wg, wa, wb, wc, wo, oa, ob, oc, g2, wu, wv, wd, ctx_row, skip_tiles=0, final_g=None):
    b, nt, d = t.shape
    n_tiles = nt // TM - skip_tiles
    extra, extra_specs = ([], []) if final_g is None else ([final_g], [_const_spec((1, d))])
    return pl.pallas_call(
        functools.partial(_tail_body, final_norm=final_g is not None),
        grid=(b, n_tiles),
        in_specs=[_tok_spec(d, skip_tiles), _mod_spec(d, ctx_row), _const_spec((1, d)), _const_spec(wg.shape),
                  _const_spec(wa.shape), _const_spec(wb.shape), _const_spec(wc.shape), _const_spec(wo.shape),
                  _tok_spec(oa.shape[2], skip_tiles), _tok_spec(ob.shape[2], skip_tiles),
                  _tok_spec(oc.shape[2], skip_tiles),
                  _const_spec((1, d)), _const_spec(wu.shape), _const_spec(wv.shape), _const_spec(wd.shape)]
        + extra_specs,
        out_specs=_tok_spec(d),
        out_shape=jax.ShapeDtypeStruct((b, n_tiles * TM, d), F32),
        compiler_params=_cparams(2),
        name="tail",
    )(t, mod, g1, wg, wa, wb, wc, wo, oa, ob, oc, g2, wu, wv, wd, *extra)


def _rope_tables(n_lat, nctx):
    rows = n_lat // GRID_W
    row = jnp.repeat(jnp.arange(rows, dtype=F32), GRID_W)
    col = jnp.tile(jnp.arange(GRID_W, dtype=F32), rows)
    n_freq = HEAD // 4
    freqs = jnp.power(ROPE_BASE, -jnp.arange(n_freq, dtype=F32) / n_freq)
    ar = row[:, None] * freqs
    ac = col[:, None] * freqs
    ang = jnp.concatenate([ar, ar, ac, ac], axis=-1)
    ang = jnp.concatenate([ang, ang], axis=-1)
    first = (jnp.arange(LANES) % 32) < 16
    cos, sin = jnp.cos(ang), jnp.sin(ang)
    sin_a = jnp.where(first, -sin, 0.0)
    sin_b = jnp.where(first, 0.0, sin)
    pad = lambda a, v: jnp.concatenate([jnp.full((nctx, LANES), v, F32), a], axis=0)
    return pad(cos, 1.0), pad(sin_a, 0.0), pad(sin_b, 0.0)


def _mixin_weight(w):
    aq, ak, av = w[:, 0:512], w[:, 512:1024], w[:, 1024:1536]
    bq, bk, bv = w[:, 1536:1792], w[:, 1792:2048], w[:, 2048:2560]
    bg = jnp.pad(w[:, 2560:2560 + 2 * GATE_RANK], ((0, 0), (0, LANES - 2 * GATE_RANK)))
    cq, ck, cv = w[:, 3104:3616], w[:, 3616:3744], w[:, 3744:3872]
    return jnp.concatenate([aq, ak, cq, ck, av, bq, bk, bv, bg, cv], axis=1).astype(BF16)


def kernel(x, c, ctx, c_ctx, w_ada, b_ada, norm_g, w_ffn1_in, w_ffn1_out, w_ffn2_in, w_ffn2_out,
           w_mix_in, diff_lambda, diff_subln, gla_gate_w, gla_gate_b, gla_norm, swa_sink,
           w_br_a, w_br_b, w_br_c, w_mix_out, final_g):
    bsz, n, d = x.shape
    nctx = ctx.shape[1]
    depth = w_ada.shape[0]
    ffn = w_ffn1_out.shape[1]
    assert nctx == TM and n % 512 == 0 and d % LANES == 0

    t = (ctx, x)
    ctx_row = bsz
    n_rows = -(-(bsz + 1) // 8) * 8
    cvec = jnp.concatenate([c, c_ctx[None, :], jnp.zeros((n_rows - bsz - 1, d), F32)], axis=0)
    mod_all = _ada_call(cvec, w_ada.astype(BF16), b_ada).reshape(depth, n_rows, N_MOD, d)
    cos, sin_a, sin_b = _rope_tables(n, nctx)
    gate_cols = 2560 + 2 * GATE_RANK
    merge_cols = 3872

    for l in range(depth):
        lam_init = 0.8 - 0.6 * math.exp(-0.3 * l)
        mod = mod_all[l]
        ng = norm_g[l]
        t, aqt, ak, cqt, ck, avt, bq, bk, bv, bg, cvt, bvt = _head_call(
            t, mod, ng[0:1], w_ffn1_in[l, :, :ffn].astype(BF16), w_ffn1_in[l, :, ffn:].astype(BF16),
            w_ffn1_out[l].astype(BF16), ng[1:2], _mixin_weight(w_mix_in[l]), cos, sin_a, sin_b, ctx_row)
        lam_rows = jnp.concatenate([diff_lambda[l], jnp.full((4, HEAD), lam_init, F32)], axis=0)
        oa = _attn_a_call(aqt, ak, avt, lam_rows, diff_subln[l][None, :], nctx)
        gw = jnp.zeros((2, LANES, B_HEADS * B_KDIM), F32)
        for dd in range(2):
            gw = gw.at[dd, dd * GATE_RANK:(dd + 1) * GATE_RANK].set(gla_gate_w[l, dd])
        ob = _gla_call(bq, bk, bv, bvt, bg, gw.astype(BF16), gla_gate_b[l][:, None, :], gla_norm[l][None, :], nctx)
        oc = _attn_c_call(cqt, ck, cvt, swa_sink[l], nctx)
        wg = jnp.concatenate([w_mix_in[l, :, merge_cols:], w_mix_in[l, :, gate_cols:gate_cols + 512]],
                             axis=1).astype(BF16)
        last = l == depth - 1
        tail_row, skip = (None, nctx // TM) if last else (ctx_row, 0)
        t = _tail_call(t, mod, ng[1:2], wg, w_br_a[l].astype(BF16), w_br_b[l].astype(BF16),
                       w_br_c[l].astype(BF16), w_mix_out[l].astype(BF16), oa, ob, oc,
                       ng[2:3], w_ffn2_in[l, :, :ffn].astype(BF16), w_ffn2_in[l, :, ffn:].astype(BF16),
                       w_ffn2_out[l].astype(BF16), tail_row, skip, final_g[None, :] if last else None)
    return t
```

```python
import functools
import math

import jax
import jax.numpy as jnp
from jax import lax
from jax.experimental import pallas as pl
from jax.experimental.pallas import tpu as pltpu

F32 = jnp.float32
BF16 = jnp.bfloat16

HEAD = 64
ROPE_BASE = 10000.0
GRID_W = 64
N_MOD = 9
A_HEADS = 4
A_VROWS = 128 + 16
B_HEADS = 4
B_KDIM = 64
B_VDIM = 128
GATE_RANK = 16
GATE_TAU = 16.0
GLA_CHUNK = 64
C_HEADS = 8
C_KV_HEADS = 2
WINDOW = 128
NEG_INF = -1e30
EPS = 1e-6

TM = 256
LANES = 128
PERIOD_ROWS = 32
VMEM_LIMIT = 56 * 1024 * 1024

_AQ, _AK, _CQ, _CK, _AV, _BQ, _BK, _BV, _BG, _CV, _MIX_COLS = (
    0, 512, 1024, 1536, 1664, 2176, 2432, 2688, 3200, 3328, 3456)


def _cparams(n_grid, vmem=VMEM_LIMIT):
    return pltpu.CompilerParams(dimension_semantics=("arbitrary",) * n_grid, vmem_limit_bytes=vmem)


def _const_spec(shape):
    nd = len(shape)
    return pl.BlockSpec(shape, lambda *_: (0,) * nd, pipeline_mode=pl.Buffered(1))


def _rms(x):
    return x * lax.rsqrt(jnp.mean(x * x, axis=-1, keepdims=True) + EPS)


def _norm_mod(x, g, shift, scale):
    return _rms(x) * g * (1.0 + scale) + shift


def _dot(a, b):
    return jnp.dot(a, b, preferred_element_type=F32)


def _ada_body(c_ref, w_ref, b_ref, o_ref):
    sc = jax.nn.silu(c_ref[...]).astype(BF16)
    o_ref[...] = _dot(sc, w_ref[...]) + b_ref[...]


def _ada_call(cvec, w_ada, b_ada):
    depth, d, n_out = w_ada.shape
    rows = cvec.shape[0]
    tn = n_out // 4
    return pl.pallas_call(
        _ada_body,
        grid=(depth, n_out // tn),
        in_specs=[pl.BlockSpec((rows, d), lambda l, n: (0, 0)),
                  pl.BlockSpec((None, d, tn), lambda l, n: (l, 0, n)),
                  pl.BlockSpec((None, 1, tn), lambda l, n: (l, 0, n))],
        out_specs=pl.BlockSpec((None, rows, tn), lambda l, n: (l, 0, n)),
        out_shape=jax.ShapeDtypeStruct((depth, rows, n_out), F32),
        compiler_params=_cparams(2),
        name="adaln",
    )(cvec, w_ada, b_ada.reshape(depth, 1, n_out))


def _tok_spec(width, skip=0):
    return pl.BlockSpec((None, TM, width), lambda b, j: (b, j + skip, 0))


def _mod_spec(d, ctx_row):
    if ctx_row is None:
        return pl.BlockSpec((None, N_MOD, d), lambda b, j: (b, 0, 0))
    return pl.BlockSpec((None, N_MOD, d), lambda b, j: (jnp.where(j == 0, ctx_row, b), 0, 0))


def _swiglu_step(x, mod, g, wu_ref, wv_ref, wd_ref, mod0):
    hb = _norm_mod(x, g, mod[mod0:mod0 + 1], mod[mod0 + 1:mod0 + 2]).astype(BF16)
    a = (jax.nn.silu(_dot(hb, wu_ref[...])) * _dot(hb, wv_ref[...])).astype(BF16)
    return x + 0.5 * mod[mod0 + 2:mod0 + 3] * _dot(a, wd_ref[...])


def _rope128(x, cos, sin_a, sin_b):
    return x * cos + pltpu.roll(x, LANES - 16, 1) * sin_a + pltpu.roll(x, 16, 1) * sin_b


def _head_body(*refs, split_input):
    refs = list(refs)
    if split_input:
        ctx_ref, lat_ref = refs[0:2]
        x = jnp.where(pl.program_id(1) == 0, ctx_ref[...], lat_ref[...])
        refs = refs[2:]
    else:
        x = refs[0][...]
        refs = refs[1:]
    (mod_ref, g0_ref, wu_ref, wv_ref, wd_ref, g_ref, w_ref, cos_ref, sa_ref, sb_ref, t_ref,
     aqt_ref, ak_ref, cqt_ref, ck_ref, avt_ref, bq_ref, bk_ref, bv_ref, bg_ref, cvt_ref, bvt_ref) = refs
    mod = mod_ref[...]
    x = _swiglu_step(x, mod, g0_ref[...], wu_ref, wv_ref, wd_ref, 0)
    t_ref[...] = x
    hb = _norm_mod(x, g_ref[...], mod[3:4], mod[4:5]).astype(BF16)
    cos, sin_a, sin_b = cos_ref[...], sa_ref[...], sb_ref[...]
    q_scale = HEAD ** -0.5 * math.log2(math.e)

    def proj(lo, hi):
        return _dot(hb, w_ref[:, lo:hi])

    def roped(p, i):
        return _rope128(p[:, i * LANES:(i + 1) * LANES], cos, sin_a, sin_b)

    p_aq, p_ak, p_cqk, p_av = proj(_AQ, _AK), proj(_AK, _CQ), proj(_CQ, _AV), proj(_AV, _BQ)
    for i in range(4):
        lo = i * LANES
        aqt_ref[lo:lo + LANES, :] = (roped(p_aq, i) * q_scale).T.astype(BF16)
        cqt_ref[lo:lo + LANES, :] = (roped(p_cqk, i) * q_scale).T.astype(BF16)
        ak_ref[:, lo:lo + LANES] = roped(p_ak, i).astype(BF16)
        avt_ref[i * A_VROWS:i * A_VROWS + LANES, :] = p_av[:, lo:lo + LANES].T.astype(BF16)
        avt_ref[i * A_VROWS + LANES:(i + 1) * A_VROWS, :] = jnp.ones((A_VROWS - LANES, TM), BF16)
    ck_ref[...] = roped(p_cqk, 4).astype(BF16)
    bq_ref[...] = proj(_BQ, _BK).astype(BF16)
    bk_ref[...] = proj(_BK, _BV).astype(BF16)
    bv = proj(_BV, _BG)
    bv_ref[...] = bv.astype(BF16)
    bvt_ref[...] = bv.T.astype(BF16)
    p_gc = proj(_BG, _MIX_COLS)
    bg_ref[...] = p_gc[:, 0:_CV - _BG].astype(BF16)
    cvt_ref[...] = p_gc[:, _CV - _BG:].T.astype(BF16)


def _head_call(t, mod, g0, wu, wv, wd, g1, w_mix, cos, sin_a, sin_b, ctx_row):
    split_input = isinstance(t, tuple)
    d, f = wu.shape
    if split_input:
        ctx, lat = t
        b, nt = lat.shape[0], ctx.shape[1] + lat.shape[1]
        acts = [ctx, lat]
        act_specs = [pl.BlockSpec((None, TM, d), lambda bb, j: (bb, 0, 0)),
                     pl.BlockSpec((None, TM, d), lambda bb, j: (bb, jnp.maximum(j - 1, 0), 0))]
    else:
        b, nt = t.shape[0], t.shape[1]
        acts, act_specs = [t], [_tok_spec(d)]
    tbl = pl.BlockSpec((TM, LANES), lambda bb, j: (j, 0))

    def tposed(rows):
        return pl.BlockSpec((None, rows, TM), lambda bb, j: (bb, 0, j))

    def sds(*shape):
        return jax.ShapeDtypeStruct(shape, BF16)

    return pl.pallas_call(
        functools.partial(_head_body, split_input=split_input),
        grid=(b, nt // TM),
        in_specs=act_specs + [_mod_spec(d, ctx_row), _const_spec((1, d)),
                              _const_spec((d, f)), _const_spec((d, f)), _const_spec((f, d)),
                              _const_spec((1, d)), _const_spec((d, _MIX_COLS)), tbl, tbl, tbl],
        out_specs=[_tok_spec(d),
                   tposed(512), _tok_spec(512), tposed(512), _tok_spec(128), tposed(A_HEADS * A_VROWS),
                   _tok_spec(256), _tok_spec(256), _tok_spec(512), _tok_spec(128), tposed(128), tposed(512)],
        out_shape=[jax.ShapeDtypeStruct((b, nt, d), F32),
                   sds(b, 512, nt), sds(b, nt, 512), sds(b, 512, nt), sds(b, nt, 128), sds(b, A_HEADS * A_VROWS, nt),
                   sds(b, nt, 256), sds(b, nt, 256), sds(b, nt, 512), sds(b, nt, 128), sds(b, 128, nt),
                   sds(b, 512, nt)],
        compiler_params=_cparams(2),
        name="head",
    )(*acts, mod, g0, wu, wv, wd, g1, w_mix, cos, sin_a, sin_b)


def _attn_a_body(lam_ref, qt_ref, qn_ref, k_ref, vt_ref, sub_ref, o_ref, s_ref, mx_ref, *, nctx, k_chunk):
    j = pl.program_id(1)
    last = pl.num_programs(1) - 1
    nt = k_ref.shape[0]
    row = lax.broadcasted_iota(jnp.int32, (LANES, 1), 0)
    lp = lam_ref[...]
    lam_init = lp[4:5, 0:1]
    lam = (jnp.exp(jnp.sum(lp[0:1] * lp[1:2], axis=-1, keepdims=True))
           - jnp.exp(jnp.sum(lp[2:3] * lp[3:4], axis=-1, keepdims=True)) + lam_init)
    out_gain = sub_ref[0:1, :] * (1.0 - lam_init)
    units = [(h, m) for h in range(A_HEADS) for m in range(2)]
    n_units = len(units)
    all_chunks = [(0, nctx)] + [(lo, k_chunk) for lo in range(nctx, nt, k_chunk)]

    def attend(n_keys, first_ready, prefetch):
        chunks = [c for c in all_chunks if c[0] < n_keys]
        state = {}

        def pass1(u, lo, w):
            nxt = u == n_units
            h, m = units[u % n_units]
            if lo == 0:
                qt = (qn_ref if nxt else qt_ref)[h * LANES:(h + 1) * LANES, :]
                state["q", u] = jnp.where((row < HEAD) if m == 0 else (row >= HEAD), qt, jnp.zeros_like(qt))
            s = _dot(k_ref[lo:lo + w, h * LANES:(h + 1) * LANES], state["q", u])
            s_ref[u % 2, lo:lo + w, :] = s
            pm = jnp.max(s, axis=0, keepdims=True)
            state["mx", u] = pm if lo == 0 else jnp.maximum(state["mx", u], pm)

        def pass2(u, lo, w):
            h, m = units[u]
            e = jnp.exp2(s_ref[u % 2, lo:lo + w, :] - state["mx", u])
            pv = _dot(vt_ref[h * A_VROWS:(h + 1) * A_VROWS, lo:lo + w], e.astype(BF16))
            state["acc", u] = pv if lo == 0 else state["acc", u] + pv

        def finish(u):
            h, m = units[u]
            acc = state.pop(("acc", u))
            res = acc[0:LANES] / acc[LANES:LANES + 1]
            if m == 0:
                state["head", h] = res
            else:
                o = (state.pop(("head", h)) - lam * res).T
                o_ref[:, h * LANES:(h + 1) * LANES] = (_rms(o) * out_gain).astype(BF16)

        if first_ready:
            state["mx", 0] = mx_ref[0:1, :]
        else:
            for lo, w in chunks:
                pass1(0, lo, w)
        for u in range(1, n_units):
            for lo, w in chunks:
                pass1(u, lo, w)
                pass2(u - 1, lo, w)
            finish(u - 1)
        if prefetch:
            for lo, w in all_chunks:
                pass1(n_units, lo, w)
                if (lo, w) in chunks:
                    pass2(n_units - 1, lo, w)
            mx_ref[0:1, :] = state["mx", n_units]
        else:
            for lo, w in chunks:
                pass2(n_units - 1, lo, w)
        finish(n_units - 1)

    @pl.when(j == 0)
    def _():
        attend(nctx, False, True)

    @pl.when(jnp.logical_and(j > 0, j < last))
    def _():
        attend(nt, True, True)

    @pl.when(j == last)
    def _():
        attend(nt, True, False)


def _attn_a_call(aqt, ak, avt, lam_rows, subln, nctx):
    b, nt, width = ak.shape
    k_chunk = 2048
    n_tiles = nt // TM
    assert (nt - nctx) % k_chunk == 0 and n_tiles >= 2
    return pl.pallas_call(
        functools.partial(_attn_a_body, nctx=nctx, k_chunk=k_chunk),
        grid=(b, n_tiles),
        in_specs=[pl.BlockSpec(lam_rows.shape, lambda bb, j: (0, 0)),
                  pl.BlockSpec((None, width, TM), lambda bb, j: (bb, 0, j)),
                  pl.BlockSpec((None, width, TM), lambda bb, j: (bb, 0, jnp.minimum(j + 1, n_tiles - 1))),
                  pl.BlockSpec((None, nt, width), lambda bb, j: (bb, 0, 0)),
                  pl.BlockSpec((None, avt.shape[1], nt), lambda bb, j: (bb, 0, 0)),
                  pl.BlockSpec(subln.shape, lambda bb, j: (0, 0))],
        out_specs=_tok_spec(width),
        out_shape=jax.ShapeDtypeStruct((b, nt, width), BF16),
        scratch_shapes=[pltpu.VMEM((2, nt, TM), F32), pltpu.VMEM((PERIOD_ROWS // 2, TM), F32)],
        compiler_params=_cparams(2),
        name="attn_a",
    )(lam_rows, aqt, aqt, ak, avt, subln)


def _dot_nt(a, b):
    return lax.dot_general(a, b, (((1,), (1,)), ((), ())), preferred_element_type=F32)


def _gla_body(q_ref, k_ref, v_ref, vt_ref, g_ref, gw_ref, gb_ref, gn_ref, o_ref, rev_ref, st_ref, *, nctx):
    nt = q_ref.shape[0]
    ch = GLA_CHUNK
    per_tile = TM // ch
    n_tiles = nt // TM
    n_ctx_tiles = nctx // TM

    row = lax.broadcasted_iota(jnp.int32, (TM, TM), 0)
    col = lax.broadcasted_iota(jnp.int32, (TM, TM), 1)
    same_chunk = (row // ch) == (col // ch)
    tok_chunk = lax.broadcasted_iota(jnp.int32, (TM, 1), 0) // ch
    srow = lax.broadcasted_iota(jnp.int32, (2 * B_VDIM, 2 * B_KDIM), 0)
    scol = lax.broadcasted_iota(jnp.int32, (2 * B_VDIM, 2 * B_KDIM), 1)
    same_head = (srow < B_VDIM) == (scol < B_KDIM)
    lane = lax.broadcasted_iota(jnp.int32, (1, LANES), 1)

    dirs = (0, 1)
    earlier = [jnp.logical_and(same_chunk, (col <= row) if d == 0 else (col >= row)) for d in dirs]
    tri = [jnp.where(e, 1.0, 0.0).astype(BF16) for e in earlier]
    half = [lane < B_KDIM, lane >= B_KDIM]

    out_refs = (o_ref, rev_ref)

    def advance(work):
        n = range(len(work))
        dr = [d for d, _ in work]
        rows = [pl.ds(t * TM if isinstance(t, int) else pl.multiple_of(t * TM, TM), TM) for _, t in work]
        z = [_dot(g_ref[rows[i], :], gw_ref[dr[i]]) + gb_ref[dr[i]] for i in n]
        la = [(jnp.minimum(x, 0.0) - jnp.log(1.0 + jnp.exp(-jnp.abs(x)))) * (1.0 / GATE_TAU) for x in z]
        hi = [x.astype(BF16) for x in la]
        pieces = [jnp.concatenate([hi[i], (la[i] - hi[i].astype(F32)).astype(BF16)], axis=1) for i in n]
        cum = [_dot(tri[dr[i]], pieces[i]) for i in n]
        cum = [x[:, :LANES] + x[:, LANES:] for x in cum]
        ends = [x.reshape(per_tile, ch, LANES) for x in cum]
        ends = [ends[i][:, ch - 1:ch, :] if dr[i] == 0 else ends[i][:, 0:1, :] for i in n]
        tot = [jnp.broadcast_to(x, (per_tile, ch, LANES)).reshape(TM, LANES) for x in ends]
        qf = [q_ref[rows[i], :].astype(F32) for i in n]
        kf = [k_ref[rows[i], :].astype(F32) for i in n]
        v = [v_ref[rows[i], :] for i in n]
        vt = [vt_ref[:, rows[i]] for i in n]
        qe = [qf[i] * (B_KDIM ** -0.5) * jnp.exp(cum[i]) for i in n]
        ke = [(kf[i] * jnp.exp(-cum[i])).astype(BF16) for i in n]
        kd = [(kf[i] * jnp.exp(tot[i] - cum[i])).astype(BF16) for i in n]
        decay = [jnp.exp(x) for x in tot]
        q_h = [[jnp.where(half[h], qe[i], 0.0).astype(BF16) for h in range(2)] for i in n]
        att = [[jnp.where(earlier[dr[i]], _dot_nt(q_h[i][h], ke[i]), 0.0).astype(BF16) for h in range(2)]
               for i in n]
        intra = [[_dot(att[i][h], v[i][:, h * B_VDIM:(h + 1) * B_VDIM]) for h in range(2)] for i in n]
        kd_c = [[jnp.where(tok_chunk == c, kd[i], jnp.zeros_like(kd[i])) for c in range(per_tile)] for i in n]
        inc2 = [[_dot(vt[i], jnp.concatenate(kd_c[i][c:c + 2], axis=1)) for c in range(0, per_tile, 2)]
                for i in n]
        incs = [[jnp.where(same_head, inc2[i][c // 2][:, (c % 2) * LANES:(c % 2 + 1) * LANES], 0.0)
                 for c in range(per_tile)] for i in n]
        state = [st_ref[d] for d in dirs]
        inter = [[None] * per_tile for _ in n]
        for i in n:
            d = dr[i]
            for c in (range(per_tile) if d == 0 else reversed(range(per_tile))):
                inter[i][c] = _dot_nt(qe[i][c * ch:(c + 1) * ch].astype(BF16), state[d].astype(BF16))
                state[d] = state[d] * decay[i][c * ch:c * ch + 1, :] + incs[i][c]
        for d in dirs:
            st_ref[d] = state[d]
        for i in n:
            out_refs[dr[i]][rows[i], :] = jnp.concatenate(intra[i], axis=1) + jnp.concatenate(inter[i], axis=0)

    st_ref[...] = jnp.zeros(st_ref.shape, F32)
    for i in range(n_ctx_tiles):
        advance([(0, i), (1, n_ctx_tiles - 1 - i)])

    def scan(i, carry):
        t_fwd = n_ctx_tiles + 2 * i
        t_rev = n_tiles - 1 - 2 * i
        advance([(0, t_fwd), (1, t_rev), (0, t_fwd + 1), (1, t_rev - 1)])
        return carry

    lax.fori_loop(0, (n_tiles - n_ctx_tiles) // 2, scan, 0)
    gn = gn_ref[...]

    def combine(t, carry):
        rows = pl.ds(pl.multiple_of(t * TM, TM), TM)
        tot = o_ref[rows, :] + rev_ref[rows, :]
        for h in range(2):
            lo = h * B_VDIM
            o_ref[rows, lo:lo + B_VDIM] = _rms(tot[:, lo:lo + B_VDIM]) * gn
        return carry

    lax.fori_loop(0, n_tiles, combine, 0, unroll=True)


def _gla_call(bq, bk, bv, bvt, bg, gw, gb, gn, nctx):
    b, nt, _ = bq.shape
    assert nctx % TM == 0 and (nt - nctx) % (2 * TM) == 0
    return pl.pallas_call(
        functools.partial(_gla_body, nctx=nctx),
        grid=(b, B_HEADS // 2),
        in_specs=[pl.BlockSpec((None, nt, LANES), lambda bb, p: (bb, 0, p)),
                  pl.BlockSpec((None, nt, LANES), lambda bb, p: (bb, 0, p)),
                  pl.BlockSpec((None, nt, 2 * B_VDIM), lambda bb, p: (bb, 0, p)),
                  pl.BlockSpec((None, 2 * B_VDIM, nt), lambda bb, p: (bb, p, 0)),
                  pl.BlockSpec((None, nt, LANES), lambda bb, p: (bb, 0, 0)),
                  pl.BlockSpec((2, LANES, LANES), lambda bb, p: (0, 0, p)),
                  pl.BlockSpec((2, 1, LANES), lambda bb, p: (0, 0, p)),
                  pl.BlockSpec((1, B_VDIM), lambda bb, p: (0, 0))],
        out_specs=pl.BlockSpec((None, nt, 2 * B_VDIM), lambda bb, p: (bb, 0, p)),
        out_shape=jax.ShapeDtypeStruct((b, nt, B_HEADS * B_VDIM), F32),
        scratch_shapes=[pltpu.VMEM((nt, 2 * B_VDIM), F32), pltpu.VMEM((2, 2 * B_VDIM, 2 * B_KDIM), F32)],
        compiler_params=_cparams(2),
        name="gla",
    )(bq, bk, bv, bvt, bg, gw, gb, gn)


def _attn_c_body(sink_ref, qt_ref, qn_ref, k_ref, vt_ref, o_ref, s_ref, mx_ref, *, nctx, span, n_tiles):
    j = pl.program_id(1)
    last = pl.num_programs(1) - 1
    nlat = k_ref.shape[0] - nctx
    per_kv = C_HEADS // C_KV_HEADS
    log2e = math.log2(math.e)
    groups = range(C_KV_HEADS)
    ck = TM
    sink = [jnp.concatenate([jnp.full((1, TM), sink_ref[g * per_kv + i] * log2e, F32)
                             for i in range(per_kv)], axis=1) for g in groups]

    def chunks_of(tile):
        if isinstance(tile, int) and tile == 0:
            return [(0, None)]
        q0 = (tile - 1) * TM
        k0 = jnp.clip(q0 - WINDOW, 0, nlat - span)
        start = nctx + k0
        kpos = k0 + lax.broadcasted_iota(jnp.int32, (span, TM), 0)
        qpos = q0 + lax.broadcasted_iota(jnp.int32, (span, TM), 1)
        valid = jnp.abs(qpos - kpos) <= WINDOW
        return [(0, None)] + [(pl.multiple_of(start + off, LANES), valid[off:off + ck])
                              for off in range(0, span, ck)]

    def pass1_ops(tile, q_ref, slot):
        chunks = chunks_of(tile)
        state = {}

        def q_stack(g):
            blocks = []
            for i in range(per_kv):
                h = g * per_kv + i
                qh = q_ref[h * HEAD:(h + 1) * HEAD, :]
                zero = jnp.zeros_like(qh)
                blocks.append(jnp.concatenate([qh, zero] if g == 0 else [zero, qh], axis=0))
            return jnp.concatenate(blocks, axis=1)

        def op(g, c):
            if c == 0:
                state["q", g] = q_stack(g)
            row0, ok = chunks[c]
            s = _dot(k_ref[pl.ds(row0, ck), :], state["q", g])
            if ok is not None:
                s = jnp.where(jnp.concatenate([ok] * per_kv, axis=1), s, NEG_INF)
            s_ref[slot, g, c * ck:(c + 1) * ck, :] = s
            pm = jnp.max(s, axis=0, keepdims=True)
            state["mx", g] = jnp.maximum(sink[g], pm) if c == 0 else jnp.maximum(state["mx", g], pm)
            if c == len(chunks) - 1:
                mx_ref[slot, g] = state["mx", g]

        return [functools.partial(op, g, c) for c in range(len(chunks)) for g in groups]

    def pass2_ops(tile, slot):
        chunks = chunks_of(tile)
        state = {}

        def op(g, c):
            if c == 0:
                state["mx", g] = mx_ref[slot, g]
            e = jnp.exp2(s_ref[slot, g, c * ck:(c + 1) * ck, :] - state["mx", g]).astype(BF16)
            vt = jnp.concatenate([vt_ref[g * HEAD:(g + 1) * HEAD, pl.ds(chunks[c][0], ck)],
                                  jnp.ones((8, ck), BF16)], axis=0)
            pv = _dot(vt, e)
            state["acc", g] = pv if c == 0 else state["acc", g] + pv

        def finish():
            outs = []
            for g in groups:
                acc = state["acc", g]
                o = acc[0:HEAD] / (acc[HEAD:HEAD + 1] + jnp.exp2(sink[g] - state["mx", g]))
                outs += [o[:, i * TM:(i + 1) * TM] for i in range(per_kv)]
            o_ref[...] = jnp.concatenate(outs, axis=0).T.astype(BF16)

        return [functools.partial(op, g, c) for c in range(len(chunks)) for g in groups], finish

    def run(first_ops, second_ops):
        for i in range(max(len(first_ops), len(second_ops))):
            for ops in (first_ops, second_ops):
                if i < len(ops):
                    ops[i]()

    @pl.when(j == 0)
    def _():
        run(pass1_ops(0, qt_ref, 0), [])
        p2, finish = pass2_ops(0, 0)
        run(pass1_ops(1, qn_ref, 1), p2)
        finish()

    for slot in range(2):
        @pl.when(jnp.logical_and(jnp.logical_and(j > 0, j < last), j % 2 == slot))
        def _():
            p2, finish = pass2_ops(j, slot)
            run(pass1_ops(j + 1, qn_ref, 1 - slot), p2)
            finish()

    @pl.when(j == last)
    def _():
        p2, finish = pass2_ops(j, (n_tiles - 1) % 2)
        run([], p2)
        finish()


def _attn_c_call(cqt, ck, cvt, sink, nctx):
    b, nt, kv_width = ck.shape
    width = cqt.shape[1]
    span = TM + 2 * WINDOW
    n_tiles = nt // TM
    assert nt - nctx >= span and nctx == TM and n_tiles >= 2
    lanes = (C_HEADS // C_KV_HEADS) * TM
    return pl.pallas_call(
        functools.partial(_attn_c_body, nctx=nctx, span=span, n_tiles=n_tiles),
        grid=(b, n_tiles),
        in_specs=[pl.BlockSpec(memory_space=pltpu.SMEM),
                  pl.BlockSpec((None, width, TM), lambda bb, j: (bb, 0, j)),
                  pl.BlockSpec((None, width, TM), lambda bb, j: (bb, 0, jnp.minimum(j + 1, n_tiles - 1))),
                  pl.BlockSpec((None, nt, kv_width), lambda bb, j: (bb, 0, 0)),
                  pl.BlockSpec((None, kv_width, nt), lambda bb, j: (bb, 0, 0))],
        out_specs=_tok_spec(width),
        out_shape=jax.ShapeDtypeStruct((b, nt, width), BF16),
        scratch_shapes=[pltpu.VMEM((2, C_KV_HEADS, nctx + span, lanes), F32),
                        pltpu.VMEM((2, C_KV_HEADS, 1, lanes), F32)],
        compiler_params=_cparams(2),
        name="attn_c",
    )(sink, cqt, cqt, ck, cvt)


def _tail_body(x_ref, mod_ref, g1_ref, wg_ref, wa_ref, wb_ref, wc_ref, wo_ref, oa_ref, ob_ref, oc_ref,
               g2_ref, wu_ref, wv_ref, wd_ref, *rest, final_norm):
    o_ref = rest[-1]
    x = x_ref[...]
    d = x.shape[1]
    mod = mod_ref[...]
    hb = _norm_mod(x, g1_ref[...], mod[3:4], mod[4:5]).astype(BF16)
    r = _dot(hb, wg_ref[:, 3 * d:])
    ob = (ob_ref[...] * jax.nn.silu(r)).astype(BF16)
    y = jax.nn.sigmoid(_dot(hb, wg_ref[:, 0:d])) * _dot(oa_ref[...], wa_ref[...])
    y = y + jax.nn.sigmoid(_dot(hb, wg_ref[:, d:2 * d])) * _dot(ob, wb_ref[...])
    y = y + jax.nn.sigmoid(_dot(hb, wg_ref[:, 2 * d:3 * d])) * _dot(oc_ref[...], wc_ref[...])
    x = x + mod[5:6] * _dot(y.astype(BF16), wo_ref[...])
    x = _swiglu_step(x, mod, g2_ref[...], wu_ref, wv_ref, wd_ref, 6)
    if final_norm:
        x = _rms(x) * rest[0][...]
    o_ref[...] = x


def _tail_call(t, mod, g1, wg, wa, wb, wc, wo, oa, ob, oc, g2, wu, wv, wd, ctx_row, skip_tiles=0, final_g=None):
    b, nt, d = t.shape
    n_tiles = nt // TM - skip_tiles
    extra, extra_specs = ([], []) if final_g is None else ([final_g], [_const_spec((1, d))])
    return pl.pallas_call(
        functools.partial(_tail_body, final_norm=final_g is not None),
        grid=(b, n_tiles),
        in_specs=[_tok_spec(d, skip_tiles), _mod_spec(d, ctx_row), _const_spec((1, d)), _const_spec(wg.shape),
                  _const_spec(wa.shape), _const_spec(wb.shape), _const_spec(wc.shape), _const_spec(wo.shape),
                  _tok_spec(oa.shape[2], skip_tiles), _tok_spec(ob.shape[2], skip_tiles),
                  _tok_spec(oc.shape[2], skip_tiles),
                  _const_spec((1, d)), _const_spec(wu.shape), _const_spec(wv.shape), _const_spec(wd.shape)]
        + extra_specs,
        out_specs=_tok_spec(d),
        out_shape=jax.ShapeDtypeStruct((b, n_tiles * TM, d), F32),
        compiler_params=_cparams(2),
        name="tail",
    )(t, mod, g1, wg, wa, wb, wc, wo, oa, ob, oc, g2, wu, wv, wd, *extra)


def _rope_tables(n_lat, nctx):
    rows = n_lat // GRID_W
    row = jnp.repeat(jnp.arange(rows, dtype=F32), GRID_W)
    col = jnp.tile(jnp.arange(GRID_W, dtype=F32), rows)
    n_freq = HEAD // 4
    freqs = jnp.power(ROPE_BASE, -jnp.arange(n_freq, dtype=F32) / n_freq)
    ar = row[:, None] * freqs
    ac = col[:, None] * freqs
    ang = jnp.concatenate([ar, ar, ac, ac], axis=-1)
    ang = jnp.concatenate([ang, ang], axis=-1)
    first = (jnp.arange(LANES) % 32) < 16
    cos, sin = jnp.cos(ang), jnp.sin(ang)
    sin_a = jnp.where(first, -sin, 0.0)
    sin_b = jnp.where(first, 0.0, sin)
    pad = lambda a, v: jnp.concatenate([jnp.full((nctx, LANES), v, F32), a], axis=0)
    return pad(cos, 1.0), pad(sin_a, 0.0), pad(sin_b, 0.0)


def _mixin_weight(w):
    aq, ak, av = w[:, 0:512], w[:, 512:1024], w[:, 1024:1536]
    bq, bk, bv = w[:, 1536:1792], w[:, 1792:2048], w[:, 2048:2560]
    bg = jnp.pad(w[:, 2560:2560 + 2 * GATE_RANK], ((0, 0), (0, LANES - 2 * GATE_RANK)))
    cq, ck, cv = w[:, 3104:3616], w[:, 3616:3744], w[:, 3744:3872]
    return jnp.concatenate([aq, ak, cq, ck, av, bq, bk, bv, bg, cv], axis=1).astype(BF16)


def kernel(x, c, ctx, c_ctx, w_ada, b_ada, norm_g, w_ffn1_in, w_ffn1_out, w_ffn2_in, w_ffn2_out,
           w_mix_in, diff_lambda, diff_subln, gla_gate_w, gla_gate_b, gla_norm, swa_sink,
           w_br_a, w_br_b, w_br_c, w_mix_out, final_g):
    bsz, n, d = x.shape
    nctx = ctx.shape[1]
    depth = w_ada.shape[0]
    ffn = w_ffn1_out.shape[1]
    assert nctx == TM and n % 512 == 0 and d % LANES == 0

    t = (ctx, x)
    ctx_row = bsz
    n_rows = -(-(bsz + 1) // 8) * 8
    cvec = jnp.concatenate([c, c_ctx[None, :], jnp.zeros((n_rows - bsz - 1, d), F32)], axis=0)
    mod_all = _ada_call(cvec, w_ada.astype(BF16), b_ada).reshape(depth, n_rows, N_MOD, d)
    cos, sin_a, sin_b = _rope_tables(n, nctx)
    gate_cols = 2560 + 2 * GATE_RANK
    merge_cols = 3872

    for l in range(depth):
        lam_init = 0.8 - 0.6 * math.exp(-0.3 * l)
        mod = mod_all[l]
        ng = norm_g[l]
        t, aqt, ak, cqt, ck, avt, bq, bk, bv, bg, cvt, bvt = _head_call(
            t, mod, ng[0:1], w_ffn1_in[l, :, :ffn].astype(BF16), w_ffn1_in[l, :, ffn:].astype(BF16),
            w_ffn1_out[l].astype(BF16), ng[1:2], _mixin_weight(w_mix_in[l]), cos, sin_a, sin_b, ctx_row)
        lam_rows = jnp.zeros((PERIOD_ROWS, LANES), F32).at[0:4, 0:HEAD].set(diff_lambda[l]).at[4].set(lam_init)
        subln = jnp.zeros((PERIOD_ROWS, LANES), F32).at[0].set(diff_subln[l])
        oa = _attn_a_call(aqt, ak, avt, lam_rows, subln, nctx)
        gw = jnp.zeros((2, LANES, B_HEADS * B_KDIM), F32)
        for dd in range(2):
            gw = gw.at[dd, dd * GATE_RANK:(dd + 1) * GATE_RANK].set(gla_gate_w[l, dd])
        ob = _gla_call(bq, bk, bv, bvt, bg, gw.astype(BF16), gla_gate_b[l][:, None, :], gla_norm[l][None, :], nctx)
        oc = _attn_c_call(cqt, ck, cvt, swa_sink[l], nctx)
        wg = jnp.concatenate([w_mix_in[l, :, merge_cols:], w_mix_in[l, :, gate_cols:gate_cols + 512]],
                             axis=1).astype(BF16)
        last = l == depth - 1
        tail_row, skip = (None, nctx // TM) if last else (ctx_row, 0)
        t = _tail_call(t, mod, ng[1:2], wg, w_br_a[l].astype(BF16), w_br_b[l].astype(BF16),
                       w_br_c[l].astype(BF16), w_mix_out[l].astype(BF16), oa, ob, oc,
                       ng[2:3], w_ffn2_in[l, :, :ffn].astype(BF16), w_ffn2_in[l, :, ffn:].astype(BF16),
                       w_ffn2_out[l].astype(BF16), tail_row, skip, final_g[None, :] if last else None)
    return t
```

```python
import functools
import math

import jax
import jax.numpy as jnp
from jax import lax
from jax.experimental import pallas as pl
from jax.experimental.pallas import tpu as pltpu

F32 = jnp.float32
BF16 = jnp.bfloat16

HEAD = 64
ROPE_BASE = 10000.0
GRID_W = 64
N_MOD = 9
A_HEADS = 4
A_VROWS = 128 + 16
B_HEADS = 4
B_KDIM = 64
B_VDIM = 128
GATE_RANK = 16
GATE_TAU = 16.0
GLA_CHUNK = 64
C_HEADS = 8
C_KV_HEADS = 2
WINDOW = 128
NEG_INF = -1e30
EPS = 1e-6

TM = 256
LANES = 128
PERIOD_ROWS = 32
VMEM_LIMIT = 56 * 1024 * 1024

_AQ, _AK, _CQ, _CK, _AV, _BQ, _BK, _BV, _BG, _CV, _MIX_COLS = (
    0, 512, 1024, 1536, 1664, 2176, 2432, 2688, 3200, 3328, 3456)


def _cparams(n_grid, vmem=VMEM_LIMIT):
    return pltpu.CompilerParams(dimension_semantics=("arbitrary",) * n_grid, vmem_limit_bytes=vmem)


def _const_spec(shape):
    nd = len(shape)
    return pl.BlockSpec(shape, lambda *_: (0,) * nd, pipeline_mode=pl.Buffered(1))


def _rms(x):
    return x * lax.rsqrt(jnp.mean(x * x, axis=-1, keepdims=True) + EPS)


def _norm_mod(x, g, shift, scale):
    return _rms(x) * g * (1.0 + scale) + shift


def _dot(a, b):
    return jnp.dot(a, b, preferred_element_type=F32)


def _ada_body(c_ref, w_ref, b_ref, o_ref):
    sc = jax.nn.silu(c_ref[...]).astype(BF16)
    o_ref[...] = _dot(sc, w_ref[...]) + b_ref[...]


def _ada_call(cvec, w_ada, b_ada):
    depth, d, n_out = w_ada.shape
    rows = cvec.shape[0]
    tn = n_out // 4
    return pl.pallas_call(
        _ada_body,
        grid=(depth, n_out // tn),
        in_specs=[pl.BlockSpec((rows, d), lambda l, n: (0, 0)),
                  pl.BlockSpec((None, d, tn), lambda l, n: (l, 0, n)),
                  pl.BlockSpec((None, 1, tn), lambda l, n: (l, 0, n))],
        out_specs=pl.BlockSpec((None, rows, tn), lambda l, n: (l, 0, n)),
        out_shape=jax.ShapeDtypeStruct((depth, rows, n_out), F32),
        compiler_params=_cparams(2),
        name="adaln",
    )(cvec, w_ada, b_ada.reshape(depth, 1, n_out))


def _tok_spec(width, skip=0):
    return pl.BlockSpec((None, TM, width), lambda b, j: (b, j + skip, 0))


def _mod_spec(d, ctx_row):
    if ctx_row is None:
        return pl.BlockSpec((None, N_MOD, d), lambda b, j: (b, 0, 0))
    return pl.BlockSpec((None, N_MOD, d), lambda b, j: (jnp.where(j == 0, ctx_row, b), 0, 0))


def _swiglu_step(x, mod, g, wu_ref, wv_ref, wd_ref, mod0):
    hb = _norm_mod(x, g, mod[mod0:mod0 + 1], mod[mod0 + 1:mod0 + 2]).astype(BF16)
    a = (jax.nn.silu(_dot(hb, wu_ref[...])) * _dot(hb, wv_ref[...])).astype(BF16)
    return x + 0.5 * mod[mod0 + 2:mod0 + 3] * _dot(a, wd_ref[...])


def _rope128(x, cos, sin_a, sin_b):
    return x * cos + pltpu.roll(x, LANES - 16, 1) * sin_a + pltpu.roll(x, 16, 1) * sin_b


def _head_body(*refs, split_input):
    refs = list(refs)
    if split_input:
        ctx_ref, lat_ref = refs[0:2]
        x = jnp.where(pl.program_id(1) == 0, ctx_ref[...], lat_ref[...])
        refs = refs[2:]
    else:
        x = refs[0][...]
        refs = refs[1:]
    (mod_ref, g0_ref, wu_ref, wv_ref, wd_ref, g_ref, w_ref, cos_ref, sa_ref, sb_ref, t_ref,
     aqt_ref, ak_ref, cqt_ref, ck_ref, avt_ref, bq_ref, bk_ref, bv_ref, bg_ref, cvt_ref, bvt_ref) = refs
    mod = mod_ref[...]
    x = _swiglu_step(x, mod, g0_ref[...], wu_ref, wv_ref, wd_ref, 0)
    t_ref[...] = x
    hb = _norm_mod(x, g_ref[...], mod[3:4], mod[4:5]).astype(BF16)
    cos, sin_a, sin_b = cos_ref[...], sa_ref[...], sb_ref[...]
    q_scale = HEAD ** -0.5 * math.log2(math.e)

    def proj(lo, hi):
        return _dot(hb, w_ref[:, lo:hi])

    def roped(p, i):
        return _rope128(p[:, i * LANES:(i + 1) * LANES], cos, sin_a, sin_b)

    p_aq, p_ak, p_cqk, p_av = proj(_AQ, _AK), proj(_AK, _CQ), proj(_CQ, _AV), proj(_AV, _BQ)
    for i in range(4):
        lo = i * LANES
        aqt_ref[lo:lo + LANES, :] = (roped(p_aq, i) * q_scale).T.astype(BF16)
        cqt_ref[lo:lo + LANES, :] = (roped(p_cqk, i) * q_scale).T.astype(BF16)
        ak_ref[:, lo:lo + LANES] = roped(p_ak, i).astype(BF16)
        avt_ref[i * A_VROWS:i * A_VROWS + LANES, :] = p_av[:, lo:lo + LANES].T.astype(BF16)
        avt_ref[i * A_VROWS + LANES:(i + 1) * A_VROWS, :] = jnp.ones((A_VROWS - LANES, TM), BF16)
    ck_ref[...] = roped(p_cqk, 4).astype(BF16)
    bq_ref[...] = proj(_BQ, _BK).astype(BF16)
    bk_ref[...] = proj(_BK, _BV).astype(BF16)
    bv = proj(_BV, _BG)
    bv_ref[...] = bv.astype(BF16)
    bvt_ref[...] = bv.T.astype(BF16)
    p_gc = proj(_BG, _MIX_COLS)
    bg_ref[...] = p_gc[:, 0:_CV - _BG].astype(BF16)
    cvt_ref[...] = p_gc[:, _CV - _BG:].T.astype(BF16)


def _head_call(t, mod, g0, wu, wv, wd, g1, w_mix, cos, sin_a, sin_b, ctx_row):
    split_input = isinstance(t, tuple)
    d, f = wu.shape
    if split_input:
        ctx, lat = t
        b, nt = lat.shape[0], ctx.shape[1] + lat.shape[1]
        acts = [ctx, lat]
        act_specs = [pl.BlockSpec((None, TM, d), lambda bb, j: (bb, 0, 0)),
                     pl.BlockSpec((None, TM, d), lambda bb, j: (bb, jnp.maximum(j - 1, 0), 0))]
    else:
        b, nt = t.shape[0], t.shape[1]
        acts, act_specs = [t], [_tok_spec(d)]
    tbl = pl.BlockSpec((TM, LANES), lambda bb, j: (j, 0))

    def tposed(rows):
        return pl.BlockSpec((None, rows, TM), lambda bb, j: (bb, 0, j))

    def sds(*shape):
        return jax.ShapeDtypeStruct(shape, BF16)

    return pl.pallas_call(
        functools.partial(_head_body, split_input=split_input),
        grid=(b, nt // TM),
        in_specs=act_specs + [_mod_spec(d, ctx_row), _const_spec((1, d)),
                              _const_spec((d, f)), _const_spec((d, f)), _const_spec((f, d)),
                              _const_spec((1, d)), _const_spec((d, _MIX_COLS)), tbl, tbl, tbl],
        out_specs=[_tok_spec(d),
                   tposed(512), _tok_spec(512), tposed(512), _tok_spec(128), tposed(A_HEADS * A_VROWS),
                   _tok_spec(256), _tok_spec(256), _tok_spec(512), _tok_spec(128), tposed(128), tposed(512)],
        out_shape=[jax.ShapeDtypeStruct((b, nt, d), F32),
                   sds(b, 512, nt), sds(b, nt, 512), sds(b, 512, nt), sds(b, nt, 128), sds(b, A_HEADS * A_VROWS, nt),
                   sds(b, nt, 256), sds(b, nt, 256), sds(b, nt, 512), sds(b, nt, 128), sds(b, 128, nt),
                   sds(b, 512, nt)],
        compiler_params=_cparams(2),
        name="head",
    )(*acts, mod, g0, wu, wv, wd, g1, w_mix, cos, sin_a, sin_b)


def _attn_a_body(lam_ref, qt_ref, qn_ref, k_ref, vt_ref, sub_ref, o_ref, s_ref, mx_ref, *, nctx, k_chunk):
    j = pl.program_id(1)
    last = pl.num_programs(1) - 1
    nt = k_ref.shape[0]
    row = lax.broadcasted_iota(jnp.int32, (LANES, 1), 0)
    lp = lam_ref[...]
    lam_init = lp[4:5, 0:1]
    lam = (jnp.exp(jnp.sum(lp[0:1] * lp[1:2], axis=-1, keepdims=True))
           - jnp.exp(jnp.sum(lp[2:3] * lp[3:4], axis=-1, keepdims=True)) + lam_init)
    out_gain = sub_ref[0:1, :] * (1.0 - lam_init)
    units = [(h, m) for h in range(A_HEADS) for m in range(2)]
    n_units = len(units)
    all_chunks = [(0, nctx)] + [(lo, k_chunk) for lo in range(nctx, nt, k_chunk)]

    def attend(n_keys, first_ready, prefetch):
        chunks = [c for c in all_chunks if c[0] < n_keys]
        state = {}

        def pass1(u, lo, w):
            nxt = u == n_units
            h, m = units[u % n_units]
            if lo == 0:
                qt = (qn_ref if nxt else qt_ref)[h * LANES:(h + 1) * LANES, :]
                state["q", u] = jnp.where((row < HEAD) if m == 0 else (row >= HEAD), qt, jnp.zeros_like(qt))
            s = _dot(k_ref[lo:lo + w, h * LANES:(h + 1) * LANES], state["q", u])
            s_ref[u % 2, lo:lo + w, :] = s
            pm = jnp.max(s, axis=0, keepdims=True)
            state["mx", u] = pm if lo == 0 else jnp.maximum(state["mx", u], pm)

        def pass2(u, lo, w):
            h, m = units[u]
            e = jnp.exp2(s_ref[u % 2, lo:lo + w, :] - state["mx", u])
            pv = _dot(vt_ref[h * A_VROWS:(h + 1) * A_VROWS, lo:lo + w], e.astype(BF16))
            state["acc", u] = pv if lo == 0 else state["acc", u] + pv

        def finish(u):
            h, m = units[u]
            acc = state.pop(("acc", u))
            res = acc[0:LANES] / acc[LANES:LANES + 1]
            if m == 0:
                state["head", h] = res
            else:
                o = (state.pop(("head", h)) - lam * res).T
                o_ref[:, h * LANES:(h + 1) * LANES] = (_rms(o) * out_gain).astype(BF16)

        if first_ready:
            state["mx", 0] = mx_ref[0:1, :]
        else:
            for lo, w in chunks:
                pass1(0, lo, w)
        for u in range(1, n_units):
            for lo, w in chunks:
                pass1(u, lo, w)
                pass2(u - 1, lo, w)
            finish(u - 1)
        if prefetch:
            for lo, w in all_chunks:
                pass1(n_units, lo, w)
                if (lo, w) in chunks:
                    pass2(n_units - 1, lo, w)
            mx_ref[0:1, :] = state["mx", n_units]
        else:
            for lo, w in chunks:
                pass2(n_units - 1, lo, w)
        finish(n_units - 1)

    @pl.when(j == 0)
    def _():
        attend(nctx, False, True)

    @pl.when(jnp.logical_and(j > 0, j < last))
    def _():
        attend(nt, True, True)

    @pl.when(j == last)
    def _():
        attend(nt, True, False)


def _attn_a_call(aqt, ak, avt, lam_rows, subln, nctx):
    b, nt, width = ak.shape
    k_chunk = 2048
    n_tiles = nt // TM
    assert (nt - nctx) % k_chunk == 0 and n_tiles >= 2
    return pl.pallas_call(
        functools.partial(_attn_a_body, nctx=nctx, k_chunk=k_chunk),
        grid=(b, n_tiles),
        in_specs=[pl.BlockSpec(lam_rows.shape, lambda bb, j: (0, 0)),
                  pl.BlockSpec((None, width, TM), lambda bb, j: (bb, 0, j)),
                  pl.BlockSpec((None, width, TM), lambda bb, j: (bb, 0, jnp.minimum(j + 1, n_tiles - 1))),
                  pl.BlockSpec((None, nt, width), lambda bb, j: (bb, 0, 0)),
                  pl.BlockSpec((None, avt.shape[1], nt), lambda bb, j: (bb, 0, 0)),
                  pl.BlockSpec(subln.shape, lambda bb, j: (0, 0))],
        out_specs=_tok_spec(width),
        out_shape=jax.ShapeDtypeStruct((b, nt, width), BF16),
        scratch_shapes=[pltpu.VMEM((2, nt, TM), F32), pltpu.VMEM((PERIOD_ROWS // 2, TM), F32)],
        compiler_params=_cparams(2),
        name="attn_a",
    )(lam_rows, aqt, aqt, ak, avt, subln)


def _dot_nt(a, b):
    return lax.dot_general(a, b, (((1,), (1,)), ((), ())), preferred_element_type=F32)


def _gla_body(q_ref, k_ref, v_ref, vt_ref, g_ref, gw_ref, gb_ref, gn_ref, o_ref, rev_ref, st_ref, *, nctx):
    nt = q_ref.shape[0]
    ch = GLA_CHUNK
    per_tile = TM // ch
    n_tiles = nt // TM
    n_ctx_tiles = nctx // TM

    row = lax.broadcasted_iota(jnp.int32, (TM, TM), 0)
    col = lax.broadcasted_iota(jnp.int32, (TM, TM), 1)
    same_chunk = (row // ch) == (col // ch)
    tok_chunk = lax.broadcasted_iota(jnp.int32, (TM, 1), 0) // ch
    srow = lax.broadcasted_iota(jnp.int32, (2 * B_VDIM, 2 * B_KDIM), 0)
    scol = lax.broadcasted_iota(jnp.int32, (2 * B_VDIM, 2 * B_KDIM), 1)
    same_head = (srow < B_VDIM) == (scol < B_KDIM)
    lane = lax.broadcasted_iota(jnp.int32, (1, LANES), 1)

    dirs = (0, 1)
    earlier = [jnp.logical_and(same_chunk, (col <= row) if d == 0 else (col >= row)) for d in dirs]
    tri = [jnp.where(e, 1.0, 0.0).astype(BF16) for e in earlier]
    half = [lane < B_KDIM, lane >= B_KDIM]

    out_refs = (o_ref, rev_ref)

    def advance(work):
        n = range(len(work))
        dr = [d for d, _ in work]
        rows = [pl.ds(t * TM if isinstance(t, int) else pl.multiple_of(t * TM, TM), TM) for _, t in work]
        z = [_dot(g_ref[rows[i], :], gw_ref[dr[i]]) + gb_ref[dr[i]] for i in n]
        la = [(jnp.minimum(x, 0.0) - jnp.log(1.0 + jnp.exp(-jnp.abs(x)))) * (1.0 / GATE_TAU) for x in z]
        hi = [x.astype(BF16) for x in la]
        pieces = [jnp.concatenate([hi[i], (la[i] - hi[i].astype(F32)).astype(BF16)], axis=1) for i in n]
        cum = [_dot(tri[dr[i]], pieces[i]) for i in n]
        cum = [x[:, :LANES] + x[:, LANES:] for x in cum]
        ends = [x.reshape(per_tile, ch, LANES) for x in cum]
        ends = [ends[i][:, ch - 1:ch, :] if dr[i] == 0 else ends[i][:, 0:1, :] for i in n]
        tot = [jnp.broadcast_to(x, (per_tile, ch, LANES)).reshape(TM, LANES) for x in ends]
        qf = [q_ref[rows[i], :].astype(F32) for i in n]
        kf = [k_ref[rows[i], :].astype(F32) for i in n]
        v = [v_ref[rows[i], :] for i in n]
        vt = [vt_ref[:, rows[i]] for i in n]
        qe = [qf[i] * (B_KDIM ** -0.5) * jnp.exp(cum[i]) for i in n]
        ke = [(kf[i] * jnp.exp(-cum[i])).astype(BF16) for i in n]
        kd = [(kf[i] * jnp.exp(tot[i] - cum[i])).astype(BF16) for i in n]
        decay = [jnp.exp(x) for x in tot]
        q_h = [[jnp.where(half[h], qe[i], 0.0).astype(BF16) for h in range(2)] for i in n]
        att = [[jnp.where(earlier[dr[i]], _dot_nt(q_h[i][h], ke[i]), 0.0).astype(BF16) for h in range(2)]
               for i in n]
        intra = [[_dot(att[i][h], v[i][:, h * B_VDIM:(h + 1) * B_VDIM]) for h in range(2)] for i in n]
        kd_c = [[jnp.where(tok_chunk == c, kd[i], jnp.zeros_like(kd[i])) for c in range(per_tile)] for i in n]
        inc2 = [[_dot(vt[i], jnp.concatenate(kd_c[i][c:c + 2], axis=1)) for c in range(0, per_tile, 2)]
                for i in n]
        incs = [[jnp.where(same_head, inc2[i][c // 2][:, (c % 2) * LANES:(c % 2 + 1) * LANES], 0.0)
                 for c in range(per_tile)] for i in n]
        state = [st_ref[d] for d in dirs]
        inter = [[None] * per_tile for _ in n]
        for i in n:
            d = dr[i]
            for c in (range(per_tile) if d == 0 else reversed(range(per_tile))):
                inter[i][c] = _dot_nt(qe[i][c * ch:(c + 1) * ch].astype(BF16), state[d].astype(BF16))
                state[d] = state[d] * decay[i][c * ch:c * ch + 1, :] + incs[i][c]
        for d in dirs:
            st_ref[d] = state[d]
        for i in n:
            out_refs[dr[i]][rows[i], :] = jnp.concatenate(intra[i], axis=1) + jnp.concatenate(inter[i], axis=0)

    st_ref[...] = jnp.zeros(st_ref.shape, F32)
    for i in range(n_ctx_tiles):
        advance([(0, i), (1, n_ctx_tiles - 1 - i)])

    def scan(i, carry):
        t_fwd = n_ctx_tiles + 2 * i
        t_rev = n_tiles - 1 - 2 * i
        advance([(0, t_fwd), (1, t_rev), (0, t_fwd + 1), (1, t_rev - 1)])
        return carry

    lax.fori_loop(0, (n_tiles - n_ctx_tiles) // 2, scan, 0, unroll=2)
    gn = gn_ref[...]

    def combine(t, carry):
        rows = pl.ds(pl.multiple_of(t * TM, TM), TM)
        tot = o_ref[rows, :] + rev_ref[rows, :]
        for h in range(2):
            lo = h * B_VDIM
            o_ref[rows, lo:lo + B_VDIM] = _rms(tot[:, lo:lo + B_VDIM]) * gn
        return carry

    lax.fori_loop(0, n_tiles, combine, 0, unroll=True)


def _gla_call(bq, bk, bv, bvt, bg, gw, gb, gn, nctx):
    b, nt, _ = bq.shape
    assert nctx % TM == 0 and (nt - nctx) % (2 * TM) == 0
    return pl.pallas_call(
        functools.partial(_gla_body, nctx=nctx),
        grid=(b, B_HEADS // 2),
        in_specs=[pl.BlockSpec((None, nt, LANES), lambda bb, p: (bb, 0, p)),
                  pl.BlockSpec((None, nt, LANES), lambda bb, p: (bb, 0, p)),
                  pl.BlockSpec((None, nt, 2 * B_VDIM), lambda bb, p: (bb, 0, p)),
                  pl.BlockSpec((None, 2 * B_VDIM, nt), lambda bb, p: (bb, p, 0)),
                  pl.BlockSpec((None, nt, LANES), lambda bb, p: (bb, 0, 0)),
                  pl.BlockSpec((2, LANES, LANES), lambda bb, p: (0, 0, p)),
                  pl.BlockSpec((2, 1, LANES), lambda bb, p: (0, 0, p)),
                  pl.BlockSpec((1, B_VDIM), lambda bb, p: (0, 0))],
        out_specs=pl.BlockSpec((None, nt, 2 * B_VDIM), lambda bb, p: (bb, 0, p)),
        out_shape=jax.ShapeDtypeStruct((b, nt, B_HEADS * B_VDIM), F32),
        scratch_shapes=[pltpu.VMEM((nt, 2 * B_VDIM), F32), pltpu.VMEM((2, 2 * B_VDIM, 2 * B_KDIM), F32)],
        compiler_params=_cparams(2),
        name="gla",
    )(bq, bk, bv, bvt, bg, gw, gb, gn)


def _attn_c_body(sink_ref, qt_ref, qn_ref, k_ref, vt_ref, o_ref, s_ref, mx_ref, *, nctx, span, n_tiles):
    j = pl.program_id(1)
    last = pl.num_programs(1) - 1
    nlat = k_ref.shape[0] - nctx
    per_kv = C_HEADS // C_KV_HEADS
    log2e = math.log2(math.e)
    groups = range(C_KV_HEADS)
    ck = TM
    sink = [jnp.concatenate([jnp.full((1, TM), sink_ref[g * per_kv + i] * log2e, F32)
                             for i in range(per_kv)], axis=1) for g in groups]

    def chunks_of(tile):
        if isinstance(tile, int) and tile == 0:
            return [(0, None)]
        q0 = (tile - 1) * TM
        k0 = jnp.clip(q0 - WINDOW, 0, nlat - span)
        start = nctx + k0
        kpos = k0 + lax.broadcasted_iota(jnp.int32, (span, TM), 0)
        qpos = q0 + lax.broadcasted_iota(jnp.int32, (span, TM), 1)
        valid = jnp.abs(qpos - kpos) <= WINDOW
        return [(0, None)] + [(pl.multiple_of(start + off, LANES), valid[off:off + ck])
                              for off in range(0, span, ck)]

    def pass1_ops(tile, q_ref, slot):
        chunks = chunks_of(tile)
        state = {}

        def q_stack(g):
            blocks = []
            for i in range(per_kv):
                h = g * per_kv + i
                qh = q_ref[h * HEAD:(h + 1) * HEAD, :]
                zero = jnp.zeros_like(qh)
                blocks.append(jnp.concatenate([qh, zero] if g == 0 else [zero, qh], axis=0))
            return jnp.concatenate(blocks, axis=1)

        def op(g, c):
            if c == 0:
                state["q", g] = q_stack(g)
            row0, ok = chunks[c]
            s = _dot(k_ref[pl.ds(row0, ck), :], state["q", g])
            if ok is not None:
                s = jnp.where(jnp.concatenate([ok] * per_kv, axis=1), s, NEG_INF)
            s_ref[slot, g, c * ck:(c + 1) * ck, :] = s
            pm = jnp.max(s, axis=0, keepdims=True)
            state["mx", g] = jnp.maximum(sink[g], pm) if c == 0 else jnp.maximum(state["mx", g], pm)
            if c == len(chunks) - 1:
                mx_ref[slot, g] = state["mx", g]

        return [functools.partial(op, g, c) for c in range(len(chunks)) for g in groups]

    def pass2_ops(tile, slot):
        chunks = chunks_of(tile)
        state = {}

        def op(g, c):
            if c == 0:
                state["mx", g] = mx_ref[slot, g]
            e = jnp.exp2(s_ref[slot, g, c * ck:(c + 1) * ck, :] - state["mx", g]).astype(BF16)
            vt = jnp.concatenate([vt_ref[g * HEAD:(g + 1) * HEAD, pl.ds(chunks[c][0], ck)],
                                  jnp.ones((8, ck), BF16)], axis=0)
            pv = _dot(vt, e)
            state["acc", g] = pv if c == 0 else state["acc", g] + pv

        def finish():
            outs = []
            for g in groups:
                acc = state["acc", g]
                o = acc[0:HEAD] / (acc[HEAD:HEAD + 1] + jnp.exp2(sink[g] - state["mx", g]))
                outs += [o[:, i * TM:(i + 1) * TM] for i in range(per_kv)]
            o_ref[...] = jnp.concatenate(outs, axis=0).T.astype(BF16)

        return [functools.partial(op, g, c) for c in range(len(chunks)) for g in groups], finish

    def run(first_ops, second_ops):
        for i in range(max(len(first_ops), len(second_ops))):
            for ops in (first_ops, second_ops):
                if i < len(ops):
                    ops[i]()

    @pl.when(j == 0)
    def _():
        run(pass1_ops(0, qt_ref, 0), [])
        p2, finish = pass2_ops(0, 0)
        run(pass1_ops(1, qn_ref, 1), p2)
        finish()

    for slot in range(2):
        @pl.when(jnp.logical_and(jnp.logical_and(j > 0, j < last), j % 2 == slot))
        def _():
            p2, finish = pass2_ops(j, slot)
            run(pass1_ops(j + 1, qn_ref, 1 - slot), p2)
            finish()

    @pl.when(j == last)
    def _():
        p2, finish = pass2_ops(j, (n_tiles - 1) % 2)
        run([], p2)
        finish()


def _attn_c_call(cqt, ck, cvt, sink, nctx):
    b, nt, kv_width = ck.shape
    width = cqt.shape[1]
    span = TM + 2 * WINDOW
    n_tiles = nt // TM
    assert nt - nctx >= span and nctx == TM and n_tiles >= 2
    lanes = (C_HEADS // C_KV_HEADS) * TM
    return pl.pallas_call(
        functools.partial(_attn_c_body, nctx=nctx, span=span, n_tiles=n_tiles),
        grid=(b, n_tiles),
        in_specs=[pl.BlockSpec(memory_space=pltpu.SMEM),
                  pl.BlockSpec((None, width, TM), lambda bb, j: (bb, 0, j)),
                  pl.BlockSpec((None, width, TM), lambda bb, j: (bb, 0, jnp.minimum(j + 1, n_tiles - 1))),
                  pl.BlockSpec((None, nt, kv_width), lambda bb, j: (bb, 0, 0)),
                  pl.BlockSpec((None, kv_width, nt), lambda bb, j: (bb, 0, 0))],
        out_specs=_tok_spec(width),
        out_shape=jax.ShapeDtypeStruct((b, nt, width), BF16),
        scratch_shapes=[pltpu.VMEM((2, C_KV_HEADS, nctx + span, lanes), F32),
                        pltpu.VMEM((2, C_KV_HEADS, 1, lanes), F32)],
        compiler_params=_cparams(2),
        name="attn_c",
    )(sink, cqt, cqt, ck, cvt)


def _tail_body(x_ref, mod_ref, g1_ref, wg_ref, wa_ref, wb_ref, wc_ref, wo_ref, oa_ref, ob_ref, oc_ref,
               g2_ref, wu_ref, wv_ref, wd_ref, *rest, final_norm):
    o_ref = rest[-1]
    x = x_ref[...]
    d = x.shape[1]
    mod = mod_ref[...]
    hb = _norm_mod(x, g1_ref[...], mod[3:4], mod[4:5]).astype(BF16)
    r = _dot(hb, wg_ref[:, 3 * d:])
    ob = (ob_ref[...] * jax.nn.silu(r)).astype(BF16)
    y = jax.nn.sigmoid(_dot(hb, wg_ref[:, 0:d])) * _dot(oa_ref[...], wa_ref[...])
    y = y + jax.nn.sigmoid(_dot(hb, wg_ref[:, d:2 * d])) * _dot(ob, wb_ref[...])
    y = y + jax.nn.sigmoid(_dot(hb, wg_ref[:, 2 * d:3 * d])) * _dot(oc_ref[...], wc_ref[...])
    x = x + mod[5:6] * _dot(y.astype(BF16), wo_ref[...])
    x = _swiglu_step(x, mod, g2_ref[...], wu_ref, wv_ref, wd_ref, 6)
    if final_norm:
        x = _rms(x) * rest[0][...]
    o_ref[...] = x


def _tail_call(t, mod, g1, wg, wa, wb, wc, wo, oa, ob, oc, g2, wu, wv, wd, ctx_row, skip_tiles=0, final_g=None):
    b, nt, d = t.shape
    n_tiles = nt // TM - skip_tiles
    extra, extra_specs = ([], []) if final_g is None else ([final_g], [_const_spec((1, d))])
    return pl.pallas_call(
        functools.partial(_tail_body, final_norm=final_g is not None),
        grid=(b, n_tiles),
        in_specs=[_tok_spec(d, skip_tiles), _mod_spec(d, ctx_row), _const_spec((1, d)), _const_spec(wg.shape),
                  _const_spec(wa.shape), _const_spec(wb.shape), _const_spec(wc.shape), _const_spec(wo.shape),
                  _tok_spec(oa.shape[2], skip_tiles), _tok_spec(ob.shape[2], skip_tiles),
                  _tok_spec(oc.shape[2], skip_tiles),
                  _const_spec((1, d)), _const_spec(wu.shape), _const_spec(wv.shape), _const_spec(wd.shape)]
        + extra_specs,
        out_specs=_tok_spec(d),
        out_shape=jax.ShapeDtypeStruct((b, n_tiles * TM, d), F32),
        compiler_params=_cparams(2),
        name="tail",
    )(t, mod, g1, wg, wa, wb, wc, wo, oa, ob, oc, g2, wu, wv, wd, *extra)


def _rope_tables(n_lat, nctx):
    rows = n_lat // GRID_W
    row = jnp.repeat(jnp.arange(rows, dtype=F32), GRID_W)
    col = jnp.tile(jnp.arange(GRID_W, dtype=F32), rows)
    n_freq = HEAD // 4
    freqs = jnp.power(ROPE_BASE, -jnp.arange(n_freq, dtype=F32) / n_freq)
    ar = row[:, None] * freqs
    ac = col[:, None] * freqs
    ang = jnp.concatenate([ar, ar, ac, ac], axis=-1)
    ang = jnp.concatenate([ang, ang], axis=-1)
    first = (jnp.arange(LANES) % 32) < 16
    cos, sin = jnp.cos(ang), jnp.sin(ang)
    sin_a = jnp.where(first, -sin, 0.0)
    sin_b = jnp.where(first, 0.0, sin)
    pad = lambda a, v: jnp.concatenate([jnp.full((nctx, LANES), v, F32), a], axis=0)
    return pad(cos, 1.0), pad(sin_a, 0.0), pad(sin_b, 0.0)


def _mixin_weight(w):
    aq, ak, av = w[:, 0:512], w[:, 512:1024], w[:, 1024:1536]
    bq, bk, bv = w[:, 1536:1792], w[:, 1792:2048], w[:, 2048:2560]
    bg = jnp.pad(w[:, 2560:2560 + 2 * GATE_RANK], ((0, 0), (0, LANES - 2 * GATE_RANK)))
    cq, ck, cv = w[:, 3104:3616], w[:, 3616:3744], w[:, 3744:3872]
    return jnp.concatenate([aq, ak, cq, ck, av, bq, bk, bv, bg, cv], axis=1).astype(BF16)


def kernel(x, c, ctx, c_ctx, w_ada, b_ada, norm_g, w_ffn1_in, w_ffn1_out, w_ffn2_in, w_ffn2_out,
           w_mix_in, diff_lambda, diff_subln, gla_gate_w, gla_gate_b, gla_norm, swa_sink,
           w_br_a, w_br_b, w_br_c, w_mix_out, final_g):
    bsz, n, d = x.shape
    nctx = ctx.shape[1]
    depth = w_ada.shape[0]
    ffn = w_ffn1_out.shape[1]
    assert nctx == TM and n % 512 == 0 and d % LANES == 0

    t = (ctx, x)
    ctx_row = bsz
    n_rows = -(-(bsz + 1) // 8) * 8
    cvec = jnp.concatenate([c, c_ctx[None, :], jnp.zeros((n_rows - bsz - 1, d), F32)], axis=0)
    mod_all = _ada_call(cvec, w_ada.astype(BF16), b_ada).reshape(depth, n_rows, N_MOD, d)
    cos, sin_a, sin_b = _rope_tables(n, nctx)
    gate_cols = 2560 + 2 * GATE_RANK
    merge_cols = 3872

    for l in range(depth):
        lam_init = 0.8 - 0.6 * math.exp(-0.3 * l)
        mod = mod_all[l]
        ng = norm_g[l]
        t, aqt, ak, cqt, ck, avt, bq, bk, bv, bg, cvt, bvt = _head_call(
            t, mod, ng[0:1], w_ffn1_in[l, :, :ffn].astype(BF16), w_ffn1_in[l, :, ffn:].astype(BF16),
            w_ffn1_out[l].astype(BF16), ng[1:2], _mixin_weight(w_mix_in[l]), cos, sin_a, sin_b, ctx_row)
        lam_rows = jnp.zeros((PERIOD_ROWS, LANES), F32).at[0:4, 0:HEAD].set(diff_lambda[l]).at[4].set(lam_init)
        subln = jnp.zeros((PERIOD_ROWS, LANES), F32).at[0].set(diff_subln[l])
        oa = _attn_a_call(aqt, ak, avt, lam_rows, subln, nctx)
        gw = jnp.zeros((2, LANES, B_HEADS * B_KDIM), F32)
        for dd in range(2):
            gw = gw.at[dd, dd * GATE_RANK:(dd + 1) * GATE_RANK].set(gla_gate_w[l, dd])
        ob = _gla_call(bq, bk, bv, bvt, bg, gw.astype(BF16), gla_gate_b[l][:, None, :], gla_norm[l][None, :], nctx)
        oc = _attn_c_call(cqt, ck, cvt, swa_sink[l], nctx)
        wg = jnp.concatenate([w_mix_in[l, :, merge_cols:], w_mix_in[l, :, gate_cols:gate_cols + 512]],
                             axis=1).astype(BF16)
        last = l == depth - 1
        tail_row, skip = (None, nctx // TM) if last else (ctx_row, 0)
        t = _tail_call(t, mod, ng[1:2], wg, w_br_a[l].astype(BF16), w_br_b[l].astype(BF16),
                       w_br_c[l].astype(BF16), w_mix_out[l].astype(BF16), oa, ob, oc,
                       ng[2:3], w_ffn2_in[l, :, :ffn].astype(BF16), w_ffn2_in[l, :, ffn:].astype(BF16),
                       w_ffn2_out[l].astype(BF16), tail_row, skip, final_g[None, :] if last else None)
    return t
```
